```python
import jax, jax.numpy as jnp
from jax import lax
import numpy as np

D_MODEL = 1024
BATCH = 4
SEQ = 4096
DEPTH = 1
DEC_BATCH = 128
DEC_SEQ = 4
PAST_LEN = 2048
PAGE_SIZE = 128

HA = 12
DA = 64
EA = HA * DA
PATTERNS = ((128, 1), (512, 4), (2048, 16))
WINDOW_MAX = 2048
QB = 128
EB = 768
CONV_W = 31
HC = 4
DC = 128
EC = HC * DC
N_MEM = 256
N_BUCKETS = 32
MAX_DIST = 2048
EPS = 1e-6
NEG = -1e30
SPLITS = (EA, 2 * EA, 3 * EA, 4 * EA, 4 * EA + 2 * EB, 4 * EA + 3 * EB,
          4 * EA + 3 * EB + EC, 4 * EA + 3 * EB + 2 * EC)
IN_COLS = 4 * EA + 3 * EB + 2 * EC + 3 * D_MODEL

kernel_name = 'hybrid_dilated_conv_memory_decoder_step'


def _rmsnorm(x, g):
    x32 = x.astype(jnp.float32)
    y = x32 * lax.rsqrt(jnp.mean(x32 * x32, axis=-1, keepdims=True) + EPS)
    return (y * g.astype(jnp.float32)).astype(x.dtype)


def _layernorm(x, g, b):
    x32 = x.astype(jnp.float32)
    mu = jnp.mean(x32, axis=-1, keepdims=True)
    var = jnp.mean(jnp.square(x32 - mu), axis=-1, keepdims=True)
    y = (x32 - mu) * lax.rsqrt(var + EPS)
    return (y * g.astype(jnp.float32) + b.astype(jnp.float32)).astype(x.dtype)


def _t5_buckets(n):
    exact = N_BUCKETS // 2
    nf = np.maximum(n, 1).astype(np.float32)
    scale = np.float32(N_BUCKETS - exact) / np.log(np.float32(MAX_DIST) / np.float32(exact))
    large = exact + (np.log(nf / np.float32(exact)) * scale).astype(np.int32)
    large = np.minimum(large, N_BUCKETS - 1)
    return np.where(n < exact, n, large).astype(np.int32)


def _pattern_bias(rel_bias, d, nw):
    idx = _t5_buckets(np.arange(nw + 1, dtype=np.int32) * d)
    return rel_bias[jnp.asarray(idx)].T


def _dilated_band(q, k, v, bias, d, nw):
    B, S, H, Dh = q.shape
    L = S // d
    N = B * d

    def to_res(t):
        return t.reshape(B, L, d, H, Dh).transpose(0, 2, 1, 3, 4).reshape(N, L, H, Dh)

    qr, kr, vr = to_res(q), to_res(k), to_res(v)
    nb = -(-L // QB)
    Lp = nb * QB
    qb = jnp.pad(qr, ((0, 0), (0, Lp - L), (0, 0), (0, 0))).reshape(N, nb, QB, H, Dh)
    kp = jnp.pad(kr, ((0, 0), (QB, Lp - L), (0, 0), (0, 0))).reshape(N, nb + 1, QB, H, Dh)
    vp = jnp.pad(vr, ((0, 0), (QB, Lp - L), (0, 0), (0, 0))).reshape(N, nb + 1, QB, H, Dh)
    kb = jnp.concatenate([kp[:, :-1], kp[:, 1:]], axis=2)
    vb = jnp.concatenate([vp[:, :-1], vp[:, 1:]], axis=2)
    iq = jnp.arange(QB)[:, None]
    ik = jnp.arange(2 * QB)[None, :]
    dist = iq - ik + QB
    band = (dist >= 0) & (dist <= nw)
    blk = jnp.arange(nb)[:, None, None]
    valid = band[None] & ((blk > 0) | (ik >= QB)[None])
    s = jnp.einsum('nbqhd,nbkhd->nbhqk', qb, kb).astype(jnp.float32)
    s = s + bias[:, jnp.clip(dist, 0, nw)].astype(jnp.float32)
    s = jnp.where(valid[None, :, None], s, NEG)
    m = jnp.max(s, axis=-1, keepdims=True)
    p = jnp.exp(s - m)
    l = jnp.sum(p, axis=-1, keepdims=True)
    o = jnp.einsum('nbhqk,nbkhd->nbqhd', (p / l).astype(v.dtype), vb)
    lse = (m + jnp.log(l))[..., 0]
    o = o.reshape(N, Lp, H, Dh)[:, :L]
    lse = lse.transpose(0, 1, 3, 2).reshape(N, Lp, H)[:, :L]
    o = o.reshape(B, d, L, H, Dh).transpose(0, 2, 1, 3, 4).reshape(B, S, H, Dh)
    lse = lse.reshape(B, d, L, H).transpose(0, 2, 1, 3).reshape(B, S, H)
    return o, lse


def _dilated_gather(q, k_all, v_all, bias, d, nw, wb, past):
    T = q.shape[1]
    i = jnp.arange(T)[:, None]
    j = jnp.arange(nw + 1)[None, :]
    kpos = past + i - j * d
    idx = wb + i - j * d
    valid = (kpos >= 0) & (idx >= 0)
    idx = jnp.maximum(idx, 0)
    kg = k_all[:, idx]
    vg = v_all[:, idx]
    s = jnp.einsum('bthd,btjhd->bthj', q, kg).astype(jnp.float32)
    s = s + bias.astype(jnp.float32)
    s = jnp.where(valid[None, :, None, :], s, NEG)
    m = jnp.max(s, axis=-1, keepdims=True)
    p = jnp.exp(s - m)
    l = jnp.sum(p, axis=-1, keepdims=True)
    o = jnp.einsum('bthj,btjhd->bthd', (p / l).astype(v_all.dtype), vg)
    return o, (m + jnp.log(l))[..., 0]


def _combine(outs, lses):
    w = jax.nn.softmax(jnp.stack(lses), axis=0)
    return jnp.einsum('gbshd,gbsh->bshd', jnp.stack(outs), w.astype(outs[0].dtype))


def _mem_kv(mem, g_mem, w_mem_kv):
    B = mem.shape[0]
    kv = _rmsnorm(mem, g_mem) @ w_mem_kv
    k, v = jnp.split(kv, 2, axis=-1)
    return k.reshape(B, N_MEM, HC, DC), v.reshape(B, N_MEM, HC, DC)


def _layer(x, mem_k, mem_v, hist_k, hist_v, hist_conv, past,
           rel_bias, g_pre, w_in, conv_w, conv_b, ln_g, ln_b,
           w_pa, w_pb, w_pc, w_out, g_post):
    Bx, S, _ = x.shape
    h = _rmsnorm(x, g_pre) @ w_in
    qa, ka, va, za, ub, zb, qc, zc, gt = jnp.split(h, SPLITS, axis=-1)
    qa = qa.reshape(Bx, S, HA, DA) * (DA ** -0.5)
    ka = ka.reshape(Bx, S, HA, DA)
    va = va.reshape(Bx, S, HA, DA)

    outs, lses = [], []
    if hist_k is None:
        for (w, d) in PATTERNS:
            nw = w // d
            o, lse = _dilated_band(qa, ka, va, _pattern_bias(rel_bias, d, nw), d, nw)
            outs.append(o)
            lses.append(lse)
        wbp = min(WINDOW_MAX, S)
        k_rows, v_rows = ka[:, S - wbp:], va[:, S - wbp:]
    else:
        wb = hist_k.shape[1]
        k_all = jnp.concatenate([hist_k, ka], axis=1)
        v_all = jnp.concatenate([hist_v, va], axis=1)
        for (w, d) in PATTERNS:
            nw = w // d
            o, lse = _dilated_gather(qa, k_all, v_all, _pattern_bias(rel_bias, d, nw), d, nw, wb, past)
            outs.append(o)
            lses.append(lse)
        k_rows, v_rows = ka, va
    oa = _combine(outs, lses).reshape(Bx, S, EA)

    glu = ub[..., :EB] * jax.nn.sigmoid(ub[..., EB:])
    if hist_conv is None:
        hist_conv = jnp.zeros((Bx, CONV_W - 1, EB), glu.dtype)
    full = jnp.concatenate([hist_conv, glu], axis=1)
    conv = lax.conv_general_dilated(full, conv_w[:, None, :], window_strides=(1,), padding='VALID',
                                    dimension_numbers=('NWC', 'WIO', 'NWC'),
                                    feature_group_count=EB) + conv_b
    ob = jax.nn.silu(_layernorm(conv, ln_g, ln_b))
    conv_state = full[:, -(CONV_W - 1):]

    qc = qc.reshape(Bx, S, HC, DC) * (DC ** -0.5)
    sc = jnp.einsum('bshd,bmhd->bhsm', qc, mem_k).astype(jnp.float32)
    pc = jax.nn.softmax(sc, axis=-1).astype(mem_v.dtype)
    oc = jnp.einsum('bhsm,bmhd->bshd', pc, mem_v).reshape(Bx, S, EC)

    gates = jax.nn.sigmoid(gt).reshape(Bx, S, 3, D_MODEL)
    mix = (gates[..., 0, :] * ((oa * jax.nn.silu(za)) @ w_pa)
           + gates[..., 1, :] * ((ob * jax.nn.silu(zb)) @ w_pb)
           + gates[..., 2, :] * ((oc * jax.nn.silu(zc)) @ w_pc))
    y = x + _rmsnorm(mix @ w_out, g_post)
    return y, k_rows, v_rows, conv_state


def setup_inputs(seed: int = 0) -> dict:
    key = jax.random.key(seed)
    ks = jax.random.split(key, 24)
    f32 = jnp.float32
    wb = min(WINDOW_MAX, PAST_LEN)

    def nrm(k, shape, scale=1.0):
        return jax.random.normal(k, shape, f32) * scale

    return {
        'x_prompt': nrm(ks[0], (BATCH, SEQ, D_MODEL)),
        'x_sample': nrm(ks[1], (DEC_BATCH, DEC_SEQ, D_MODEL)),
        'mem_prompt': nrm(ks[2], (BATCH, N_MEM, D_MODEL)),
        'cache_k_win': nrm(ks[3], (DEPTH, DEC_BATCH, wb, HA, DA)),
        'cache_v_win': nrm(ks[4], (DEPTH, DEC_BATCH, wb, HA, DA)),
        'state_conv': nrm(ks[5], (DEPTH, DEC_BATCH, CONV_W - 1, EB), 0.5),
        'cache_k_mem': nrm(ks[6], (DEPTH, DEC_BATCH, N_MEM, HC, DC)),
        'cache_v_mem': nrm(ks[7], (DEPTH, DEC_BATCH, N_MEM, HC, DC)),
        'rel_bias': nrm(ks[8], (N_BUCKETS, HA), 0.5),
        'g_pre': 1.0 + nrm(ks[9], (DEPTH, D_MODEL), 0.05),
        'w_in': nrm(ks[10], (DEPTH, D_MODEL, IN_COLS), D_MODEL ** -0.5),
        'g_mem': 1.0 + nrm(ks[11], (DEPTH, D_MODEL), 0.05),
        'w_mem_kv': nrm(ks[12], (DEPTH, D_MODEL, 2 * EC), D_MODEL ** -0.5),
        'conv_w': nrm(ks[13], (DEPTH, CONV_W, EB), CONV_W ** -0.5),
        'conv_b': nrm(ks[14], (DEPTH, EB), 0.02),
        'ln_g': 1.0 + nrm(ks[15], (DEPTH, EB), 0.05),
        'ln_b': nrm(ks[16], (DEPTH, EB), 0.02),
        'w_proj_a': nrm(ks[17], (DEPTH, EA, D_MODEL), EA ** -0.5),
        'w_proj_b': nrm(ks[18], (DEPTH, EB, D_MODEL), EB ** -0.5),
        'w_proj_c': nrm(ks[19], (DEPTH, EC, D_MODEL), EC ** -0.5),
        'w_out': nrm(ks[20], (DEPTH, D_MODEL, D_MODEL), D_MODEL ** -0.5),
        'g_post': 1.0 + nrm(ks[21], (DEPTH, D_MODEL), 0.05),
    }


def reference(x_prompt, x_sample, mem_prompt, cache_k_win, cache_v_win, state_conv,
              cache_k_mem, cache_v_mem, rel_bias, g_pre, w_in, g_mem, w_mem_kv,
              conv_w, conv_b, ln_g, ln_b, w_proj_a, w_proj_b, w_proj_c, w_out, g_post):
    y_p, y_s = x_prompt, x_sample
    kw_p, vw_p, cv_p, km_p, vm_p = [], [], [], [], []
    kw_s, vw_s, cv_s = [], [], []
    for l in range(DEPTH):
        w = (rel_bias, g_pre[l], w_in[l], conv_w[l], conv_b[l], ln_g[l], ln_b[l],
             w_proj_a[l], w_proj_b[l], w_proj_c[l], w_out[l], g_post[l])
        mk, mv = _mem_kv(mem_prompt, g_mem[l], w_mem_kv[l])
        y_p, k_r, v_r, c_r = _layer(y_p, mk, mv, None, None, None, 0, *w)
        kw_p.append(k_r)
        vw_p.append(v_r)
        cv_p.append(c_r)
        km_p.append(mk)
        vm_p.append(mv)
        y_s, k_r, v_r, c_r = _layer(y_s, cache_k_mem[l], cache_v_mem[l], cache_k_win[l],
                                    cache_v_win[l], state_conv[l], PAST_LEN, *w)
        kw_s.append(k_r)
        vw_s.append(v_r)
        cv_s.append(c_r)
    return (y_p, y_s, jnp.stack(kw_p), jnp.stack(vw_p), jnp.stack(cv_p), jnp.stack(km_p),
            jnp.stack(vm_p), jnp.stack(kw_s), jnp.stack(vw_s), jnp.stack(cv_s))
```

```python
import functools

import jax
import jax.numpy as jnp
import numpy as np
from jax import lax
from jax.experimental import pallas as pl
from jax.experimental.pallas import tpu as pltpu

F32 = jnp.float32
BF16 = jnp.bfloat16

D_MODEL = 1024
HA, DA = 12, 64
EA = HA * DA
NHP = HA // 2
PATTERNS = ((128, 1), (512, 4), (2048, 16))
NW = 128
QB = 128
EB = 768
CONV_W = 31
HC, DC = 4, 128
EC = HC * DC
N_MEM = 256
N_BUCKETS = 32
MAX_DIST = 2048
EPS = 1e-6
NEG = -1e30

C_Q, C_K, C_V, C_ZA = 0, EA, 2 * EA, 3 * EA
C_U = 4 * EA
C_G = C_U + EB
C_ZB = C_U + 2 * EB
C_QC = C_ZB + EB
C_ZC = C_QC + EC
C_GT = C_ZC + EC
IN_COLS = C_GT + 3 * D_MODEL

V7X_VMEM_LIMIT = 56 * 1024 * 1024
HALO = 32
QPAD = 16
CPAD = 8


def _t5_bucket_ids(n):
    exact = N_BUCKETS // 2
    nf = np.maximum(n, 1).astype(np.float32)
    scale = np.float32(N_BUCKETS - exact) / np.log(np.float32(MAX_DIST) / np.float32(exact))
    large = exact + (np.log(nf / np.float32(exact)) * scale).astype(np.int32)
    large = np.minimum(large, N_BUCKETS - 1)
    return np.where(n < exact, n, large).astype(np.int32)


def _sigmoid(x):
    return 1.0 / (1.0 + jnp.exp(-x))


def _silu(x):
    return x * _sigmoid(x)


def _rmsnorm_f32(x, g):
    return x * lax.rsqrt(jnp.mean(x * x, axis=-1, keepdims=True) + EPS) * g


def _params(*sem):
    return pltpu.CompilerParams(dimension_semantics=sem, vmem_limit_bytes=V7X_VMEM_LIMIT)


def _inproj_prompt_kernel(x_ref, g_ref, w_ref, qkv_ref, kf_ref, vf_ref, za_ref, glu_ref, zb_ref,
                          qc_ref, zc_ref, gt_ref, tail_ref):
    xn = _rmsnorm_f32(x_ref[0], g_ref[...]).astype(BF16)

    def mm(lo, n):
        return jnp.dot(xn, w_ref[:, lo:lo + n], preferred_element_type=F32)

    def put_pairs(res, base):
        for c in range(NHP):
            qkv_ref[0, base + c] = res[:, c * 128:(c + 1) * 128].astype(BF16)

    put_pairs(mm(C_Q, EA) * (DA ** -0.5), 0)
    k = mm(C_K, EA)
    put_pairs(k, NHP)
    kf_ref[0] = k
    v = mm(C_V, EA)
    put_pairs(v, 2 * NHP)
    vf_ref[0] = v
    za_ref[0] = mm(C_ZA, EA).astype(BF16)
    glu = mm(C_U, EB) * _sigmoid(mm(C_G, EB))
    glu_ref[0] = glu.astype(BF16)
    tail_ref[0] = glu[glu.shape[0] - HALO:, :]
    zb_ref[0] = mm(C_ZB, EB).astype(BF16)
    qc_ref[0] = (mm(C_QC, EC) * (DC ** -0.5)).astype(BF16)
    zc_ref[0] = mm(C_ZC, EC).astype(BF16)
    for c in range(3):
        gt_ref[0, :, c * D_MODEL:(c + 1) * D_MODEL] = mm(C_GT + c * D_MODEL, D_MODEL).astype(BF16)


def _inproj_prompt(x, g, w, tm):
    B, S, _ = x.shape
    nt = S // tm
    wb = min(MAX_DIST, S)
    first_kept = (S - wb) // tm
    row = lambda b, i: (b, i, 0)
    kept = lambda b, i: (b, jnp.maximum(i - first_kept, 0), 0)
    out_shape = (
        jax.ShapeDtypeStruct((B, 3 * NHP, S, 128), BF16),
        jax.ShapeDtypeStruct((B, wb, EA), F32),
        jax.ShapeDtypeStruct((B, wb, EA), F32),
        jax.ShapeDtypeStruct((B, S, EA), BF16),
        jax.ShapeDtypeStruct((B, S, EB), BF16),
        jax.ShapeDtypeStruct((B, S, EB), BF16),
        jax.ShapeDtypeStruct((B, S, EC), BF16),
        jax.ShapeDtypeStruct((B, S, EC), BF16),
        jax.ShapeDtypeStruct((B, S, 3 * D_MODEL), BF16),
        jax.ShapeDtypeStruct((B, HALO, EB), F32),
    )
    out_specs = (
        pl.BlockSpec((1, 3 * NHP, tm, 128), lambda b, i: (b, 0, i, 0)),
        pl.BlockSpec((1, tm, EA), kept),
        pl.BlockSpec((1, tm, EA), kept),
        pl.BlockSpec((1, tm, EA), row),
        pl.BlockSpec((1, tm, EB), row),
        pl.BlockSpec((1, tm, EB), row),
        pl.BlockSpec((1, tm, EC), row),
        pl.BlockSpec((1, tm, EC), row),
        pl.BlockSpec((1, tm, 3 * D_MODEL), row),
        pl.BlockSpec((1, HALO, EB), lambda b, i: (b, 0, 0)),
    )
    return pl.pallas_call(
        _inproj_prompt_kernel,
        out_shape=out_shape,
        grid=(B, nt),
        in_specs=[
            pl.BlockSpec((1, tm, D_MODEL), row),
            pl.BlockSpec((1, D_MODEL), lambda b, i: (0, 0)),
            pl.BlockSpec((D_MODEL, IN_COLS), lambda b, i: (0, 0), pipeline_mode=pl.Buffered(1)),
        ],
        out_specs=out_specs,
        compiler_params=_params("arbitrary", "arbitrary"),
        name="inproj_prompt",
    )(x, g, w)


def _inproj_sample_kernel(x_ref, g_ref, w_ref, q_ref, k_ref, v_ref, za_ref, glu_ref, zb_ref,
                          qc_ref, zc_ref, gt_ref):
    xn = _rmsnorm_f32(x_ref[...], g_ref[...]).astype(BF16)

    def mm(lo, n):
        return jnp.dot(xn, w_ref[:, lo:lo + n], preferred_element_type=F32)

    q_ref[...] = mm(C_Q, EA) * (DA ** -0.5)
    k_ref[...] = mm(C_K, EA)
    v_ref[...] = mm(C_V, EA)
    za_ref[...] = mm(C_ZA, EA).astype(BF16)
    glu_ref[...] = mm(C_U, EB) * _sigmoid(mm(C_G, EB))
    zb_ref[...] = mm(C_ZB, EB).astype(BF16)
    qc_ref[...] = mm(C_QC, EC) * (DC ** -0.5)
    zc_ref[...] = mm(C_ZC, EC).astype(BF16)
    for c in range(3):
        gt_ref[:, c * D_MODEL:(c + 1) * D_MODEL] = mm(C_GT + c * D_MODEL, D_MODEL).astype(BF16)


def _inproj_sample(x, g, w, tm):
    M = x.shape[0]
    row = lambda i: (i, 0)
    widths = (EA, EA, EA, EA, EB, EB, EC, EC, 3 * D_MODEL)
    dtypes = (F32, F32, F32, BF16, F32, BF16, F32, BF16, BF16)
    return pl.pallas_call(
        _inproj_sample_kernel,
        out_shape=tuple(jax.ShapeDtypeStruct((M, n), dt) for n, dt in zip(widths, dtypes)),
        grid=(M // tm,),
        in_specs=[
            pl.BlockSpec((tm, D_MODEL), row),
            pl.BlockSpec((1, D_MODEL), lambda i: (0, 0)),
            pl.BlockSpec((D_MODEL, IN_COLS), lambda i: (0, 0), pipeline_mode=pl.Buffered(1)),
        ],
        out_specs=tuple(pl.BlockSpec((tm, n), row) for n in widths),
        compiler_params=_params("arbitrary"),
        name="inproj_sample",
    )(x, g, w)


def _memkv_kernel(m_ref, g_ref, w_ref, kf_ref, vf_ref, kb_ref, vb_ref):
    xn = _rmsnorm_f32(m_ref[...], g_ref[...]).astype(BF16)
    k = jnp.dot(xn, w_ref[:, :EC], preferred_element_type=F32)
    v = jnp.dot(xn, w_ref[:, EC:], preferred_element_type=F32)
    kf_ref[...] = k
    vf_ref[...] = v
    kb_ref[...] = k.astype(BF16)
    vb_ref[...] = v.astype(BF16)


def _memkv(mem, g, w, tm):
    M = mem.shape[0]
    row = lambda i: (i, 0)
    return pl.pallas_call(
        _memkv_kernel,
        out_shape=(jax.ShapeDtypeStruct((M, EC), F32), jax.ShapeDtypeStruct((M, EC), F32),
                   jax.ShapeDtypeStruct((M, EC), BF16), jax.ShapeDtypeStruct((M, EC), BF16)),
        grid=(M // tm,),
        in_specs=[pl.BlockSpec((tm, D_MODEL), row),
                  pl.BlockSpec((1, D_MODEL), lambda i: (0, 0)),
                  pl.BlockSpec((D_MODEL, 2 * EC), lambda i: (0, 0))],
        out_specs=tuple(pl.BlockSpec((tm, EC), row) for _ in range(4)),
        compiler_params=_params("arbitrary"),
        name="memkv",
    )(mem, g, w)


def _band_tables(rel_bias, d):
    iq = np.arange(QB)[:, None]
    ik = np.arange(2 * QB)[None, :]
    dist = iq - ik + QB
    band = (dist >= 0) & (dist <= NW)
    idx = _t5_bucket_ids(np.clip(dist, 0, NW) * d)
    bias = rel_bias[jnp.asarray(idx)]
    bias = jnp.transpose(bias, (2, 0, 1)).astype(F32)
    later = jnp.where(jnp.asarray(band)[None], bias, NEG)
    first = jnp.where(jnp.asarray(band & (ik >= QB))[None], bias, NEG)
    t = jnp.stack([first, later], axis=1)
    return t.reshape(NHP, 2, 2, QB, 2 * QB).transpose(0, 2, 1, 3, 4).reshape(NHP, 2, 2 * QB, 2 * QB)


def _band_attn_kernel(q_ref, k_ref, v_ref, t_ref, o_ref, lse_ref, *, d, L):
    lane = lax.broadcasted_iota(jnp.int32, (QB, 128), 1)
    head0 = lane < DA
    keep0 = head0.astype(F32).astype(BF16)
    keep1 = (1.0 - head0.astype(F32)).astype(BF16)
    for r in range(d):
        cs = slice(r * 128, (r + 1) * 128)

        def body(qb, carry, cs=cs):
            row = pl.multiple_of(qb * QB, QB)
            prow = pl.multiple_of(jnp.maximum(qb - 1, 0) * QB, QB)
            q2 = q_ref[0, 0, pl.ds(row, QB), cs]
            kk = jnp.concatenate([k_ref[0, 0, pl.ds(prow, QB), cs],
                                  k_ref[0, 0, pl.ds(row, QB), cs]], axis=0)
            vv = jnp.concatenate([v_ref[0, 0, pl.ds(prow, QB), cs],
                                  v_ref[0, 0, pl.ds(row, QB), cs]], axis=0)
            qs = jnp.concatenate([q2 * keep0, q2 * keep1], axis=0)
            s = lax.dot_general(qs, kk, (((1,), (1,)), ((), ())), preferred_element_type=F32)
            s = s + t_ref[0, jnp.minimum(qb, 1)]
            m = jnp.max(s, axis=-1, keepdims=True)
            p = jnp.exp(s - m)
            l = jnp.sum(p, axis=-1, keepdims=True)
            o = jnp.dot(p.astype(BF16), vv, preferred_element_type=F32) / l
            lse = m + jnp.log(l)
            o_ref[0, 0, pl.ds(row, QB), cs] = jnp.where(head0, o[:QB], o[QB:]).astype(BF16)
            lse_ref[0, 0, pl.ds(row, QB), cs] = jnp.where(
                head0, jnp.broadcast_to(lse[:QB], (QB, 128)), jnp.broadcast_to(lse[QB:], (QB, 128)))
            return carry

        lax.fori_loop(0, L // QB, body, 0)


def _band_attn(qkv, tables, d):
    B, _, S, _ = qkv.shape
    L = S // d
    view = qkv.reshape(B, 3 * NHP, L, d * 128)
    blk = (1, 1, L, d * 128)
    o, lse = pl.pallas_call(
        functools.partial(_band_attn_kernel, d=d, L=L),
        out_shape=(jax.ShapeDtypeStruct((B, NHP, L, d * 128), BF16),
                   jax.ShapeDtypeStruct((B, NHP, L, d * 128), F32)),
        grid=(B, NHP),
        in_specs=[pl.BlockSpec(blk, lambda b, h: (b, h, 0, 0)),
                  pl.BlockSpec(blk, lambda b, h: (b, NHP + h, 0, 0)),
                  pl.BlockSpec(blk, lambda b, h: (b, 2 * NHP + h, 0, 0)),
                  pl.BlockSpec((1, 2, 2 * QB, 2 * QB), lambda b, h: (h, 0, 0, 0))],
        out_specs=(pl.BlockSpec(blk, lambda b, h: (b, h, 0, 0)),
                   pl.BlockSpec(blk, lambda b, h: (b, h, 0, 0))),
        compiler_params=_params("arbitrary", "arbitrary"),
        name=f"band_attn_d{d}",
    )(view, view, view, tables)
    return o.reshape(B, NHP, S, 128), lse.reshape(B, NHP, S, 128)


def _layernorm_silu(c, g, b):
    mu = jnp.mean(c, axis=-1, keepdims=True)
    cc = c - mu
    var = jnp.mean(cc * cc, axis=-1, keepdims=True)
    return _silu(cc * lax.rsqrt(var + EPS) * g + b)


def _conv_prompt_kernel(glu_ref, halo_ref, zb_ref, cw_ref, cb_ref, lg_ref, lb_ref, o_ref, full_ref,
                        *, tm, rc):
    i = pl.program_id(1)
    halo = halo_ref[0].astype(F32)
    full_ref[0, 0:HALO, :] = jnp.where(i == 0, jnp.zeros_like(halo), halo)
    full_ref[0, HALO:HALO + tm, :] = glu_ref[0].astype(F32)
    n_shift = HALO + tm - 8
    for s in range(1, 8):
        full_ref[s, 0:n_shift, :] = full_ref[0, s:s + n_shift, :]
    first_tap = HALO - (CONV_W - 1)

    def chunk(c, carry):
        base = pl.multiple_of(c * rc, rc)
        acc = jnp.zeros((rc, EB), F32)
        for j in range(CONV_W):
            off = first_tap + j
            acc = acc + full_ref[off % 8, pl.ds(base + (off // 8) * 8, rc), :] * cw_ref[j:j + 1, :]
        ob = _layernorm_silu(acc + cb_ref[...], lg_ref[...], lb_ref[...])
        zb = zb_ref[0, pl.ds(base, rc), :].astype(F32)
        o_ref[0, pl.ds(base, rc), :] = (ob * _silu(zb)).astype(BF16)
        return carry

    lax.fori_loop(0, tm // rc, chunk, 0)


def _conv_prompt(glu, zb, cw, cb, lg, lb, tm, rc):
    B, S, _ = glu.shape
    row = lambda b, i: (b, i, 0)
    vec = lambda b, i: (0, 0)
    return pl.pallas_call(
        functools.partial(_conv_prompt_kernel, tm=tm, rc=rc),
        out_shape=jax.ShapeDtypeStruct((B, S, EB), BF16),
        grid=(B, S // tm),
        in_specs=[pl.BlockSpec((1, tm, EB), row),
                  pl.BlockSpec((1, HALO, EB), lambda b, i: (b, jnp.maximum(i * (tm // HALO) - 1, 0), 0)),
                  pl.BlockSpec((1, tm, EB), row),
                  pl.BlockSpec((CONV_W, EB), vec),
                  pl.BlockSpec((1, EB), vec), pl.BlockSpec((1, EB), vec), pl.BlockSpec((1, EB), vec)],
        out_specs=pl.BlockSpec((1, tm, EB), row),
        scratch_shapes=[pltpu.VMEM((8, HALO + tm, EB), F32)],
        compiler_params=_params("arbitrary", "arbitrary"),
        name="conv_prompt",
    )(glu, glu, zb, cw, cb, lg, lb)


def _gated_tail(a_g, b_g, c_g, gt_ref, x, wpa_ref, wpb_ref, wpc_ref, wout_ref, gpost_ref):
    def gate(k):
        return _sigmoid(gt_ref[:, k * D_MODEL:(k + 1) * D_MODEL].astype(F32))

    mix = gate(0) * jnp.dot(a_g, wpa_ref[...], preferred_element_type=F32)
    mix = mix + gate(1) * jnp.dot(b_g, wpb_ref[...], preferred_element_type=F32)
    mix = mix + gate(2) * jnp.dot(c_g, wpc_ref[...], preferred_element_type=F32)
    z = jnp.dot(mix.astype(BF16), wout_ref[...], preferred_element_type=F32)
    return x + _rmsnorm_f32(z, gpost_ref[...])


def _tail_prompt_kernel(o1_ref, o4_ref, o16_ref, l1_ref, l4_ref, l16_ref, za_ref, obg_ref,
                        qc_ref, zc_ref, mk_ref, mv_ref, gt_ref, x_ref,
                        wpa_ref, wpb_ref, wpc_ref, wout_ref, gpost_ref, y_ref):
    pieces = []
    for c in range(NHP):
        l1, l4, l16 = l1_ref[0, c], l4_ref[0, c], l16_ref[0, c]
        mx = jnp.maximum(jnp.maximum(l1, l4), l16)
        e1, e4, e16 = jnp.exp(l1 - mx), jnp.exp(l4 - mx), jnp.exp(l16 - mx)
        oa = (o1_ref[0, c].astype(F32) * e1 + o4_ref[0, c].astype(F32) * e4
              + o16_ref[0, c].astype(F32) * e16) / (e1 + e4 + e16)
        za = za_ref[0, :, c * 128:(c + 1) * 128].astype(F32)
        pieces.append((oa * _silu(za)).astype(BF16))
    a_g = jnp.concatenate(pieces, axis=-1)
    pieces = []
    for h in range(HC):
        hs = slice(h * DC, (h + 1) * DC)
        s = lax.dot_general(qc_ref[0, :, hs], mk_ref[0, :, hs], (((1,), (1,)), ((), ())),
                            preferred_element_type=F32)
        p = jnp.exp(s - jnp.max(s, axis=-1, keepdims=True))
        l = jnp.sum(p, axis=-1, keepdims=True)
        oc = jnp.dot(p.astype(BF16), mv_ref[0, :, hs], preferred_element_type=F32) / l
        pieces.append((oc * _silu(zc_ref[0, :, hs].astype(F32))).astype(BF16))
    c_g = jnp.concatenate(pieces, axis=-1)
    y_ref[0] = _gated_tail(a_g, obg_ref[0], c_g, gt_ref.at[0], x_ref[0],
                           wpa_ref, wpb_ref, wpc_ref, wout_ref, gpost_ref)


def _tail_prompt(os_, ls_, za, obg, qc, zc, mk, mv, gt, x, wpa, wpb, wpc, wout, gpost, tm):
    B, S, _ = x.shape
    row = lambda b, i: (b, i, 0)
    pair = lambda b, i: (b, 0, i, 0)
    const = lambda b, i: (0, 0)
    pair_spec = pl.BlockSpec((1, NHP, tm, 128), pair)
    return pl.pallas_call(
        _tail_prompt_kernel,
        out_shape=jax.ShapeDtypeStruct((B, S, D_MODEL), F32),
        grid=(B, S // tm),
        in_specs=[pair_spec] * 6 + [
            pl.BlockSpec((1, tm, EA), row), pl.BlockSpec((1, tm, EB), row),
            pl.BlockSpec((1, tm, EC), row), pl.BlockSpec((1, tm, EC), row),
            pl.BlockSpec((1, N_MEM, EC), lambda b, i: (b, 0, 0)),
            pl.BlockSpec((1, N_MEM, EC), lambda b, i: (b, 0, 0)),
            pl.BlockSpec((1, tm, 3 * D_MODEL), row), pl.BlockSpec((1, tm, D_MODEL), row),
            pl.BlockSpec((EA, D_MODEL), const), pl.BlockSpec((EB, D_MODEL), const),
            pl.BlockSpec((EC, D_MODEL), const), pl.BlockSpec((D_MODEL, D_MODEL), const),
            pl.BlockSpec((1, D_MODEL), const)],
        out_specs=pl.BlockSpec((1, tm, D_MODEL), row),
        compiler_params=_params("arbitrary", "arbitrary"),
        name="tail_prompt",
    )(*os_, *ls_, za, obg, qc, zc, mk, mv, gt, x, wpa, wpb, wpc, wout, gpost)


def _tail_sample_kernel(oa_ref, za_ref, ob_ref, zb_ref, oc_ref, zc_ref, gt_ref, x_ref,
                        wpa_ref, wpb_ref, wpc_ref, wout_ref, gpost_ref, y_ref):
    a_g = (oa_ref[...] * _silu(za_ref[...].astype(F32))).astype(BF16)
    b_g = (ob_ref[...] * _silu(zb_ref[...].astype(F32))).astype(BF16)
    c_g = (oc_ref[...] * _silu(zc_ref[...].astype(F32))).astype(BF16)
    y_ref[...] = _gated_tail(a_g, b_g, c_g, gt_ref, x_ref[...],
                             wpa_ref, wpb_ref, wpc_ref, wout_ref, gpost_ref)


def _tail_sample(oa, za, ob, zb, oc, zc, gt, x, wpa, wpb, wpc, wout, gpost, tm):
    M = x.shape[0]
    row = lambda i: (i, 0)
    const = lambda i: (0, 0)
    widths = (EA, EA, EB, EB, EC, EC, 3 * D_MODEL, D_MODEL)
    return pl.pallas_call(
        _tail_sample_kernel,
        out_shape=jax.ShapeDtypeStruct((M, D_MODEL), F32),
        grid=(M // tm,),
        in_specs=[pl.BlockSpec((tm, n), row) for n in widths] + [
            pl.BlockSpec((EA, D_MODEL), const), pl.BlockSpec((EB, D_MODEL), const),
            pl.BlockSpec((EC, D_MODEL), const), pl.BlockSpec((D_MODEL, D_MODEL), const),
            pl.BlockSpec((1, D_MODEL), const)],
        out_specs=pl.BlockSpec((tm, D_MODEL), row),
        compiler_params=_params("arbitrary"),
        name="tail_sample",
    )(oa, za, ob, zb, oc, zc, gt, x, wpa, wpb, wpc, wout, gpost)


def _sample_tables(rel_bias, wb, past, T, n_tail, n_far):
    n_cols = n_tail + 16
    i = np.arange(T)[:, None]
    r = np.arange(n_cols)[None, :]
    dl = n_tail + i - r
    real = (r < n_tail + T) & (dl >= 0) & (past + i - dl >= 0) & (wb + i - dl >= 0)
    mult = np.zeros(dl.shape, np.float32)
    for (w, d) in PATTERNS:
        mult += (real & (dl % d == 0) & (dl <= w)).astype(np.float32)
    idx = _t5_bucket_ids(np.clip(dl, 0, MAX_DIST))
    bias = jnp.transpose(rel_bias[jnp.asarray(idx)], (0, 2, 1)).astype(F32)
    bias = jnp.pad(bias, ((0, 0), (0, QPAD - HA), (0, 0)))
    tail_b = jnp.where(jnp.asarray(mult > 0)[:, None, :], bias, NEG).reshape(T * QPAD, n_cols)
    tail_m = jnp.asarray(np.repeat(mult, QPAD, axis=0))
    g = np.arange(n_far)
    dist_far = wb - 16 * g
    far = rel_bias[jnp.asarray(_t5_bucket_ids(dist_far))].T.astype(F32)
    far_b = jnp.tile(jnp.pad(far, ((0, QPAD - HA), (0, 0))), (T, 1))
    return tail_b, tail_m, far_b


def _sample_attn_kernel(q_ref, kt_ref, kf_ref, kn_ref, vt_ref, vf_ref, vn_ref, tb_ref, tm_ref, fb_ref,
                        qc_ref, mk_ref, mv_ref, oa_ref, oc_ref, *, T, n_far):
    rowi = lax.broadcasted_iota(jnp.int32, (QPAD, EA), 0)
    lanei = lax.broadcasted_iota(jnp.int32, (QPAD, EA), 1)
    hmask = ((lanei >= rowi * DA) & (lanei < (rowi + 1) * DA)).astype(F32)
    qbd = jnp.concatenate([jnp.broadcast_to(q_ref[0, i:i + 1, :], (QPAD, EA)) * hmask
                           for i in range(T)], axis=0).astype(BF16)
    pad = jnp.zeros((16 - T, EA), F32)
    k_all = jnp.concatenate([kt_ref[0].astype(BF16),
                             jnp.concatenate([kn_ref[0], pad], axis=0).astype(BF16)], axis=0)
    v_all = jnp.concatenate([vt_ref[0].astype(BF16),
                             jnp.concatenate([vn_ref[0], pad], axis=0).astype(BF16)], axis=0)
    nt = (((1,), (1,)), ((), ()))
    s_t = lax.dot_general(qbd, k_all, nt, preferred_element_type=F32) + tb_ref[...]
    s_f = jnp.concatenate(
        [lax.dot_general(qbd[i * QPAD:(i + 1) * QPAD], kf_ref[0, :, i * EA:(i + 1) * EA].astype(BF16),
                         nt, preferred_element_type=F32) for i in range(T)], axis=0) + fb_ref[...]
    m = jnp.maximum(jnp.max(s_t, axis=-1, keepdims=True), jnp.max(s_f, axis=-1, keepdims=True))
    p_t = jnp.exp(s_t - m) * tm_ref[...]
    p_f = jnp.exp(s_f - m)
    l = jnp.sum(p_t, axis=-1, keepdims=True) + jnp.sum(p_f, axis=-1, keepdims=True)
    o = jnp.dot(p_t.astype(BF16), v_all, preferred_element_type=F32)
    o = o + jnp.concatenate(
        [jnp.dot(p_f[i * QPAD:(i + 1) * QPAD].astype(BF16), vf_ref[0, :, i * EA:(i + 1) * EA].astype(BF16),
                 preferred_element_type=F32) for i in range(T)], axis=0)
    o = o / l
    oa_ref[0] = jnp.concatenate(
        [jnp.sum(o[i * QPAD:(i + 1) * QPAD] * hmask, axis=0, keepdims=True) for i in range(T)], axis=0)
    rowc = lax.broadcasted_iota(jnp.int32, (CPAD, EC), 0)
    lanec = lax.broadcasted_iota(jnp.int32, (CPAD, EC), 1)
    cmask = ((lanec >= rowc * DC) & (lanec < (rowc + 1) * DC)).astype(F32)
    qcb = jnp.concatenate([jnp.broadcast_to(qc_ref[0, i:i + 1, :], (CPAD, EC)) * cmask
                           for i in range(T)], axis=0).astype(BF16)
    sc = lax.dot_general(qcb, mk_ref[0].astype(BF16), nt, preferred_element_type=F32)
    pc = jnp.exp(sc - jnp.max(sc, axis=-1, keepdims=True))
    lc = jnp.sum(pc, axis=-1, keepdims=True)
    ocf = jnp.dot(pc.astype(BF16), mv_ref[0].astype(BF16), preferred_element_type=F32) / lc
    oc_ref[0] = jnp.concatenate(
        [jnp.sum(ocf[i * CPAD:(i + 1) * CPAD] * cmask, axis=0, keepdims=True) for i in range(T)], axis=0)


def _sample_attn(q, k_new, v_new, cache_k, cache_v, tables, qc, mem_k, mem_v, n_tail, n_far):
    Bd, T, _ = q.shape
    wb = cache_k.shape[1]
    tail_b, tail_m, far_b = tables
    n_cols = n_tail + 16
    ck_g = cache_k.reshape(Bd, wb // 16, 16 * EA)
    cv_g = cache_v.reshape(Bd, wb // 16, 16 * EA)
    tail_spec = pl.BlockSpec((1, n_tail, EA), lambda b: (b, wb // n_tail - 1, 0))
    far_spec = pl.BlockSpec((1, n_far, T * EA), lambda b: (b, 0, 0))
    new_spec = pl.BlockSpec((1, T, EA), lambda b: (b, 0, 0))
    const = lambda b: (0, 0)
    return pl.pallas_call(
        functools.partial(_sample_attn_kernel, T=T, n_far=n_far),
        out_shape=(jax.ShapeDtypeStruct((Bd, T, EA), F32), jax.ShapeDtypeStruct((Bd, T, EC), F32)),
        grid=(Bd,),
        in_specs=[new_spec, tail_spec, far_spec, new_spec, tail_spec, far_spec, new_spec,
                  pl.BlockSpec((T * QPAD, n_cols), const), pl.BlockSpec((T * QPAD, n_cols), const),
                  pl.BlockSpec((T * QPAD, n_far), const),
                  pl.BlockSpec((1, T, EC), lambda b: (b, 0, 0)),
                  pl.BlockSpec((1, N_MEM, EC), lambda b: (b, 0, 0)),
                  pl.BlockSpec((1, N_MEM, EC), lambda b: (b, 0, 0))],
        out_specs=(pl.BlockSpec((1, T, EA), lambda b: (b, 0, 0)),
                   pl.BlockSpec((1, T, EC), lambda b: (b, 0, 0))),
        compiler_params=_params("arbitrary"),
        name="sample_attn",
    )(q, cache_k, ck_g, k_new, cache_v, cv_g, v_new, tail_b, tail_m, far_b, qc, mem_k, mem_v)


def _conv_sample_kernel(st_ref, glu_ref, cw_ref, cb_ref, lg_ref, lb_ref, o_ref, full_ref, *, G, T):
    def one(b, carry):
        full_ref[0:CONV_W - 1, :] = st_ref[b]
        full_ref[CONV_W - 1:CONV_W - 1 + T, :] = glu_ref[b]
        rows = [jnp.sum(full_ref[i:i + CONV_W, :] * cw_ref[...], axis=0, keepdims=True)
                for i in range(T)]
        conv = jnp.concatenate(rows, axis=0) + cb_ref[...]
        o_ref[b] = _layernorm_silu(conv, lg_ref[...], lb_ref[...])
        return carry

    lax.fori_loop(0, G, one, 0)


def _conv_sample(state, glu, cw, cb, lg, lb, G):
    Bd, _, _ = state.shape
    T = glu.shape[1]
    grp = lambda i: (i, 0, 0)
    vec = lambda i: (0, 0)
    return pl.pallas_call(
        functools.partial(_conv_sample_kernel, G=G, T=T),
        out_shape=jax.ShapeDtypeStruct((Bd, T, EB), F32),
        grid=(Bd // G,),
        in_specs=[pl.BlockSpec((G, CONV_W - 1, EB), grp), pl.BlockSpec((G, T, EB), grp),
                  pl.BlockSpec((CONV_W, EB), vec),
                  pl.BlockSpec((1, EB), vec), pl.BlockSpec((1, EB), vec), pl.BlockSpec((1, EB), vec)],
        out_specs=pl.BlockSpec((G, T, EB), grp),
        scratch_shapes=[pltpu.VMEM((CONV_W - 1 + 16, EB), F32)],
        compiler_params=_params("arbitrary"),
        name="conv_sample",
    )(state, glu, cw, cb, lg, lb)


def kernel(x_prompt, x_sample, mem_prompt, cache_k_win, cache_v_win, state_conv, cache_k_mem, cache_v_mem,
           rel_bias, g_pre, w_in, g_mem, w_mem_kv, conv_w, conv_b, ln_g, ln_b, w_proj_a, w_proj_b,
           w_proj_c, w_out, g_post):
    depth = g_pre.shape[0]
    assert depth == 1, "single-layer step"
    B, S, _ = x_prompt.shape
    Bd, T, _ = x_sample.shape
    wb = cache_k_win.shape[2]
    past = wb
    assert wb == MAX_DIST and S % (16 * QB) == 0 and T == 4

    l = 0
    w_in_b = w_in[l].astype(BF16)
    wpa, wpb, wpc = w_proj_a[l].astype(BF16), w_proj_b[l].astype(BF16), w_proj_c[l].astype(BF16)
    wout = w_out[l].astype(BF16)
    gpre, gpost = g_pre[l][None], g_post[l][None]
    cb, lg, lb = conv_b[l][None], ln_g[l][None], ln_b[l][None]
    cw = conv_w[l]

    (qkv, k_keep, v_keep, za, glu, zb, qc, zc, gt, glu_tail) = _inproj_prompt(x_prompt, gpre, w_in_b, tm=256)
    mk_f, mv_f, mk_b, mv_b = _memkv(mem_prompt.reshape(B * N_MEM, D_MODEL), g_mem[l][None],
                                    w_mem_kv[l].astype(BF16), tm=512)
    os_, ls_ = [], []
    for (_, d) in PATTERNS:
        o, lse = _band_attn(qkv, _band_tables(rel_bias, d), d)
        os_.append(o)
        ls_.append(lse)
    obg = _conv_prompt(glu, zb, cw, cb, lg, lb, tm=512, rc=32)
    y_p = _tail_prompt(os_, ls_, za, obg, qc, zc, mk_b.reshape(B, N_MEM, EC), mv_b.reshape(B, N_MEM, EC),
                       gt, x_prompt, wpa, wpb, wpc, wout, gpost, tm=256)

    (q_s, k_s, v_s, za_s, glu_s, zb_s, qc_s, zc_s, gt_s) = _inproj_sample(
        x_sample.reshape(Bd * T, D_MODEL), gpre, w_in_b, tm=Bd * T)
    n_tail, n_far = 512, (wb - 512) // 16
    tables = _sample_tables(rel_bias, wb, past, T, n_tail, n_far)
    oa_s, oc_s = _sample_attn(q_s.reshape(Bd, T, EA), k_s.reshape(Bd, T, EA), v_s.reshape(Bd, T, EA),
                              cache_k_win[l].reshape(Bd, wb, EA), cache_v_win[l].reshape(Bd, wb, EA),
                              tables, qc_s.reshape(Bd, T, EC),
                              cache_k_mem[l].reshape(Bd, N_MEM, EC), cache_v_mem[l].reshape(Bd, N_MEM, EC),
                              n_tail, n_far)
    ob_s = _conv_sample(state_conv[l], glu_s.reshape(Bd, T, EB), cw, cb, lg, lb, G=32)
    y_s = _tail_sample(oa_s.reshape(Bd * T, EA), za_s, ob_s.reshape(Bd * T, EB), zb_s,
                       oc_s.reshape(Bd * T, EC), zc_s, gt_s, x_sample.reshape(Bd * T, D_MODEL),
                       wpa, wpb, wpc, wout, gpost, tm=Bd * T)

    conv_state_s = jnp.concatenate([state_conv[l][:, T:], glu_s.reshape(Bd, T, EB)], axis=1)
    return (y_p, y_s.reshape(Bd, T, D_MODEL),
            k_keep.reshape(1, B, wb, HA, DA), v_keep.reshape(1, B, wb, HA, DA),
            glu_tail[:, HALO - (CONV_W - 1):][None],
            mk_f.reshape(1, B, N_MEM, HC, DC), mv_f.reshape(1, B, N_MEM, HC, DC),
            k_s.reshape(1, Bd, T, HA, DA), v_s.reshape(1, Bd, T, HA, DA),
            conv_state_s[None])
```

```python
import functools

import jax
import jax.numpy as jnp
import numpy as np
from jax import lax
from jax.experimental import pallas as pl
from jax.experimental.pallas import tpu as pltpu

F32 = jnp.float32
BF16 = jnp.bfloat16

D_MODEL = 1024
HA, DA = 12, 64
EA = HA * DA
NHP = HA // 2
PATTERNS = ((128, 1), (512, 4), (2048, 16))
NW = 128
QB = 128
EB = 768
CONV_W = 31
HC, DC = 4, 128
EC = HC * DC
N_MEM = 256
N_BUCKETS = 32
MAX_DIST = 2048
EPS = 1e-6
NEG = -1e30

C_Q, C_K, C_V, C_ZA = 0, EA, 2 * EA, 3 * EA
C_U = 4 * EA
C_G = C_U + EB
C_ZB = C_U + 2 * EB
C_QC = C_ZB + EB
C_ZC = C_QC + EC
C_GT = C_ZC + EC
IN_COLS = C_GT + 3 * D_MODEL

V7X_VMEM_LIMIT = 56 * 1024 * 1024
HALO = 32
QROWS = 8
NT_DIMS = (((1,), (1,)), ((), ()))


def _t5_bucket_ids(n):
    exact = N_BUCKETS // 2
    nf = np.maximum(n, 1).astype(np.float32)
    scale = np.float32(N_BUCKETS - exact) / np.log(np.float32(MAX_DIST) / np.float32(exact))
    large = exact + (np.log(nf / np.float32(exact)) * scale).astype(np.int32)
    large = np.minimum(large, N_BUCKETS - 1)
    return np.where(n < exact, n, large).astype(np.int32)


def _bias_by_bucket(rel_bias, ids):
    onehot = (ids[None] == np.arange(N_BUCKETS).reshape((-1,) + (1,) * ids.ndim)).astype(np.float32)
    return jnp.tensordot(rel_bias.astype(F32).T, jnp.asarray(onehot), axes=1,
                         precision=lax.Precision.HIGHEST)


def _sigmoid(x):
    return 1.0 / (1.0 + jnp.exp(-x))


def _silu(x):
    return x * _sigmoid(x)


def _rmsnorm_f32(x, g):
    return x * lax.rsqrt(jnp.mean(x * x, axis=-1, keepdims=True) + EPS) * g


def _params(*sem):
    return pltpu.CompilerParams(dimension_semantics=sem, vmem_limit_bytes=V7X_VMEM_LIMIT)


def _inproj_prompt_kernel(x_ref, g_ref, w_ref, qkv_ref, kf_ref, vf_ref, za_ref, glu_ref, zb_ref,
                          qc_ref, zc_ref, gt_ref, tail_ref):
    xn = _rmsnorm_f32(x_ref[0], g_ref[...]).astype(BF16)

    def mm(lo, n):
        return jnp.dot(xn, w_ref[:, lo:lo + n], preferred_element_type=F32)

    def put_pairs(res, base):
        for c in range(NHP):
            qkv_ref[0, base + c] = res[:, c * 128:(c + 1) * 128].astype(BF16)

    put_pairs(mm(C_Q, EA) * (DA ** -0.5), 0)
    k = mm(C_K, EA)
    put_pairs(k, NHP)
    kf_ref[0] = k
    v = mm(C_V, EA)
    put_pairs(v, 2 * NHP)
    vf_ref[0] = v
    za_ref[0] = mm(C_ZA, EA).astype(BF16)
    glu = mm(C_U, EB) * _sigmoid(mm(C_G, EB))
    glu_ref[0] = glu.astype(BF16)
    tail_ref[0] = glu[glu.shape[0] - HALO:, :]
    zb_ref[0] = mm(C_ZB, EB).astype(BF16)
    qc_ref[0] = (mm(C_QC, EC) * (DC ** -0.5)).astype(BF16)
    zc_ref[0] = mm(C_ZC, EC).astype(BF16)
    for c in range(3):
        gt_ref[0, :, c * D_MODEL:(c + 1) * D_MODEL] = mm(C_GT + c * D_MODEL, D_MODEL).astype(BF16)


def _inproj_prompt(x, g, w, tm):
    B, S, _ = x.shape
    nt = S // tm
    wb = min(MAX_DIST, S)
    first_kept = (S - wb) // tm
    row = lambda b, i: (b, i, 0)
    kept = lambda b, i: (b, jnp.maximum(i - first_kept, 0), 0)
    out_shape = (
        jax.ShapeDtypeStruct((B, 3 * NHP, S, 128), BF16),
        jax.ShapeDtypeStruct((B, wb, EA), F32),
        jax.ShapeDtypeStruct((B, wb, EA), F32),
        jax.ShapeDtypeStruct((B, S, EA), BF16),
        jax.ShapeDtypeStruct((B, S, EB), BF16),
        jax.ShapeDtypeStruct((B, S, EB), BF16),
        jax.ShapeDtypeStruct((B, S, EC), BF16),
        jax.ShapeDtypeStruct((B, S, EC), BF16),
        jax.ShapeDtypeStruct((B, S, 3 * D_MODEL), BF16),
        jax.ShapeDtypeStruct((B, HALO, EB), F32),
    )
    out_specs = (
        pl.BlockSpec((1, 3 * NHP, tm, 128), lambda b, i: (b, 0, i, 0)),
        pl.BlockSpec((1, tm, EA), kept),
        pl.BlockSpec((1, tm, EA), kept),
        pl.BlockSpec((1, tm, EA), row),
        pl.BlockSpec((1, tm, EB), row),
        pl.BlockSpec((1, tm, EB), row),
        pl.BlockSpec((1, tm, EC), row),
        pl.BlockSpec((1, tm, EC), row),
        pl.BlockSpec((1, tm, 3 * D_MODEL), row),
        pl.BlockSpec((1, HALO, EB), lambda b, i: (b, 0, 0)),
    )
    return pl.pallas_call(
        _inproj_prompt_kernel,
        out_shape=out_shape,
        grid=(B, nt),
        in_specs=[
            pl.BlockSpec((1, tm, D_MODEL), row),
            pl.BlockSpec((1, D_MODEL), lambda b, i: (0, 0)),
            pl.BlockSpec((D_MODEL, IN_COLS), lambda b, i: (0, 0), pipeline_mode=pl.Buffered(1)),
        ],
        out_specs=out_specs,
        compiler_params=_params("arbitrary", "arbitrary"),
        name="inproj_prompt",
    )(x, g, w)


def _inproj_sample_kernel(x_ref, g_ref, w_ref, q_ref, k_ref, v_ref, za_ref, glu_ref, zb_ref,
                          qc_ref, zc_ref, gt_ref):
    xn = _rmsnorm_f32(x_ref[...], g_ref[...]).astype(BF16)

    def mm(lo, n):
        return jnp.dot(xn, w_ref[:, lo:lo + n], preferred_element_type=F32)

    q_ref[...] = mm(C_Q, EA) * (DA ** -0.5)
    k_ref[...] = mm(C_K, EA)
    v_ref[...] = mm(C_V, EA)
    za_ref[...] = mm(C_ZA, EA).astype(BF16)
    glu_ref[...] = mm(C_U, EB) * _sigmoid(mm(C_G, EB))
    zb_ref[...] = mm(C_ZB, EB).astype(BF16)
    qc_ref[...] = mm(C_QC, EC) * (DC ** -0.5)
    zc_ref[...] = mm(C_ZC, EC).astype(BF16)
    for c in range(3):
        gt_ref[:, c * D_MODEL:(c + 1) * D_MODEL] = mm(C_GT + c * D_MODEL, D_MODEL).astype(BF16)


def _inproj_sample(x, g, w, tm):
    M = x.shape[0]
    row = lambda i: (i, 0)
    widths = (EA, EA, EA, EA, EB, EB, EC, EC, 3 * D_MODEL)
    dtypes = (F32, F32, F32, BF16, F32, BF16, F32, BF16, BF16)
    return pl.pallas_call(
        _inproj_sample_kernel,
        out_shape=tuple(jax.ShapeDtypeStruct((M, n), dt) for n, dt in zip(widths, dtypes)),
        grid=(M // tm,),
        in_specs=[
            pl.BlockSpec((tm, D_MODEL), row),
            pl.BlockSpec((1, D_MODEL), lambda i: (0, 0)),
            pl.BlockSpec((D_MODEL, IN_COLS), lambda i: (0, 0), pipeline_mode=pl.Buffered(1)),
        ],
        out_specs=tuple(pl.BlockSpec((tm, n), row) for n in widths),
        compiler_params=_params("arbitrary"),
        name="inproj_sample",
    )(x, g, w)


def _memkv_kernel(m_ref, g_ref, w_ref, kf_ref, vf_ref, kb_ref, vb_ref):
    xn = _rmsnorm_f32(m_ref[...], g_ref[...]).astype(BF16)
    k = jnp.dot(xn, w_ref[:, :EC], preferred_element_type=F32)
    v = jnp.dot(xn, w_ref[:, EC:], preferred_element_type=F32)
    kf_ref[...] = k
    vf_ref[...] = v
    kb_ref[...] = k.astype(BF16)
    vb_ref[...] = v.astype(BF16)


def _memkv(mem, g, w, tm):
    M = mem.shape[0]
    row = lambda i: (i, 0)
    return pl.pallas_call(
        _memkv_kernel,
        out_shape=(jax.ShapeDtypeStruct((M, EC), F32), jax.ShapeDtypeStruct((M, EC), F32),
                   jax.ShapeDtypeStruct((M, EC), BF16), jax.ShapeDtypeStruct((M, EC), BF16)),
        grid=(M // tm,),
        in_specs=[pl.BlockSpec((tm, D_MODEL), row),
                  pl.BlockSpec((1, D_MODEL), lambda i: (0, 0)),
                  pl.BlockSpec((D_MODEL, 2 * EC), lambda i: (0, 0))],
        out_specs=tuple(pl.BlockSpec((tm, EC), row) for _ in range(4)),
        compiler_params=_params("arbitrary"),
        name="memkv",
    )(mem, g, w)


def _band_tables(rel_bias):
    iq = np.arange(QB)[:, None]
    ik = np.arange(2 * QB)[None, :]
    dist = iq - ik + QB
    band = (dist >= 0) & (dist <= NW)
    out = []
    for (_, d) in PATTERNS:
        bias = _bias_by_bucket(rel_bias, _t5_bucket_ids(np.clip(dist, 0, NW) * d))
        later = jnp.where(jnp.asarray(band)[None], bias, NEG)
        first = jnp.where(jnp.asarray(band & (ik >= QB))[None], bias, NEG)
        t = jnp.stack([first, later], axis=1)
        out.append(t.reshape(NHP, 2, 2, QB, 2 * QB).transpose(0, 2, 1, 3, 4)
                   .reshape(NHP, 2, 2 * QB, 2 * QB))
    return jnp.stack(out)


def _band_attn_kernel(q_ref, k_ref, v_ref, t_ref, oa_ref,
                      nat32, p4_32, p4_ref, p16_ref, o4_ref, l4_ref, o16_ref, l16_ref, *, S, unroll):
    n_units = S // QB
    lane = lax.broadcasted_iota(jnp.int32, (QB, 128), 1)
    head0 = lane < DA
    keep0 = head0.astype(F32).astype(BF16)
    keep1 = (1.0 - head0.astype(F32)).astype(BF16)
    ones = jnp.ones((2 * QB, 128), BF16)
    n4 = S // 4

    for t, src in enumerate((q_ref, k_ref, v_ref)):
        def widen(c, carry, src=src):
            r0 = pl.multiple_of(c * 256, 256)
            nat32[pl.ds(r0, 256), :] = src[0, 0, pl.ds(r0, 256), :].astype(F32)
            return carry
        lax.fori_loop(0, S // 256, widen, 0)
        for r in range(4):
            def by4(c, carry, r=r, t=t):
                u0 = pl.multiple_of(c * 256, 256)
                x = nat32[pl.ds(4 * u0 + r, 256, stride=4), :]
                p4_32[pl.ds(r * n4 + u0, 256), :] = x
                p4_ref[t, pl.ds(r * n4 + u0, 256), :] = x.astype(BF16)
                return carry
            lax.fori_loop(0, n4 // 256, by4, 0)
        for r4 in range(4):
            for s in range(4):
                x = p4_32[pl.ds(r4 * n4 + s, n4 // 4, stride=4), :]
                p16_ref[t, pl.ds((4 * s + r4) * (n4 // 4), n4 // 4), :] = x.astype(BF16)

    def unit(u, pat, d, qsrc, ksrc, vsrc):
        nqb = n_units // d
        row = pl.multiple_of(u * QB, QB)
        prow = pl.multiple_of(jnp.maximum(u - 1, 0) * QB, QB)
        q2 = qsrc[pl.ds(row, QB), :]
        kk = jnp.concatenate([ksrc[pl.ds(prow, QB), :], ksrc[pl.ds(row, QB), :]], axis=0)
        vv = jnp.concatenate([vsrc[pl.ds(prow, QB), :], vsrc[pl.ds(row, QB), :]], axis=0)
        qs = jnp.concatenate([q2 * keep0, q2 * keep1], axis=0)
        s = lax.dot_general(qs, kk, NT_DIMS, preferred_element_type=F32)
        s = s + t_ref[pat, 0, jnp.minimum(u & (nqb - 1), 1)]
        m = jnp.max(s, axis=-1, keepdims=True)
        p = jnp.exp(s - m).astype(BF16)
        oe = jnp.dot(p, jnp.concatenate([vv, ones], axis=1), preferred_element_type=F32)
        l = oe[:, 128:]
        o = oe[:, :128] / l
        lse = m + jnp.log(l)
        return (jnp.where(head0, o[:QB], o[QB:]), jnp.where(head0, lse[:QB], lse[QB:]))

    def strided_pattern(pat, d, src, o_ref, l_ref):
        nqb = n_units // d

        def body(u, carry):
            o, lse = unit(u, pat, d, src.at[0], src.at[1], src.at[2])
            start = u // nqb + (u & (nqb - 1)) * (d * QB)
            o_ref[pl.ds(start, QB, stride=d), :] = o
            l_ref[pl.ds(start, QB, stride=d), :] = lse
            return carry
        lax.fori_loop(0, n_units, body, 0, unroll=unroll)

    strided_pattern(2, 16, p16_ref, o16_ref, l16_ref)
    strided_pattern(1, 4, p4_ref, o4_ref, l4_ref)

    def dense(u, carry):
        o1, l1 = unit(u, 0, 1, q_ref.at[0, 0], k_ref.at[0, 0], v_ref.at[0, 0])
        row = pl.multiple_of(u * QB, QB)
        l4, l16 = l4_ref[pl.ds(row, QB), :], l16_ref[pl.ds(row, QB), :]
        mx = jnp.maximum(jnp.maximum(l1, l4), l16)
        e1, e4, e16 = jnp.exp(l1 - mx), jnp.exp(l4 - mx), jnp.exp(l16 - mx)
        oa = (o1 * e1 + o4_ref[pl.ds(row, QB), :] * e4 + o16_ref[pl.ds(row, QB), :] * e16) / (e1 + e4 + e16)
        oa_ref[0, 0, pl.ds(row, QB), :] = oa.astype(BF16)
        return carry
    lax.fori_loop(0, n_units, dense, 0, unroll=unroll)


def _band_attn(qkv, tables, unroll):
    B, _, S, _ = qkv.shape
    blk = (1, 1, S, 128)
    return pl.pallas_call(
        functools.partial(_band_attn_kernel, S=S, unroll=unroll),
        out_shape=jax.ShapeDtypeStruct((B, NHP, S, 128), BF16),
        grid=(B, NHP),
        in_specs=[pl.BlockSpec(blk, lambda b, h: (b, h, 0, 0)),
                  pl.BlockSpec(blk, lambda b, h: (b, NHP + h, 0, 0)),
                  pl.BlockSpec(blk, lambda b, h: (b, 2 * NHP + h, 0, 0)),
                  pl.BlockSpec((3, 1, 2, 2 * QB, 2 * QB), lambda b, h: (0, h, 0, 0, 0))],
        out_specs=pl.BlockSpec(blk, lambda b, h: (b, h, 0, 0)),
        scratch_shapes=[pltpu.VMEM((S, 128), F32), pltpu.VMEM((S, 128), F32),
                        pltpu.VMEM((3, S, 128), BF16), pltpu.VMEM((3, S, 128), BF16),
                        pltpu.VMEM((S, 128), F32), pltpu.VMEM((S, 128), F32),
                        pltpu.VMEM((S, 128), F32), pltpu.VMEM((S, 128), F32)],
        compiler_params=_params("arbitrary", "arbitrary"),
        name="band_attn",
    )(qkv, qkv, qkv, tables)


def _layernorm_silu(c, g, b):
    mu = jnp.mean(c, axis=-1, keepdims=True)
    cc = c - mu
    var = jnp.mean(cc * cc, axis=-1, keepdims=True)
    return _silu(cc * lax.rsqrt(var + EPS) * g + b)


def _conv_prompt_kernel(glu_ref, halo_ref, zb_ref, cw_ref, cb_ref, lg_ref, lb_ref, o_ref, full_ref,
                        *, tm, rc):
    i = pl.program_id(1)
    halo = halo_ref[0].astype(F32)
    full_ref[0, 0:HALO, :] = jnp.where(i == 0, jnp.zeros_like(halo), halo)
    full_ref[0, HALO:HALO + tm, :] = glu_ref[0].astype(F32)
    n_shift = HALO + tm - 8
    for s in range(1, 8):
        full_ref[s, 0:n_shift, :] = full_ref[0, s:s + n_shift, :]
    first_tap = HALO - (CONV_W - 1)

    def chunk(c, carry):
        base = pl.multiple_of(c * rc, rc)
        acc = jnp.zeros((rc, EB), F32)
        for j in range(CONV_W):
            off = first_tap + j
            acc = acc + full_ref[off % 8, pl.ds(base + (off // 8) * 8, rc), :] * cw_ref[j:j + 1, :]
        ob = _layernorm_silu(acc + cb_ref[...], lg_ref[...], lb_ref[...])
        zb = zb_ref[0, pl.ds(base, rc), :].astype(F32)
        o_ref[0, pl.ds(base, rc), :] = (ob * _silu(zb)).astype(BF16)
        return carry

    lax.fori_loop(0, tm // rc, chunk, 0)


def _conv_prompt(glu, zb, cw, cb, lg, lb, tm, rc):
    B, S, _ = glu.shape
    row = lambda b, i: (b, i, 0)
    vec = lambda b, i: (0, 0)
    return pl.pallas_call(
        functools.partial(_conv_prompt_kernel, tm=tm, rc=rc),
        out_shape=jax.ShapeDtypeStruct((B, S, EB), BF16),
        grid=(B, S // tm),
        in_specs=[pl.BlockSpec((1, tm, EB), row),
                  pl.BlockSpec((1, HALO, EB), lambda b, i: (b, jnp.maximum(i * (tm // HALO) - 1, 0), 0)),
                  pl.BlockSpec((1, tm, EB), row),
                  pl.BlockSpec((CONV_W, EB), vec),
                  pl.BlockSpec((1, EB), vec), pl.BlockSpec((1, EB), vec), pl.BlockSpec((1, EB), vec)],
        out_specs=pl.BlockSpec((1, tm, EB), row),
        scratch_shapes=[pltpu.VMEM((8, HALO + tm, EB), F32)],
        compiler_params=_params("arbitrary", "arbitrary"),
        name="conv_prompt",
    )(glu, glu, zb, cw, cb, lg, lb)


def _gated_tail(a_g, b_g, c_g, gt_ref, x, wpa_ref, wpb_ref, wpc_ref, wout_ref, gpost_ref):
    def gate(k):
        return _sigmoid(gt_ref[:, k * D_MODEL:(k + 1) * D_MODEL].astype(F32))

    mix = gate(0) * jnp.dot(a_g, wpa_ref[...], preferred_element_type=F32)
    mix = mix + gate(1) * jnp.dot(b_g, wpb_ref[...], preferred_element_type=F32)
    mix = mix + gate(2) * jnp.dot(c_g, wpc_ref[...], preferred_element_type=F32)
    z = jnp.dot(mix.astype(BF16), wout_ref[...], preferred_element_type=F32)
    return x + _rmsnorm_f32(z, gpost_ref[...])


def _tail_prompt_kernel(oa_ref, za_ref, obg_ref, qc_ref, zc_ref, mk_ref, mv_ref, gt_ref, x_ref,
                        wpa_ref, wpb_ref, wpc_ref, wout_ref, gpost_ref, y_ref):
    a_g = jnp.concatenate(
        [(oa_ref[0, c].astype(F32) * _silu(za_ref[0, :, c * 128:(c + 1) * 128].astype(F32))).astype(BF16)
         for c in range(NHP)], axis=-1)
    pieces = []
    for h in range(HC):
        hs = slice(h * DC, (h + 1) * DC)
        s = lax.dot_general(qc_ref[0, :, hs], mk_ref[0, :, hs], NT_DIMS, preferred_element_type=F32)
        p = jnp.exp(s - jnp.max(s, axis=-1, keepdims=True))
        l = jnp.sum(p, axis=-1, keepdims=True)
        oc = jnp.dot(p.astype(BF16), mv_ref[0, :, hs], preferred_element_type=F32) / l
        pieces.append((oc * _silu(zc_ref[0, :, hs].astype(F32))).astype(BF16))
    c_g = jnp.concatenate(pieces, axis=-1)
    y_ref[0] = _gated_tail(a_g, obg_ref[0], c_g, gt_ref.at[0], x_ref[0],
                           wpa_ref, wpb_ref, wpc_ref, wout_ref, gpost_ref)


def _tail_prompt(oa, za, obg, qc, zc, mk, mv, gt, x, wpa, wpb, wpc, wout, gpost, tm):
    B, S, _ = x.shape
    row = lambda b, i: (b, i, 0)
    const = lambda b, i: (0, 0)
    return pl.pallas_call(
        _tail_prompt_kernel,
        out_shape=jax.ShapeDtypeStruct((B, S, D_MODEL), F32),
        grid=(B, S // tm),
        in_specs=[
            pl.BlockSpec((1, NHP, tm, 128), lambda b, i: (b, 0, i, 0)),
            pl.BlockSpec((1, tm, EA), row), pl.BlockSpec((1, tm, EB), row),
            pl.BlockSpec((1, tm, EC), row), pl.BlockSpec((1, tm, EC), row),
            pl.BlockSpec((1, N_MEM, EC), lambda b, i: (b, 0, 0)),
            pl.BlockSpec((1, N_MEM, EC), lambda b, i: (b, 0, 0)),
            pl.BlockSpec((1, tm, 3 * D_MODEL), row), pl.BlockSpec((1, tm, D_MODEL), row),
            pl.BlockSpec((EA, D_MODEL), const), pl.BlockSpec((EB, D_MODEL), const),
            pl.BlockSpec((EC, D_MODEL), const), pl.BlockSpec((D_MODEL, D_MODEL), const),
            pl.BlockSpec((1, D_MODEL), const)],
        out_specs=pl.BlockSpec((1, tm, D_MODEL), row),
        compiler_params=_params("arbitrary", "arbitrary"),
        name="tail_prompt",
    )(oa, za, obg, qc, zc, mk, mv, gt, x, wpa, wpb, wpc, wout, gpost)


def _tail_sample_kernel(oa_ref, za_ref, ob_ref, zb_ref, oc_ref, zc_ref, gt_ref, x_ref,
                        wpa_ref, wpb_ref, wpc_ref, wout_ref, gpost_ref, y_ref):
    a_g = (oa_ref[...] * _silu(za_ref[...].astype(F32))).astype(BF16)
    b_g = (ob_ref[...] * _silu(zb_ref[...].astype(F32))).astype(BF16)
    c_g = (oc_ref[...] * _silu(zc_ref[...].astype(F32))).astype(BF16)
    y_ref[...] = _gated_tail(a_g, b_g, c_g, gt_ref, x_ref[...],
                             wpa_ref, wpb_ref, wpc_ref, wout_ref, gpost_ref)


def _tail_sample(oa, za, ob, zb, oc, zc, gt, x, wpa, wpb, wpc, wout, gpost, tm):
    M = x.shape[0]
    row = lambda i: (i, 0)
    const = lambda i: (0, 0)
    widths = (EA, EA, EB, EB, EC, EC, 3 * D_MODEL, D_MODEL)
    return pl.pallas_call(
        _tail_sample_kernel,
        out_shape=jax.ShapeDtypeStruct((M, D_MODEL), F32),
        grid=(M // tm,),
        in_specs=[pl.BlockSpec((tm, n), row) for n in widths] + [
            pl.BlockSpec((EA, D_MODEL), const), pl.BlockSpec((EB, D_MODEL), const),
            pl.BlockSpec((EC, D_MODEL), const), pl.BlockSpec((D_MODEL, D_MODEL), const),
            pl.BlockSpec((1, D_MODEL), const)],
        out_specs=pl.BlockSpec((tm, D_MODEL), row),
        compiler_params=_params("arbitrary"),
        name="tail_sample",
    )(oa, za, ob, zb, oc, zc, gt, x, wpa, wpb, wpc, wout, gpost)


def _sample_tables(rel_bias, wb, past, T):
    def mult_of(dl, real):
        m = np.zeros(dl.shape, np.float32)
        for (w, d) in PATTERNS:
            m += (real & (dl % d == 0) & (dl <= w)).astype(np.float32)
        return m

    def tables(dl, real):
        row = np.arange(QROWS)[:, None]
        real = real & (dl >= 0) & (past + row - dl >= 0) & (wb + row - dl >= 0)
        m = mult_of(dl, real)
        dlc = np.clip(dl, 0, MAX_DIST)
        m[T:] = m[0]
        bias = jnp.where(jnp.asarray(m > 0)[None], _bias_by_bucket(rel_bias, _t5_bucket_ids(dlc)), NEG)
        return bias, jnp.asarray(m)

    i = np.arange(QROWS)[:, None]
    pos = np.arange(wb)[None, :]
    cache_b, cache_m = tables(wb + i - pos, np.ones((QROWS, wb), bool))
    j = np.arange(QROWS)[None, :]
    new_b, new_m = tables(i - j, np.broadcast_to(j < T, (QROWS, QROWS)))
    return cache_b, cache_m, new_b, new_m


def _sample_attn_kernel(q_ref, kt_ref, vt_ref, kn_ref, vn_ref, cb_ref, cm_ref, nb_ref, nm_ref,
                        qc_ref, mk_ref, mv_ref, oa_ref, oc_ref, *, T):
    zpad = jnp.zeros((QROWS - T, EA), F32)
    q8 = jnp.concatenate([q_ref[0], zpad], axis=0).astype(BF16)
    kn8 = jnp.concatenate([kn_ref[0], zpad], axis=0).astype(BF16)
    vn8 = jnp.concatenate([vn_ref[0], zpad], axis=0).astype(BF16)
    outs = []
    for h in range(HA):
        hs = slice(h * DA, (h + 1) * DA)
        qh = q8[:, hs]
        s = jnp.dot(qh, kt_ref[0, h].astype(BF16), preferred_element_type=F32) + cb_ref[h]
        sn = lax.dot_general(qh, kn8[:, hs], NT_DIMS, preferred_element_type=F32) + nb_ref[h]
        m = jnp.maximum(jnp.max(s, axis=-1, keepdims=True), jnp.max(sn, axis=-1, keepdims=True))
        p = jnp.exp(s - m) * cm_ref[...]
        pn = jnp.exp(sn - m) * nm_ref[...]
        l = jnp.sum(p, axis=-1, keepdims=True) + jnp.sum(pn, axis=-1, keepdims=True)
        o = lax.dot_general(p.astype(BF16), vt_ref[0, h].astype(BF16), NT_DIMS, preferred_element_type=F32)
        o = o + jnp.dot(pn.astype(BF16), vn8[:, hs], preferred_element_type=F32)
        outs.append(o / l)
    oa_ref[0] = jnp.concatenate(outs, axis=-1)[:T]
    cpad = jnp.zeros((QROWS - T, EC), F32)
    qc8 = jnp.concatenate([qc_ref[0], cpad], axis=0).astype(BF16)
    outs = []
    for h in range(HC):
        hs = slice(h * DC, (h + 1) * DC)
        sc = lax.dot_general(qc8[:, hs], mk_ref[0, :, h, :].astype(BF16), NT_DIMS, preferred_element_type=F32)
        pc = jnp.exp(sc - jnp.max(sc, axis=-1, keepdims=True))
        lc = jnp.sum(pc, axis=-1, keepdims=True)
        outs.append(jnp.dot(pc.astype(BF16), mv_ref[0, :, h, :].astype(BF16), preferred_element_type=F32) / lc)
    oc_ref[0] = jnp.concatenate(outs, axis=-1)[:T]


def _sample_attn(q, k_new, v_new, cache_kt, cache_vt, tables, qc, mem_k, mem_v):
    Bd, T, _ = q.shape
    wb = cache_kt.shape[-1]
    cache_b, cache_m, new_b, new_m = tables
    per_b3 = lambda b: (b, 0, 0)
    per_b4 = lambda b: (b, 0, 0, 0)
    return pl.pallas_call(
        functools.partial(_sample_attn_kernel, T=T),
        out_shape=(jax.ShapeDtypeStruct((Bd, T, EA), F32), jax.ShapeDtypeStruct((Bd, T, EC), F32)),
        grid=(Bd,),
        in_specs=[pl.BlockSpec((1, T, EA), per_b3),
                  pl.BlockSpec((1, HA, DA, wb), per_b4), pl.BlockSpec((1, HA, DA, wb), per_b4),
                  pl.BlockSpec((1, T, EA), per_b3), pl.BlockSpec((1, T, EA), per_b3),
                  pl.BlockSpec((HA, QROWS, wb), lambda b: (0, 0, 0)),
                  pl.BlockSpec((QROWS, wb), lambda b: (0, 0)),
                  pl.BlockSpec((HA, QROWS, QROWS), lambda b: (0, 0, 0)),
                  pl.BlockSpec((QROWS, QROWS), lambda b: (0, 0)),
                  pl.BlockSpec((1, T, EC), per_b3),
                  pl.BlockSpec((1, N_MEM, HC, DC), per_b4), pl.BlockSpec((1, N_MEM, HC, DC), per_b4)],
        out_specs=(pl.BlockSpec((1, T, EA), per_b3), pl.BlockSpec((1, T, EC), per_b3)),
        compiler_params=_params("arbitrary"),
        name="sample_attn",
    )(q, cache_kt, cache_vt, k_new, v_new, cache_b, cache_m, new_b, new_m, qc, mem_k, mem_v)


def _conv_sample_kernel(st_ref, glu_ref, cw_ref, cb_ref, lg_ref, lb_ref, o_ref, *, T, rc):
    n_hist = CONV_W - 1
    Bd = st_ref.shape[1]
    for i in range(T):
        def chunk(c, carry, i=i):
            rows = pl.ds(pl.multiple_of(c * rc, rc), rc)
            acc = jnp.zeros((rc, EB), F32)
            for j in range(CONV_W):
                src = st_ref[i + j, rows, :] if i + j < n_hist else glu_ref[i + j - n_hist, rows, :]
                acc = acc + src * cw_ref[j:j + 1, :]
            o_ref[i, rows, :] = _layernorm_silu(acc + cb_ref[...], lg_ref[...], lb_ref[...])
            return carry
        lax.fori_loop(0, Bd // rc, chunk, 0)


def _conv_sample(state_t, glu_t, cw, cb, lg, lb, rc):
    T, Bd, _ = glu_t.shape
    return pl.pallas_call(
        functools.partial(_conv_sample_kernel, T=T, rc=rc),
        out_shape=jax.ShapeDtypeStruct((T, Bd, EB), F32),
        compiler_params=pltpu.CompilerParams(vmem_limit_bytes=V7X_VMEM_LIMIT),
        name="conv_sample",
    )(state_t, glu_t, cw, cb, lg, lb)


def kernel(x_prompt, x_sample, mem_prompt, cache_k_win, cache_v_win, state_conv, cache_k_mem, cache_v_mem,
           rel_bias, g_pre, w_in, g_mem, w_mem_kv, conv_w, conv_b, ln_g, ln_b, w_proj_a, w_proj_b,
           w_proj_c, w_out, g_post):
    depth = g_pre.shape[0]
    assert depth == 1, "single-layer step"
    B, S, _ = x_prompt.shape
    Bd, T, _ = x_sample.shape
    wb = cache_k_win.shape[2]
    past = wb
    assert wb == MAX_DIST and S % (16 * QB) == 0 and T <= QROWS

    l = 0
    w_in_b = w_in[l].astype(BF16)
    wpa, wpb, wpc = w_proj_a[l].astype(BF16), w_proj_b[l].astype(BF16), w_proj_c[l].astype(BF16)
    wout = w_out[l].astype(BF16)
    gpre, gpost = g_pre[l][None], g_post[l][None]
    cb, lg, lb = conv_b[l][None], ln_g[l][None], ln_b[l][None]
    cw = conv_w[l]

    (qkv, k_keep, v_keep, za, glu, zb, qc, zc, gt, glu_tail) = _inproj_prompt(x_prompt, gpre, w_in_b, tm=256)
    mk_f, mv_f, mk_b, mv_b = _memkv(mem_prompt.reshape(B * N_MEM, D_MODEL), g_mem[l][None],
                                    w_mem_kv[l].astype(BF16), tm=512)
    oa = _band_attn(qkv, _band_tables(rel_bias), unroll=4)
    obg = _conv_prompt(glu, zb, cw, cb, lg, lb, tm=512, rc=32)
    y_p = _tail_prompt(oa, za, obg, qc, zc, mk_b.reshape(B, N_MEM, EC), mv_b.reshape(B, N_MEM, EC),
                       gt, x_prompt, wpa, wpb, wpc, wout, gpost, tm=256)

    (q_s, k_s, v_s, za_s, glu_s, zb_s, qc_s, zc_s, gt_s) = _inproj_sample(
        x_sample.reshape(Bd * T, D_MODEL), gpre, w_in_b, tm=Bd * T)
    cache_kt = jnp.transpose(cache_k_win[l], (0, 2, 3, 1))
    cache_vt = jnp.transpose(cache_v_win[l], (0, 2, 3, 1))
    oa_s, oc_s = _sample_attn(q_s.reshape(Bd, T, EA), k_s.reshape(Bd, T, EA), v_s.reshape(Bd, T, EA),
                              cache_kt, cache_vt, _sample_tables(rel_bias, wb, past, T),
                              qc_s.reshape(Bd, T, EC), cache_k_mem[l], cache_v_mem[l])
    state_t = jnp.transpose(state_conv[l], (1, 0, 2))
    glu_t = jnp.transpose(glu_s.reshape(Bd, T, EB), (1, 0, 2))
    ob_t = _conv_sample(state_t, glu_t, cw, cb, lg, lb, rc=32)
    ob_s = jnp.transpose(ob_t, (1, 0, 2)).reshape(Bd * T, EB)
    y_s = _tail_sample(oa_s.reshape(Bd * T, EA), za_s, ob_s, zb_s,
                       oc_s.reshape(Bd * T, EC), zc_s, gt_s, x_sample.reshape(Bd * T, D_MODEL),
                       wpa, wpb, wpc, wout, gpost, tm=Bd * T)

    conv_state_s = jnp.transpose(jnp.concatenate([state_t[T:], glu_t], axis=0), (1, 0, 2))
    return (y_p, y_s.reshape(Bd, T, D_MODEL),
            k_keep.reshape(1, B, wb, HA, DA), v_keep.reshape(1, B, wb, HA, DA),
            glu_tail[:, HALO - (CONV_W - 1):][None],
            mk_f.reshape(1, B, N_MEM, HC, DC), mv_f.reshape(1, B, N_MEM, HC, DC),
            k_s.reshape(1, Bd, T, HA, DA), v_s.reshape(1, Bd, T, HA, DA),
            conv_state_s[None])
```

```python
import functools

import jax
import jax.numpy as jnp
import numpy as np
from jax import lax
from jax.experimental import pallas as pl
from jax.experimental.pallas import tpu as pltpu

F32 = jnp.float32
BF16 = jnp.bfloat16

D_MODEL = 1024
HA, DA = 12, 64
EA = HA * DA
NHP = HA // 2
PATTERNS = ((128, 1), (512, 4), (2048, 16))
NW = 128
QB = 128
EB = 768
CONV_W = 31
HC, DC = 4, 128
EC = HC * DC
N_MEM = 256
N_BUCKETS = 32
MAX_DIST = 2048
EPS = 1e-6
NEG = -1e30
LOG2E = 1.4426950408889634

C_Q, C_K, C_V, C_ZA = 0, EA, 2 * EA, 3 * EA
C_U = 4 * EA
C_G = C_U + EB
C_ZB = C_U + 2 * EB
C_QC = C_ZB + EB
C_ZC = C_QC + EC
C_GT = C_ZC + EC
IN_COLS = C_GT + 3 * D_MODEL

V7X_VMEM_LIMIT = 56 * 1024 * 1024
HALO = 32
QROWS = 8
NT_DIMS = (((1,), (1,)), ((), ()))


def _t5_bucket_ids(n):
    exact = N_BUCKETS // 2
    nf = np.maximum(n, 1).astype(np.float32)
    scale = np.float32(N_BUCKETS - exact) / np.log(np.float32(MAX_DIST) / np.float32(exact))
    large = exact + (np.log(nf / np.float32(exact)) * scale).astype(np.int32)
    large = np.minimum(large, N_BUCKETS - 1)
    return np.where(n < exact, n, large).astype(np.int32)


def _bias_by_bucket(rel_bias, ids):
    onehot = (ids[None] == np.arange(N_BUCKETS).reshape((-1,) + (1,) * ids.ndim)).astype(np.float32)
    return jnp.tensordot(rel_bias.astype(F32).T, jnp.asarray(onehot), axes=1,
                         precision=lax.Precision.HIGHEST)


def _sigmoid(x):
    return 1.0 / (1.0 + jnp.exp(-x))


def _silu(x):
    return x * _sigmoid(x)


def _rmsnorm_f32(x, g):
    return x * lax.rsqrt(jnp.mean(x * x, axis=-1, keepdims=True) + EPS) * g


def _params(*sem):
    return pltpu.CompilerParams(dimension_semantics=sem, vmem_limit_bytes=V7X_VMEM_LIMIT)


def _inproj_prompt_kernel(x_ref, g_ref, w_ref, qkv_ref, kf_ref, vf_ref, za_ref, glu_ref, zb_ref,
                          qc_ref, zc_ref, gt_ref, tail_ref):
    xn = _rmsnorm_f32(x_ref[0], g_ref[...]).astype(BF16)

    def mm(lo, n):
        return jnp.dot(xn, w_ref[:, lo:lo + n], preferred_element_type=F32)

    def put_pairs(res, base):
        for c in range(NHP):
            qkv_ref[0, base + c] = res[:, c * 128:(c + 1) * 128].astype(BF16)

    put_pairs(mm(C_Q, EA) * (DA ** -0.5 * LOG2E), 0)
    k = mm(C_K, EA)
    put_pairs(k, NHP)
    kf_ref[0] = k
    v = mm(C_V, EA)
    put_pairs(v, 2 * NHP)
    vf_ref[0] = v
    za_ref[0] = mm(C_ZA, EA).astype(BF16)
    glu = mm(C_U, EB) * _sigmoid(mm(C_G, EB))
    glu_ref[0] = glu.astype(BF16)
    tail_ref[0] = glu[glu.shape[0] - HALO:, :]
    zb_ref[0] = mm(C_ZB, EB).astype(BF16)
    qc_ref[0] = (mm(C_QC, EC) * (DC ** -0.5)).astype(BF16)
    zc_ref[0] = mm(C_ZC, EC).astype(BF16)
    for c in range(3):
        gt_ref[0, :, c * D_MODEL:(c + 1) * D_MODEL] = mm(C_GT + c * D_MODEL, D_MODEL).astype(BF16)


def _inproj_prompt(x, g, w, tm):
    B, S, _ = x.shape
    nt = S // tm
    wb = min(MAX_DIST, S)
    first_kept = (S - wb) // tm
    row = lambda b, i: (b, i, 0)
    kept = lambda b, i: (b, jnp.maximum(i - first_kept, 0), 0)
    out_shape = (
        jax.ShapeDtypeStruct((B, 3 * NHP, S, 128), BF16),
        jax.ShapeDtypeStruct((B, wb, EA), F32),
        jax.ShapeDtypeStruct((B, wb, EA), F32),
        jax.ShapeDtypeStruct((B, S, EA), BF16),
        jax.ShapeDtypeStruct((B, S, EB), BF16),
        jax.ShapeDtypeStruct((B, S, EB), BF16),
        jax.ShapeDtypeStruct((B, S, EC), BF16),
        jax.ShapeDtypeStruct((B, S, EC), BF16),
        jax.ShapeDtypeStruct((B, S, 3 * D_MODEL), BF16),
        jax.ShapeDtypeStruct((B, HALO, EB), F32),
    )
    out_specs = (
        pl.BlockSpec((1, 3 * NHP, tm, 128), lambda b, i: (b, 0, i, 0)),
        pl.BlockSpec((1, tm, EA), kept),
        pl.BlockSpec((1, tm, EA), kept),
        pl.BlockSpec((1, tm, EA), row),
        pl.BlockSpec((1, tm, EB), row),
        pl.BlockSpec((1, tm, EB), row),
        pl.BlockSpec((1, tm, EC), row),
        pl.BlockSpec((1, tm, EC), row),
        pl.BlockSpec((1, tm, 3 * D_MODEL), row),
        pl.BlockSpec((1, HALO, EB), lambda b, i: (b, 0, 0)),
    )
    return pl.pallas_call(
        _inproj_prompt_kernel,
        out_shape=out_shape,
        grid=(B, nt),
        in_specs=[
            pl.BlockSpec((1, tm, D_MODEL), row),
            pl.BlockSpec((1, D_MODEL), lambda b, i: (0, 0)),
            pl.BlockSpec((D_MODEL, IN_COLS), lambda b, i: (0, 0), pipeline_mode=pl.Buffered(1)),
        ],
        out_specs=out_specs,
        compiler_params=_params("arbitrary", "arbitrary"),
        name="inproj_prompt",
    )(x, g, w)


def _inproj_sample_kernel(x_ref, g_ref, w_ref, q_ref, k_ref, v_ref, za_ref, glu_ref, zb_ref,
                          qc_ref, zc_ref, gt_ref):
    xn = _rmsnorm_f32(x_ref[...], g_ref[...]).astype(BF16)

    def mm(lo, n):
        return jnp.dot(xn, w_ref[:, lo:lo + n], preferred_element_type=F32)

    q_ref[...] = mm(C_Q, EA) * (DA ** -0.5)
    k_ref[...] = mm(C_K, EA)
    v_ref[...] = mm(C_V, EA)
    za_ref[...] = mm(C_ZA, EA).astype(BF16)
    glu_ref[...] = mm(C_U, EB) * _sigmoid(mm(C_G, EB))
    zb_ref[...] = mm(C_ZB, EB).astype(BF16)
    qc_ref[...] = mm(C_QC, EC) * (DC ** -0.5)
    zc_ref[...] = mm(C_ZC, EC).astype(BF16)
    for c in range(3):
        gt_ref[:, c * D_MODEL:(c + 1) * D_MODEL] = mm(C_GT + c * D_MODEL, D_MODEL).astype(BF16)


def _inproj_sample(x, g, w, tm):
    M = x.shape[0]
    row = lambda i: (i, 0)
    widths = (EA, EA, EA, EA, EB, EB, EC, EC, 3 * D_MODEL)
    dtypes = (F32, F32, F32, BF16, F32, BF16, F32, BF16, BF16)
    return pl.pallas_call(
        _inproj_sample_kernel,
        out_shape=tuple(jax.ShapeDtypeStruct((M, n), dt) for n, dt in zip(widths, dtypes)),
        grid=(M // tm,),
        in_specs=[
            pl.BlockSpec((tm, D_MODEL), row),
            pl.BlockSpec((1, D_MODEL), lambda i: (0, 0)),
            pl.BlockSpec((D_MODEL, IN_COLS), lambda i: (0, 0), pipeline_mode=pl.Buffered(1)),
        ],
        out_specs=tuple(pl.BlockSpec((tm, n), row) for n in widths),
        compiler_params=_params("arbitrary"),
        name="inproj_sample",
    )(x, g, w)


def _memkv_kernel(m_ref, g_ref, w_ref, kf_ref, vf_ref, kb_ref, vb_ref):
    xn = _rmsnorm_f32(m_ref[...], g_ref[...]).astype(BF16)
    k = jnp.dot(xn, w_ref[:, :EC], preferred_element_type=F32)
    v = jnp.dot(xn, w_ref[:, EC:], preferred_element_type=F32)
    kf_ref[...] = k
    vf_ref[...] = v
    kb_ref[...] = k.astype(BF16)
    vb_ref[...] = v.astype(BF16)


def _memkv(mem, g, w, tm):
    M = mem.shape[0]
    row = lambda i: (i, 0)
    return pl.pallas_call(
        _memkv_kernel,
        out_shape=(jax.ShapeDtypeStruct((M, EC), F32), jax.ShapeDtypeStruct((M, EC), F32),
                   jax.ShapeDtypeStruct((M, EC), BF16), jax.ShapeDtypeStruct((M, EC), BF16)),
        grid=(M // tm,),
        in_specs=[pl.BlockSpec((tm, D_MODEL), row),
                  pl.BlockSpec((1, D_MODEL), lambda i: (0, 0)),
                  pl.BlockSpec((D_MODEL, 2 * EC), lambda i: (0, 0))],
        out_specs=tuple(pl.BlockSpec((tm, EC), row) for _ in range(4)),
        compiler_params=_params("arbitrary"),
        name="memkv",
    )(mem, g, w)


def _band_tables(rel_bias):
    iq = np.arange(QB)[:, None]
    ik = np.arange(2 * QB)[None, :]
    dist = iq - ik + QB
    band = (dist >= 0) & (dist <= NW)
    out = []
    for (_, d) in PATTERNS:
        bias = LOG2E * _bias_by_bucket(rel_bias, _t5_bucket_ids(np.clip(dist, 0, NW) * d))
        later = jnp.where(jnp.asarray(band)[None], bias, NEG)
        first = jnp.where(jnp.asarray(band & (ik >= QB))[None], bias, NEG)
        t = jnp.stack([first, later], axis=1)
        out.append(t.reshape(NHP, 2, 2, QB, 2 * QB).transpose(0, 2, 1, 3, 4)
                   .reshape(NHP, 2, 2 * QB, 2 * QB))
    return jnp.stack(out)


def _band_attn_kernel(q_ref, k_ref, v_ref, t_ref, oa_ref,
                      nat32, p4_32, p4_ref, p16_ref, a4_ref, l4_ref, m4_ref, a16_ref, l16_ref, m16_ref,
                      *, S, unroll):
    n_units = S // QB
    lane = lax.broadcasted_iota(jnp.int32, (QB, 128), 1)
    head0 = lane < DA
    keep0 = head0.astype(F32).astype(BF16)
    keep1 = (1.0 - head0.astype(F32)).astype(BF16)
    ones = jnp.ones((2 * QB, 128), BF16)
    n4 = S // 4

    for t, src in enumerate((q_ref, k_ref, v_ref)):
        def widen(c, carry, src=src):
            r0 = pl.multiple_of(c * 256, 256)
            nat32[pl.ds(r0, 256), :] = src[0, 0, pl.ds(r0, 256), :].astype(F32)
            return carry
        lax.fori_loop(0, S // 256, widen, 0)
        for r in range(4):
            def by4(c, carry, r=r, t=t):
                u0 = pl.multiple_of(c * 256, 256)
                x = nat32[pl.ds(4 * u0 + r, 256, stride=4), :]
                p4_32[pl.ds(r * n4 + u0, 256), :] = x
                p4_ref[t, pl.ds(r * n4 + u0, 256), :] = x.astype(BF16)
                return carry
            lax.fori_loop(0, n4 // 256, by4, 0)
        for r4 in range(4):
            for s in range(4):
                x = p4_32[pl.ds(r4 * n4 + s, n4 // 4, stride=4), :]
                p16_ref[t, pl.ds((4 * s + r4) * (n4 // 4), n4 // 4), :] = x.astype(BF16)

    def unit(u, pat, d, qsrc, ksrc, vsrc):
        nqb = n_units // d
        row = pl.multiple_of(u * QB, QB)
        prow = pl.multiple_of(jnp.maximum(u - 1, 0) * QB, QB)
        q2 = qsrc[pl.ds(row, QB), :]
        kk = jnp.concatenate([ksrc[pl.ds(prow, QB), :], ksrc[pl.ds(row, QB), :]], axis=0)
        vv = jnp.concatenate([vsrc[pl.ds(prow, QB), :], vsrc[pl.ds(row, QB), :]], axis=0)
        qs = jnp.concatenate([q2 * keep0, q2 * keep1], axis=0)
        s = lax.dot_general(qs, kk, NT_DIMS, preferred_element_type=F32)
        s = s + t_ref[pat, 0, jnp.minimum(u & (nqb - 1), 1)]
        m = jnp.max(s, axis=-1, keepdims=True)
        p = jnp.exp2(s - m).astype(BF16)
        oe = jnp.dot(p, jnp.concatenate([vv, ones], axis=1), preferred_element_type=F32)
        mb = jnp.broadcast_to(m, (2 * QB, 128))
        return (jnp.where(head0, oe[:QB, :128], oe[QB:, :128]),
                jnp.where(head0, oe[:QB, 128:], oe[QB:, 128:]),
                jnp.where(head0, mb[:QB], mb[QB:]))

    def strided_pattern(pat, d, src, acc_ref, den_ref, max_ref):
        nqb = n_units // d

        def body(u, carry):
            acc, den, mx = unit(u, pat, d, src.at[0], src.at[1], src.at[2])
            start = u // nqb + (u & (nqb - 1)) * (d * QB)
            acc_ref[pl.ds(start, QB, stride=d), :] = acc
            den_ref[pl.ds(start, QB, stride=d), :] = den
            max_ref[pl.ds(start, QB, stride=d), :] = mx
            return carry
        lax.fori_loop(0, n_units, body, 0, unroll=unroll)

    strided_pattern(2, 16, p16_ref, a16_ref, l16_ref, m16_ref)
    strided_pattern(1, 4, p4_ref, a4_ref, l4_ref, m4_ref)

    def dense(u, carry):
        a1, l1, m1 = unit(u, 0, 1, q_ref.at[0, 0], k_ref.at[0, 0], v_ref.at[0, 0])
        rows = pl.ds(pl.multiple_of(u * QB, QB), QB)
        m4, m16 = m4_ref[rows, :], m16_ref[rows, :]
        mx = jnp.maximum(jnp.maximum(m1, m4), m16)
        e1, e4, e16 = jnp.exp2(m1 - mx), jnp.exp2(m4 - mx), jnp.exp2(m16 - mx)
        num = a1 * e1 + a4_ref[rows, :] * e4 + a16_ref[rows, :] * e16
        den = l1 * e1 + l4_ref[rows, :] * e4 + l16_ref[rows, :] * e16
        oa_ref[0, 0, rows, :] = (num / den).astype(BF16)
        return carry
    lax.fori_loop(0, n_units, dense, 0, unroll=unroll)


def _band_attn(qkv, tables, unroll):
    B, _, S, _ = qkv.shape
    blk = (1, 1, S, 128)
    return pl.pallas_call(
        functools.partial(_band_attn_kernel, S=S, unroll=unroll),
        out_shape=jax.ShapeDtypeStruct((B, NHP, S, 128), BF16),
        grid=(B, NHP),
        in_specs=[pl.BlockSpec(blk, lambda b, h: (b, h, 0, 0)),
                  pl.BlockSpec(blk, lambda b, h: (b, NHP + h, 0, 0)),
                  pl.BlockSpec(blk, lambda b, h: (b, 2 * NHP + h, 0, 0)),
                  pl.BlockSpec((3, 1, 2, 2 * QB, 2 * QB), lambda b, h: (0, h, 0, 0, 0))],
        out_specs=pl.BlockSpec(blk, lambda b, h: (b, h, 0, 0)),
        scratch_shapes=[pltpu.VMEM((S, 128), F32), pltpu.VMEM((S, 128), F32),
                        pltpu.VMEM((3, S, 128), BF16), pltpu.VMEM((3, S, 128), BF16)]
                       + [pltpu.VMEM((S, 128), F32)] * 6,
        compiler_params=_params("arbitrary", "arbitrary"),
        name="band_attn",
    )(qkv, qkv, qkv, tables)


def _layernorm_silu(c, g, b):
    mu = jnp.mean(c, axis=-1, keepdims=True)
    cc = c - mu
    var = jnp.mean(cc * cc, axis=-1, keepdims=True)
    return _silu(cc * lax.rsqrt(var + EPS) * g + b)


def _conv_prompt_kernel(glu_ref, halo_ref, zb_ref, cw_ref, cb_ref, lg_ref, lb_ref, o_ref, full_ref,
                        *, tm, rc):
    i = pl.program_id(1)
    halo = halo_ref[0].astype(F32)
    full_ref[0, 0:HALO, :] = jnp.where(i == 0, jnp.zeros_like(halo), halo)
    full_ref[0, HALO:HALO + tm, :] = glu_ref[0].astype(F32)
    n_shift = HALO + tm - 8
    for s in range(1, 8):
        full_ref[s, 0:n_shift, :] = full_ref[0, s:s + n_shift, :]
    first_tap = HALO - (CONV_W - 1)

    def chunk(c, carry):
        base = pl.multiple_of(c * rc, rc)
        acc = jnp.zeros((rc, EB), F32)
        for j in range(CONV_W):
            off = first_tap + j
            acc = acc + full_ref[off % 8, pl.ds(base + (off // 8) * 8, rc), :] * cw_ref[j:j + 1, :]
        ob = _layernorm_silu(acc + cb_ref[...], lg_ref[...], lb_ref[...])
        zb = zb_ref[0, pl.ds(base, rc), :].astype(F32)
        o_ref[0, pl.ds(base, rc), :] = (ob * _silu(zb)).astype(BF16)
        return carry

    lax.fori_loop(0, tm // rc, chunk, 0)


def _conv_prompt(glu, zb, cw, cb, lg, lb, tm, rc):
    B, S, _ = glu.shape
    row = lambda b, i: (b, i, 0)
    vec = lambda b, i: (0, 0)
    return pl.pallas_call(
        functools.partial(_conv_prompt_kernel, tm=tm, rc=rc),
        out_shape=jax.ShapeDtypeStruct((B, S, EB), BF16),
        grid=(B, S // tm),
        in_specs=[pl.BlockSpec((1, tm, EB), row),
                  pl.BlockSpec((1, HALO, EB), lambda b, i: (b, jnp.maximum(i * (tm // HALO) - 1, 0), 0)),
                  pl.BlockSpec((1, tm, EB), row),
                  pl.BlockSpec((CONV_W, EB), vec),
                  pl.BlockSpec((1, EB), vec), pl.BlockSpec((1, EB), vec), pl.BlockSpec((1, EB), vec)],
        out_specs=pl.BlockSpec((1, tm, EB), row),
        scratch_shapes=[pltpu.VMEM((8, HALO + tm + 8, EB), F32)],
        compiler_params=_params("arbitrary", "arbitrary"),
        name="conv_prompt",
    )(glu, glu, zb, cw, cb, lg, lb)


def _gated_tail(a_g, b_g, c_g, gt_ref, x, wpa_ref, wpb_ref, wpc_ref, wout_ref, gpost_ref):
    def gate(k):
        return _sigmoid(gt_ref[:, k * D_MODEL:(k + 1) * D_MODEL].astype(F32))

    mix = gate(0) * jnp.dot(a_g, wpa_ref[...], preferred_element_type=F32)
    mix = mix + gate(1) * jnp.dot(b_g, wpb_ref[...], preferred_element_type=F32)
    mix = mix + gate(2) * jnp.dot(c_g, wpc_ref[...], preferred_element_type=F32)
    z = jnp.dot(mix.astype(BF16), wout_ref[...], preferred_element_type=F32)
    return x + _rmsnorm_f32(z, gpost_ref[...])


def _tail_prompt_kernel(oa_ref, za_ref, obg_ref, qc_ref, zc_ref, mk_ref, mv_ref, gt_ref, x_ref,
                        wpa_ref, wpb_ref, wpc_ref, wout_ref, gpost_ref, y_ref):
    a_g = jnp.concatenate(
        [(oa_ref[0, c].astype(F32) * _silu(za_ref[0, :, c * 128:(c + 1) * 128].astype(F32))).astype(BF16)
         for c in range(NHP)], axis=-1)
    pieces = []
    for h in range(HC):
        hs = slice(h * DC, (h + 1) * DC)
        s = lax.dot_general(qc_ref[0, :, hs], mk_ref[0, :, hs], NT_DIMS, preferred_element_type=F32)
        p = jnp.exp(s - jnp.max(s, axis=-1, keepdims=True))
        l = jnp.sum(p, axis=-1, keepdims=True)
        oc = jnp.dot(p.astype(BF16), mv_ref[0, :, hs], preferred_element_type=F32) / l
        pieces.append((oc * _silu(zc_ref[0, :, hs].astype(F32))).astype(BF16))
    c_g = jnp.concatenate(pieces, axis=-1)
    y_ref[0] = _gated_tail(a_g, obg_ref[0], c_g, gt_ref.at[0], x_ref[0],
                           wpa_ref, wpb_ref, wpc_ref, wout_ref, gpost_ref)


def _tail_prompt(oa, za, obg, qc, zc, mk, mv, gt, x, wpa, wpb, wpc, wout, gpost, tm):
    B, S, _ = x.shape
    row = lambda b, i: (b, i, 0)
    const = lambda b, i: (0, 0)
    return pl.pallas_call(
        _tail_prompt_kernel,
        out_shape=jax.ShapeDtypeStruct((B, S, D_MODEL), F32),
        grid=(B, S // tm),
        in_specs=[
            pl.BlockSpec((1, NHP, tm, 128), lambda b, i: (b, 0, i, 0)),
            pl.BlockSpec((1, tm, EA), row), pl.BlockSpec((1, tm, EB), row),
            pl.BlockSpec((1, tm, EC), row), pl.BlockSpec((1, tm, EC), row),
            pl.BlockSpec((1, N_MEM, EC), lambda b, i: (b, 0, 0)),
            pl.BlockSpec((1, N_MEM, EC), lambda b, i: (b, 0, 0)),
            pl.BlockSpec((1, tm, 3 * D_MODEL), row), pl.BlockSpec((1, tm, D_MODEL), row),
            pl.BlockSpec((EA, D_MODEL), const), pl.BlockSpec((EB, D_MODEL), const),
            pl.BlockSpec((EC, D_MODEL), const), pl.BlockSpec((D_MODEL, D_MODEL), const),
            pl.BlockSpec((1, D_MODEL), const)],
        out_specs=pl.BlockSpec((1, tm, D_MODEL), row),
        compiler_params=_params("arbitrary", "arbitrary"),
        name="tail_prompt",
    )(oa, za, obg, qc, zc, mk, mv, gt, x, wpa, wpb, wpc, wout, gpost)


def _tail_sample_kernel(oa_ref, za_ref, ob_ref, zb_ref, oc_ref, zc_ref, gt_ref, x_ref,
                        wpa_ref, wpb_ref, wpc_ref, wout_ref, gpost_ref, y_ref):
    a_g = (oa_ref[...] * _silu(za_ref[...].astype(F32))).astype(BF16)
    b_g = (ob_ref[...] * _silu(zb_ref[...].astype(F32))).astype(BF16)
    c_g = (oc_ref[...] * _silu(zc_ref[...].astype(F32))).astype(BF16)
    y_ref[...] = _gated_tail(a_g, b_g, c_g, gt_ref, x_ref[...],
                             wpa_ref, wpb_ref, wpc_ref, wout_ref, gpost_ref)


def _tail_sample(oa, za, ob, zb, oc, zc, gt, x, wpa, wpb, wpc, wout, gpost, tm):
    M = x.shape[0]
    row = lambda i: (i, 0)
    const = lambda i: (0, 0)
    widths = (EA, EA, EB, EB, EC, EC, 3 * D_MODEL, D_MODEL)
    return pl.pallas_call(
        _tail_sample_kernel,
        out_shape=jax.ShapeDtypeStruct((M, D_MODEL), F32),
        grid=(M // tm,),
        in_specs=[pl.BlockSpec((tm, n), row) for n in widths] + [
            pl.BlockSpec((EA, D_MODEL), const), pl.BlockSpec((EB, D_MODEL), const),
            pl.BlockSpec((EC, D_MODEL), const), pl.BlockSpec((D_MODEL, D_MODEL), const),
            pl.BlockSpec((1, D_MODEL), const)],
        out_specs=pl.BlockSpec((tm, D_MODEL), row),
        compiler_params=_params("arbitrary"),
        name="tail_sample",
    )(oa, za, ob, zb, oc, zc, gt, x, wpa, wpb, wpc, wout, gpost)


def _sample_tables(rel_bias, wb, past, T):
    def mult_of(dl, real):
        m = np.zeros(dl.shape, np.float32)
        for (w, d) in PATTERNS:
            m += (real & (dl % d == 0) & (dl <= w)).astype(np.float32)
        return m

    def tables(dl, real):
        row = np.arange(QROWS)[:, None]
        real = real & (dl >= 0) & (past + row - dl >= 0) & (wb + row - dl >= 0)
        m = mult_of(dl, real)
        dlc = np.clip(dl, 0, MAX_DIST)
        m[T:] = m[0]
        bias = jnp.where(jnp.asarray(m > 0)[None], _bias_by_bucket(rel_bias, _t5_bucket_ids(dlc)), NEG)
        return bias, jnp.asarray(m)

    i = np.arange(QROWS)[:, None]
    pos = np.arange(wb)[None, :]
    cache_b, cache_m = tables(wb + i - pos, np.ones((QROWS, wb), bool))
    j = np.arange(QROWS)[None, :]
    new_b, new_m = tables(i - j, np.broadcast_to(j < T, (QROWS, QROWS)))
    return cache_b, cache_m, new_b, new_m


def _sample_attn_kernel(q_ref, kt_ref, vt_ref, kn_ref, vn_ref, cb_ref, cm_ref, nb_ref, nm_ref,
                        qc_ref, mk_ref, mv_ref, xm_ref, oa_ref, oc_ref, *, T):
    zpad = jnp.zeros((QROWS - T, EA), F32)
    q8 = jnp.concatenate([q_ref[0], zpad], axis=0).astype(BF16)
    kn8 = jnp.concatenate([kn_ref[0], zpad], axis=0).astype(BF16)
    vn8 = jnp.concatenate([vn_ref[0], zpad], axis=0).astype(BF16)
    outs = []
    for h in range(HA):
        hs = slice(h * DA, (h + 1) * DA)
        qh = q8[:, hs]
        s = jnp.dot(qh, kt_ref[0, h].astype(BF16), preferred_element_type=F32) + cb_ref[h]
        sn = lax.dot_general(qh, kn8[:, hs], NT_DIMS, preferred_element_type=F32) + nb_ref[h]
        m = jnp.maximum(jnp.max(s, axis=-1, keepdims=True), jnp.max(sn, axis=-1, keepdims=True))
        p = jnp.exp(s - m) * cm_ref[...]
        pn = jnp.exp(sn - m) * nm_ref[...]
        l = jnp.sum(p, axis=-1, keepdims=True) + jnp.sum(pn, axis=-1, keepdims=True)
        o = lax.dot_general(p.astype(BF16), vt_ref[0, h].astype(BF16), NT_DIMS, preferred_element_type=F32)
        o = o + jnp.dot(pn.astype(BF16), vn8[:, hs], preferred_element_type=F32)
        outs.append(o / l)
    oa_ref[0] = jnp.concatenate(outs, axis=-1)[:T]
    cpad = jnp.zeros((QROWS - T, EC), F32)
    qc8 = jnp.concatenate([qc_ref[0], cpad], axis=0)
    qc_heads = jnp.concatenate([qc8[:, h * DC:(h + 1) * DC] for h in range(HC)], axis=0).astype(BF16)
    sc = lax.dot_general(qc_heads, mk_ref[0].astype(BF16), NT_DIMS, preferred_element_type=F32) + xm_ref[...]
    pc = jnp.exp(sc - jnp.max(sc, axis=-1, keepdims=True))
    lc = jnp.sum(pc, axis=-1, keepdims=True)
    oc = jnp.dot(pc.astype(BF16), mv_ref[0].astype(BF16), preferred_element_type=F32) / lc
    oc_ref[0] = jnp.concatenate([oc[h * QROWS:h * QROWS + T] for h in range(HC)], axis=-1)


def _sample_attn(q, k_new, v_new, cache_kt, cache_vt, tables, qc, mem_k, mem_v):
    Bd, T, _ = q.shape
    wb = cache_kt.shape[-1]
    cache_b, cache_m, new_b, new_m = tables
    per_b3 = lambda b: (b, 0, 0)
    per_b4 = lambda b: (b, 0, 0, 0)
    own_head = jnp.asarray(np.where(
        np.arange(HC * QROWS)[:, None] // QROWS == np.arange(N_MEM * HC)[None, :] % HC, 0.0, NEG
    ).astype(np.float32))
    return pl.pallas_call(
        functools.partial(_sample_attn_kernel, T=T),
        out_shape=(jax.ShapeDtypeStruct((Bd, T, EA), F32), jax.ShapeDtypeStruct((Bd, T, EC), F32)),
        grid=(Bd,),
        in_specs=[pl.BlockSpec((1, T, EA), per_b3),
                  pl.BlockSpec((1, HA, DA, wb), per_b4), pl.BlockSpec((1, HA, DA, wb), per_b4),
                  pl.BlockSpec((1, T, EA), per_b3), pl.BlockSpec((1, T, EA), per_b3),
                  pl.BlockSpec((HA, QROWS, wb), lambda b: (0, 0, 0)),
                  pl.BlockSpec((QROWS, wb), lambda b: (0, 0)),
                  pl.BlockSpec((HA, QROWS, QROWS), lambda b: (0, 0, 0)),
                  pl.BlockSpec((QROWS, QROWS), lambda b: (0, 0)),
                  pl.BlockSpec((1, T, EC), per_b3),
                  pl.BlockSpec((1, N_MEM * HC, DC), per_b3), pl.BlockSpec((1, N_MEM * HC, DC), per_b3),
                  pl.BlockSpec((HC * QROWS, N_MEM * HC), lambda b: (0, 0))],
        out_specs=(pl.BlockSpec((1, T, EA), per_b3), pl.BlockSpec((1, T, EC), per_b3)),
        compiler_params=_params("arbitrary"),
        name="sample_attn",
    )(q, cache_kt, cache_vt, k_new, v_new, cache_b, cache_m, new_b, new_m, qc, mem_k, mem_v, own_head)


def _conv_sample_kernel(st_ref, glu_ref, cw_ref, cb_ref, lg_ref, lb_ref, o_ref, *, T, rc):
    n_hist = CONV_W - 1
    Bd = st_ref.shape[1]
    for i in range(T):
        def chunk(c, carry, i=i):
            rows = pl.ds(pl.multiple_of(c * rc, rc), rc)
            acc = jnp.zeros((rc, EB), F32)
            for j in range(CONV_W):
                src = st_ref[i + j, rows, :] if i + j < n_hist else glu_ref[i + j - n_hist, rows, :]
                acc = acc + src * cw_ref[j:j + 1, :]
            o_ref[i, rows, :] = _layernorm_silu(acc + cb_ref[...], lg_ref[...], lb_ref[...])
            return carry
        lax.fori_loop(0, Bd // rc, chunk, 0)


def _conv_sample(state_t, glu_t, cw, cb, lg, lb, rc):
    T, Bd, _ = glu_t.shape
    return pl.pallas_call(
        functools.partial(_conv_sample_kernel, T=T, rc=rc),
        out_shape=jax.ShapeDtypeStruct((T, Bd, EB), F32),
        compiler_params=pltpu.CompilerParams(vmem_limit_bytes=V7X_VMEM_LIMIT),
        name="conv_sample",
    )(state_t, glu_t, cw, cb, lg, lb)


def kernel(x_prompt, x_sample, mem_prompt, cache_k_win, cache_v_win, state_conv, cache_k_mem, cache_v_mem,
           rel_bias, g_pre, w_in, g_mem, w_mem_kv, conv_w, conv_b, ln_g, ln_b, w_proj_a, w_proj_b,
           w_proj_c, w_out, g_post):
    depth = g_pre.shape[0]
    assert depth == 1, "single-layer step"
    B, S, _ = x_prompt.shape
    Bd, T, _ = x_sample.shape
    wb = cache_k_win.shape[2]
    past = wb
    assert wb == MAX_DIST and S % (16 * QB) == 0 and T <= QROWS

    l = 0
    w_in_b = w_in[l].astype(BF16)
    wpa, wpb, wpc = w_proj_a[l].astype(BF16), w_proj_b[l].astype(BF16), w_proj_c[l].astype(BF16)
    wout = w_out[l].astype(BF16)
    gpre, gpost = g_pre[l][None], g_post[l][None]
    cb, lg, lb = conv_b[l][None], ln_g[l][None], ln_b[l][None]
    cw = conv_w[l]

    (qkv, k_keep, v_keep, za, glu, zb, qc, zc, gt, glu_tail) = _inproj_prompt(x_prompt, gpre, w_in_b, tm=256)
    mk_f, mv_f, mk_b, mv_b = _memkv(mem_prompt.reshape(B * N_MEM, D_MODEL), g_mem[l][None],
                                    w_mem_kv[l].astype(BF16), tm=512)
    oa = _band_attn(qkv, _band_tables(rel_bias), unroll=8)
    obg = _conv_prompt(glu, zb, cw, cb, lg, lb, tm=512, rc=32)
    y_p = _tail_prompt(oa, za, obg, qc, zc, mk_b.reshape(B, N_MEM, EC), mv_b.reshape(B, N_MEM, EC),
                       gt, x_prompt, wpa, wpb, wpc, wout, gpost, tm=256)

    (q_s, k_s, v_s, za_s, glu_s, zb_s, qc_s, zc_s, gt_s) = _inproj_sample(
        x_sample.reshape(Bd * T, D_MODEL), gpre, w_in_b, tm=Bd * T)
    cache_kt = jnp.transpose(cache_k_win[l], (0, 2, 3, 1))
    cache_vt = jnp.transpose(cache_v_win[l], (0, 2, 3, 1))
    oa_s, oc_s = _sample_attn(q_s.reshape(Bd, T, EA), k_s.reshape(Bd, T, EA), v_s.reshape(Bd, T, EA),
                              cache_kt, cache_vt, _sample_tables(rel_bias, wb, past, T),
                              qc_s.reshape(Bd, T, EC), cache_k_mem[l].reshape(Bd, N_MEM * HC, DC),
                              cache_v_mem[l].reshape(Bd, N_MEM * HC, DC))
    state_t = jnp.transpose(state_conv[l], (1, 0, 2))
    glu_t = jnp.transpose(glu_s.reshape(Bd, T, EB), (1, 0, 2))
    ob_t = _conv_sample(state_t, glu_t, cw, cb, lg, lb, rc=32)
    ob_s = jnp.transpose(ob_t, (1, 0, 2)).reshape(Bd * T, EB)
    y_s = _tail_sample(oa_s.reshape(Bd * T, EA), za_s, ob_s, zb_s,
                       oc_s.reshape(Bd * T, EC), zc_s, gt_s, x_sample.reshape(Bd * T, D_MODEL),
                       wpa, wpb, wpc, wout, gpost, tm=Bd * T)

    conv_state_s = jnp.transpose(jnp.concatenate([state_t[T:], glu_t], axis=0), (1, 0, 2))
    return (y_p, y_s.reshape(Bd, T, D_MODEL),
            k_keep.reshape(1, B, wb, HA, DA), v_keep.reshape(1, B, wb, HA, DA),
            glu_tail[:, HALO - (CONV_W - 1):][None],
            mk_f.reshape(1, B, N_MEM, HC, DC), mv_f.reshape(1, B, N_MEM, HC, DC),
            k_s.reshape(1, Bd, T, HA, DA), v_s.reshape(1, Bd, T, HA, DA),
            conv_state_s[None])
```

```python
import functools

import jax
import jax.numpy as jnp
import numpy as np
from jax import lax
from jax.experimental import pallas as pl
from jax.experimental.pallas import tpu as pltpu

F32 = jnp.float32
BF16 = jnp.bfloat16

D_MODEL = 1024
HA, DA = 12, 64
EA = HA * DA
NHP = HA // 2
PATTERNS = ((128, 1), (512, 4), (2048, 16))
NW = 128
QB = 128
EB = 768
CONV_W = 31
HC, DC = 4, 128
EC = HC * DC
N_MEM = 256
N_BUCKETS = 32
MAX_DIST = 2048
EPS = 1e-6
NEG = -1e30
LOG2E = 1.4426950408889634

C_Q, C_K, C_V, C_ZA = 0, EA, 2 * EA, 3 * EA
C_U = 4 * EA
C_G = C_U + EB
C_ZB = C_U + 2 * EB
C_QC = C_ZB + EB
C_ZC = C_QC + EC
C_GT = C_ZC + EC
IN_COLS = C_GT + 3 * D_MODEL

V7X_VMEM_LIMIT = 56 * 1024 * 1024
HALO = 32
QROWS = 8
NT_DIMS = (((1,), (1,)), ((), ()))


def _t5_bucket_ids(n):
    exact = N_BUCKETS // 2
    nf = np.maximum(n, 1).astype(np.float32)
    scale = np.float32(N_BUCKETS - exact) / np.log(np.float32(MAX_DIST) / np.float32(exact))
    large = exact + (np.log(nf / np.float32(exact)) * scale).astype(np.int32)
    large = np.minimum(large, N_BUCKETS - 1)
    return np.where(n < exact, n, large).astype(np.int32)


def _bias_by_bucket(rel_bias, ids):
    onehot = (ids[None] == np.arange(N_BUCKETS).reshape((-1,) + (1,) * ids.ndim)).astype(np.float32)
    return jnp.tensordot(rel_bias.astype(F32).T, jnp.asarray(onehot), axes=1,
                         precision=lax.Precision.HIGHEST)


def _sigmoid(x):
    return 1.0 / (1.0 + jnp.exp(-x))


def _silu(x):
    return x * _sigmoid(x)


def _rmsnorm_f32(x, g):
    return x * lax.rsqrt(jnp.mean(x * x, axis=-1, keepdims=True) + EPS) * g


def _params(*sem):
    return pltpu.CompilerParams(dimension_semantics=sem, vmem_limit_bytes=V7X_VMEM_LIMIT)


def _inproj_prompt_kernel(x_ref, g_ref, w_ref, qkv_ref, kf_ref, vf_ref, za_ref, glu_ref, zb_ref,
                          qc_ref, zc_ref, gt_ref, tail_ref):
    xn = _rmsnorm_f32(x_ref[0], g_ref[...]).astype(BF16)

    def mm(lo, n):
        return jnp.dot(xn, w_ref[:, lo:lo + n], preferred_element_type=F32)

    def put_pairs(res, base):
        for c in range(NHP):
            qkv_ref[0, base + c] = res[:, c * 128:(c + 1) * 128].astype(BF16)

    put_pairs(mm(C_Q, EA) * (DA ** -0.5 * LOG2E), 0)
    k = mm(C_K, EA)
    put_pairs(k, NHP)
    kf_ref[0] = k
    v = mm(C_V, EA)
    put_pairs(v, 2 * NHP)
    vf_ref[0] = v
    za_ref[0] = mm(C_ZA, EA).astype(BF16)
    glu = mm(C_U, EB) * _sigmoid(mm(C_G, EB))
    glu_ref[0] = glu.astype(BF16)
    tail_ref[0] = glu[glu.shape[0] - HALO:, :]
    zb_ref[0] = mm(C_ZB, EB).astype(BF16)
    qc_ref[0] = (mm(C_QC, EC) * (DC ** -0.5)).astype(BF16)
    zc_ref[0] = mm(C_ZC, EC).astype(BF16)
    for c in range(3):
        gt_ref[0, :, c * D_MODEL:(c + 1) * D_MODEL] = mm(C_GT + c * D_MODEL, D_MODEL).astype(BF16)


def _inproj_prompt(x, g, w, tm):
    B, S, _ = x.shape
    nt = S // tm
    wb = min(MAX_DIST, S)
    first_kept = (S - wb) // tm
    row = lambda b, i: (b, i, 0)
    kept = lambda b, i: (b, jnp.maximum(i - first_kept, 0), 0)
    out_shape = (
        jax.ShapeDtypeStruct((B, 3 * NHP, S, 128), BF16),
        jax.ShapeDtypeStruct((B, wb, EA), F32),
        jax.ShapeDtypeStruct((B, wb, EA), F32),
        jax.ShapeDtypeStruct((B, S, EA), BF16),
        jax.ShapeDtypeStruct((B, S, EB), BF16),
        jax.ShapeDtypeStruct((B, S, EB), BF16),
        jax.ShapeDtypeStruct((B, S, EC), BF16),
        jax.ShapeDtypeStruct((B, S, EC), BF16),
        jax.ShapeDtypeStruct((B, S, 3 * D_MODEL), BF16),
        jax.ShapeDtypeStruct((B, HALO, EB), F32),
    )
    out_specs = (
        pl.BlockSpec((1, 3 * NHP, tm, 128), lambda b, i: (b, 0, i, 0)),
        pl.BlockSpec((1, tm, EA), kept),
        pl.BlockSpec((1, tm, EA), kept),
        pl.BlockSpec((1, tm, EA), row),
        pl.BlockSpec((1, tm, EB), row),
        pl.BlockSpec((1, tm, EB), row),
        pl.BlockSpec((1, tm, EC), row),
        pl.BlockSpec((1, tm, EC), row),
        pl.BlockSpec((1, tm, 3 * D_MODEL), row),
        pl.BlockSpec((1, HALO, EB), lambda b, i: (b, 0, 0)),
    )
    return pl.pallas_call(
        _inproj_prompt_kernel,
        out_shape=out_shape,
        grid=(B, nt),
        in_specs=[
            pl.BlockSpec((1, tm, D_MODEL), row),
            pl.BlockSpec((1, D_MODEL), lambda b, i: (0, 0)),
            pl.BlockSpec((D_MODEL, IN_COLS), lambda b, i: (0, 0), pipeline_mode=pl.Buffered(1)),
        ],
        out_specs=out_specs,
        compiler_params=_params("arbitrary", "arbitrary"),
        name="inproj_prompt",
    )(x, g, w)


def _inproj_sample_kernel(x_ref, g_ref, w_ref, q_ref, k_ref, v_ref, za_ref, glu_ref, zb_ref,
                          qc_ref, zc_ref, gt_ref):
    xn = _rmsnorm_f32(x_ref[...], g_ref[...]).astype(BF16)

    def mm(lo, n):
        return jnp.dot(xn, w_ref[:, lo:lo + n], preferred_element_type=F32)

    q_ref[...] = mm(C_Q, EA) * (DA ** -0.5)
    k_ref[...] = mm(C_K, EA)
    v_ref[...] = mm(C_V, EA)
    za_ref[...] = mm(C_ZA, EA).astype(BF16)
    glu_ref[...] = mm(C_U, EB) * _sigmoid(mm(C_G, EB))
    zb_ref[...] = mm(C_ZB, EB).astype(BF16)
    qc_ref[...] = mm(C_QC, EC) * (DC ** -0.5)
    zc_ref[...] = mm(C_ZC, EC).astype(BF16)
    for c in range(3):
        gt_ref[:, c * D_MODEL:(c + 1) * D_MODEL] = mm(C_GT + c * D_MODEL, D_MODEL).astype(BF16)


def _inproj_sample(x, g, w, tm):
    M = x.shape[0]
    row = lambda i: (i, 0)
    widths = (EA, EA, EA, EA, EB, EB, EC, EC, 3 * D_MODEL)
    dtypes = (F32, F32, F32, BF16, F32, BF16, F32, BF16, BF16)
    return pl.pallas_call(
        _inproj_sample_kernel,
        out_shape=tuple(jax.ShapeDtypeStruct((M, n), dt) for n, dt in zip(widths, dtypes)),
        grid=(M // tm,),
        in_specs=[
            pl.BlockSpec((tm, D_MODEL), row),
            pl.BlockSpec((1, D_MODEL), lambda i: (0, 0)),
            pl.BlockSpec((D_MODEL, IN_COLS), lambda i: (0, 0), pipeline_mode=pl.Buffered(1)),
        ],
        out_specs=tuple(pl.BlockSpec((tm, n), row) for n in widths),
        compiler_params=_params("arbitrary"),
        name="inproj_sample",
    )(x, g, w)


def _memkv_kernel(m_ref, g_ref, w_ref, kf_ref, vf_ref, kb_ref, vb_ref):
    xn = _rmsnorm_f32(m_ref[...], g_ref[...]).astype(BF16)
    k = jnp.dot(xn, w_ref[:, :EC], preferred_element_type=F32)
    v = jnp.dot(xn, w_ref[:, EC:], preferred_element_type=F32)
    kf_ref[...] = k
    vf_ref[...] = v
    kb_ref[...] = k.astype(BF16)
    vb_ref[...] = v.astype(BF16)


def _memkv(mem, g, w, tm):
    M = mem.shape[0]
    row = lambda i: (i, 0)
    return pl.pallas_call(
        _memkv_kernel,
        out_shape=(jax.ShapeDtypeStruct((M, EC), F32), jax.ShapeDtypeStruct((M, EC), F32),
                   jax.ShapeDtypeStruct((M, EC), BF16), jax.ShapeDtypeStruct((M, EC), BF16)),
        grid=(M // tm,),
        in_specs=[pl.BlockSpec((tm, D_MODEL), row),
                  pl.BlockSpec((1, D_MODEL), lambda i: (0, 0)),
                  pl.BlockSpec((D_MODEL, 2 * EC), lambda i: (0, 0))],
        out_specs=tuple(pl.BlockSpec((tm, EC), row) for _ in range(4)),
        compiler_params=_params("arbitrary"),
        name="memkv",
    )(mem, g, w)


def _band_tables(rel_bias):
    iq = np.arange(QB)[:, None]
    ik = np.arange(2 * QB)[None, :]
    dist = iq - ik + QB
    band = (dist >= 0) & (dist <= NW)
    out = []
    for (_, d) in PATTERNS:
        bias = LOG2E * _bias_by_bucket(rel_bias, _t5_bucket_ids(np.clip(dist, 0, NW) * d))
        later = jnp.where(jnp.asarray(band)[None], bias, NEG)
        first = jnp.where(jnp.asarray(band & (ik >= QB))[None], bias, NEG)
        t = jnp.stack([first, later], axis=1)
        out.append(t.reshape(NHP, 2, 2, QB, 2 * QB).transpose(0, 2, 1, 3, 4)
                   .reshape(NHP, 2, 2 * QB, 2 * QB))
    return jnp.stack(out)


def _band_attn_kernel(q_ref, k_ref, v_ref, t_ref, oa_ref,
                      nat32, p4_32, p4_ref, p16_ref, a4_ref, l4_ref, m4_ref, a16_ref, l16_ref, m16_ref,
                      *, S, unroll):
    n_units = S // QB
    lane = lax.broadcasted_iota(jnp.int32, (QB, 128), 1)
    head0 = lane < DA
    keep0 = head0.astype(F32).astype(BF16)
    keep1 = (1.0 - head0.astype(F32)).astype(BF16)
    ones = jnp.ones((2 * QB, 128), BF16)
    n4 = S // 4

    for t, src in enumerate((q_ref, k_ref, v_ref)):
        def widen(c, carry, src=src):
            r0 = pl.multiple_of(c * 256, 256)
            nat32[pl.ds(r0, 256), :] = src[0, 0, pl.ds(r0, 256), :].astype(F32)
            return carry
        lax.fori_loop(0, S // 256, widen, 0)
        for r in range(4):
            def by4(c, carry, r=r, t=t):
                u0 = pl.multiple_of(c * 256, 256)
                x = nat32[pl.ds(4 * u0 + r, 256, stride=4), :]
                p4_32[pl.ds(r * n4 + u0, 256), :] = x
                p4_ref[t, pl.ds(r * n4 + u0, 256), :] = x.astype(BF16)
                return carry
            lax.fori_loop(0, n4 // 256, by4, 0)
        for r4 in range(4):
            for s in range(4):
                x = p4_32[pl.ds(r4 * n4 + s, n4 // 4, stride=4), :]
                p16_ref[t, pl.ds((4 * s + r4) * (n4 // 4), n4 // 4), :] = x.astype(BF16)

    def unit(u, pat, d, qsrc, ksrc, vsrc):
        nqb = n_units // d
        row = pl.multiple_of(u * QB, QB)
        prow = pl.multiple_of(jnp.maximum(u - 1, 0) * QB, QB)
        q2 = qsrc[pl.ds(row, QB), :]
        kk = jnp.concatenate([ksrc[pl.ds(prow, QB), :], ksrc[pl.ds(row, QB), :]], axis=0)
        vv = jnp.concatenate([vsrc[pl.ds(prow, QB), :], vsrc[pl.ds(row, QB), :]], axis=0)
        qs = jnp.concatenate([q2 * keep0, q2 * keep1], axis=0)
        s = lax.dot_general(qs, kk, NT_DIMS, preferred_element_type=F32)
        s = s + t_ref[pat, 0, jnp.minimum(u & (nqb - 1), 1)]
        m = jnp.max(s, axis=-1, keepdims=True)
        p = jnp.exp2(s - m).astype(BF16)
        oe = jnp.dot(p, jnp.concatenate([vv, ones], axis=1), preferred_element_type=F32)
        mb = jnp.broadcast_to(m, (2 * QB, 128))
        return (jnp.where(head0, oe[:QB, :128], oe[QB:, :128]),
                jnp.where(head0, oe[:QB, 128:], oe[QB:, 128:]),
                jnp.where(head0, mb[:QB], mb[QB:]))

    def strided_pattern(pat, d, src, acc_ref, den_ref, max_ref):
        nqb = n_units // d

        def body(u, carry):
            acc, den, mx = unit(u, pat, d, src.at[0], src.at[1], src.at[2])
            start = u // nqb + (u & (nqb - 1)) * (d * QB)
            acc_ref[pl.ds(start, QB, stride=d), :] = acc
            den_ref[pl.ds(start, QB, stride=d), :] = den
            max_ref[pl.ds(start, QB, stride=d), :] = mx
            return carry
        lax.fori_loop(0, n_units, body, 0, unroll=unroll)

    strided_pattern(2, 16, p16_ref, a16_ref, l16_ref, m16_ref)
    strided_pattern(1, 4, p4_ref, a4_ref, l4_ref, m4_ref)

    def dense(u, carry):
        a1, l1, m1 = unit(u, 0, 1, q_ref.at[0, 0], k_ref.at[0, 0], v_ref.at[0, 0])
        rows = pl.ds(pl.multiple_of(u * QB, QB), QB)
        m4, m16 = m4_ref[rows, :], m16_ref[rows, :]
        mx = jnp.maximum(jnp.maximum(m1, m4), m16)
        e1, e4, e16 = jnp.exp2(m1 - mx), jnp.exp2(m4 - mx), jnp.exp2(m16 - mx)
        num = a1 * e1 + a4_ref[rows, :] * e4 + a16_ref[rows, :] * e16
        den = l1 * e1 + l4_ref[rows, :] * e4 + l16_ref[rows, :] * e16
        oa_ref[0, 0, rows, :] = (num / den).astype(BF16)
        return carry
    lax.fori_loop(0, n_units, dense, 0, unroll=unroll)


def _band_attn(qkv, tables, unroll):
    B, _, S, _ = qkv.shape
    blk = (1, 1, S, 128)
    return pl.pallas_call(
        functools.partial(_band_attn_kernel, S=S, unroll=unroll),
        out_shape=jax.ShapeDtypeStruct((B, NHP, S, 128), BF16),
        grid=(B, NHP),
        in_specs=[pl.BlockSpec(blk, lambda b, h: (b, h, 0, 0)),
                  pl.BlockSpec(blk, lambda b, h: (b, NHP + h, 0, 0)),
                  pl.BlockSpec(blk, lambda b, h: (b, 2 * NHP + h, 0, 0)),
                  pl.BlockSpec((3, 1, 2, 2 * QB, 2 * QB), lambda b, h: (0, h, 0, 0, 0))],
        out_specs=pl.BlockSpec(blk, lambda b, h: (b, h, 0, 0)),
        scratch_shapes=[pltpu.VMEM((S, 128), F32), pltpu.VMEM((S, 128), F32),
                        pltpu.VMEM((3, S, 128), BF16), pltpu.VMEM((3, S, 128), BF16)]
                       + [pltpu.VMEM((S, 128), F32)] * 6,
        compiler_params=_params("arbitrary", "arbitrary"),
        name="band_attn",
    )(qkv, qkv, qkv, tables)


def _layernorm_silu(c, g, b):
    mu = jnp.mean(c, axis=-1, keepdims=True)
    cc = c - mu
    var = jnp.mean(cc * cc, axis=-1, keepdims=True)
    return _silu(cc * lax.rsqrt(var + EPS) * g + b)


def _conv_prompt_kernel(glu_ref, halo_ref, zb_ref, cw_ref, cb_ref, lg_ref, lb_ref, o_ref, full_ref,
                        conv_ref, *, tm, rc, rn):
    i = pl.program_id(1)
    halo = halo_ref[0].astype(F32)
    full_ref[0, 0:HALO, :] = jnp.where(i == 0, jnp.zeros_like(halo), halo)
    full_ref[0, HALO:HALO + tm, :] = glu_ref[0].astype(F32)
    n_shift = HALO + tm - 8
    for s in range(1, 8):
        full_ref[s, 0:n_shift, :] = full_ref[0, s:s + n_shift, :]
    first_tap = HALO - (CONV_W - 1)

    def taps(c, carry):
        base = pl.multiple_of(c * rc, rc)
        accs = [jnp.zeros((8, EB), F32) for _ in range(rc // 8)]
        for j in range(CONV_W):
            off = first_tap + j
            w8 = cw_ref[j]
            for g in range(rc // 8):
                accs[g] = accs[g] + full_ref[off % 8, pl.ds(base + (off // 8) * 8 + 8 * g, 8), :] * w8
        conv_ref[pl.ds(base, rc), :] = jnp.concatenate(accs, axis=0)
        return carry

    lax.fori_loop(0, tm // rc, taps, 0)

    def norm_gate(c, carry):
        rows = pl.ds(pl.multiple_of(c * rn, rn), rn)
        ob = _layernorm_silu(conv_ref[rows, :] + cb_ref[...], lg_ref[...], lb_ref[...])
        o_ref[0, rows, :] = (ob * _silu(zb_ref[0, rows, :].astype(F32))).astype(BF16)
        return carry

    lax.fori_loop(0, tm // rn, norm_gate, 0, unroll=2)


def _conv_prompt(glu, zb, cw, cb, lg, lb, tm, rc, rn):
    B, S, _ = glu.shape
    row = lambda b, i: (b, i, 0)
    vec = lambda b, i: (0, 0)
    return pl.pallas_call(
        functools.partial(_conv_prompt_kernel, tm=tm, rc=rc, rn=rn),
        out_shape=jax.ShapeDtypeStruct((B, S, EB), BF16),
        grid=(B, S // tm),
        in_specs=[pl.BlockSpec((1, tm, EB), row),
                  pl.BlockSpec((1, HALO, EB), lambda b, i: (b, jnp.maximum(i * (tm // HALO) - 1, 0), 0)),
                  pl.BlockSpec((1, tm, EB), row),
                  pl.BlockSpec((CONV_W, 8, EB), lambda b, i: (0, 0, 0)),
                  pl.BlockSpec((1, EB), vec), pl.BlockSpec((1, EB), vec), pl.BlockSpec((1, EB), vec)],
        out_specs=pl.BlockSpec((1, tm, EB), row),
        scratch_shapes=[pltpu.VMEM((8, HALO + tm, EB), F32), pltpu.VMEM((tm, EB), F32)],
        compiler_params=_params("arbitrary", "arbitrary"),
        name="conv_prompt",
    )(glu, glu, zb, cw, cb, lg, lb)


def _gated_tail(a_g, b_g, c_g, gt_ref, x, wpa_ref, wpb_ref, wpc_ref, wout_ref, gpost_ref):
    def gate(k):
        return _sigmoid(gt_ref[:, k * D_MODEL:(k + 1) * D_MODEL].astype(F32))

    mix = gate(0) * jnp.dot(a_g, wpa_ref[...], preferred_element_type=F32)
    mix = mix + gate(1) * jnp.dot(b_g, wpb_ref[...], preferred_element_type=F32)
    mix = mix + gate(2) * jnp.dot(c_g, wpc_ref[...], preferred_element_type=F32)
    z = jnp.dot(mix.astype(BF16), wout_ref[...], preferred_element_type=F32)
    return x + _rmsnorm_f32(z, gpost_ref[...])


def _tail_prompt_kernel(oa_ref, za_ref, obg_ref, qc_ref, zc_ref, mk_ref, mv_ref, gt_ref, x_ref,
                        wpa_ref, wpb_ref, wpc_ref, wout_ref, gpost_ref, y_ref):
    a_g = jnp.concatenate(
        [(oa_ref[0, c].astype(F32) * _silu(za_ref[0, :, c * 128:(c + 1) * 128].astype(F32))).astype(BF16)
         for c in range(NHP)], axis=-1)
    pieces = []
    for h in range(HC):
        hs = slice(h * DC, (h + 1) * DC)
        s = lax.dot_general(qc_ref[0, :, hs], mk_ref[0, :, hs], NT_DIMS, preferred_element_type=F32)
        p = jnp.exp(s - jnp.max(s, axis=-1, keepdims=True))
        l = jnp.sum(p, axis=-1, keepdims=True)
        oc = jnp.dot(p.astype(BF16), mv_ref[0, :, hs], preferred_element_type=F32) / l
        pieces.append((oc * _silu(zc_ref[0, :, hs].astype(F32))).astype(BF16))
    c_g = jnp.concatenate(pieces, axis=-1)
    y_ref[0] = _gated_tail(a_g, obg_ref[0], c_g, gt_ref.at[0], x_ref[0],
                           wpa_ref, wpb_ref, wpc_ref, wout_ref, gpost_ref)


def _tail_prompt(oa, za, obg, qc, zc, mk, mv, gt, x, wpa, wpb, wpc, wout, gpost, tm):
    B, S, _ = x.shape
    row = lambda b, i: (b, i, 0)
    const = lambda b, i: (0, 0)
    return pl.pallas_call(
        _tail_prompt_kernel,
        out_shape=jax.ShapeDtypeStruct((B, S, D_MODEL), F32),
        grid=(B, S // tm),
        in_specs=[
            pl.BlockSpec((1, NHP, tm, 128), lambda b, i: (b, 0, i, 0)),
            pl.BlockSpec((1, tm, EA), row), pl.BlockSpec((1, tm, EB), row),
            pl.BlockSpec((1, tm, EC), row), pl.BlockSpec((1, tm, EC), row),
            pl.BlockSpec((1, N_MEM, EC), lambda b, i: (b, 0, 0)),
            pl.BlockSpec((1, N_MEM, EC), lambda b, i: (b, 0, 0)),
            pl.BlockSpec((1, tm, 3 * D_MODEL), row), pl.BlockSpec((1, tm, D_MODEL), row),
            pl.BlockSpec((EA, D_MODEL), const), pl.BlockSpec((EB, D_MODEL), const),
            pl.BlockSpec((EC, D_MODEL), const), pl.BlockSpec((D_MODEL, D_MODEL), const),
            pl.BlockSpec((1, D_MODEL), const)],
        out_specs=pl.BlockSpec((1, tm, D_MODEL), row),
        compiler_params=_params("arbitrary", "arbitrary"),
        name="tail_prompt",
    )(oa, za, obg, qc, zc, mk, mv, gt, x, wpa, wpb, wpc, wout, gpost)


def _tail_sample_kernel(oa_ref, za_ref, ob_ref, zb_ref, oc_ref, zc_ref, gt_ref, x_ref,
                        wpa_ref, wpb_ref, wpc_ref, wout_ref, gpost_ref, y_ref):
    a_g = (oa_ref[...] * _silu(za_ref[...].astype(F32))).astype(BF16)
    b_g = (ob_ref[...] * _silu(zb_ref[...].astype(F32))).astype(BF16)
    c_g = (oc_ref[...] * _silu(zc_ref[...].astype(F32))).astype(BF16)
    y_ref[...] = _gated_tail(a_g, b_g, c_g, gt_ref, x_ref[...],
                             wpa_ref, wpb_ref, wpc_ref, wout_ref, gpost_ref)


def _tail_sample(oa, za, ob, zb, oc, zc, gt, x, wpa, wpb, wpc, wout, gpost, tm):
    M = x.shape[0]
    row = lambda i: (i, 0)
    const = lambda i: (0, 0)
    widths = (EA, EA, EB, EB, EC, EC, 3 * D_MODEL, D_MODEL)
    return pl.pallas_call(
        _tail_sample_kernel,
        out_shape=jax.ShapeDtypeStruct((M, D_MODEL), F32),
        grid=(M // tm,),
        in_specs=[pl.BlockSpec((tm, n), row) for n in widths] + [
            pl.BlockSpec((EA, D_MODEL), const), pl.BlockSpec((EB, D_MODEL), const),
            pl.BlockSpec((EC, D_MODEL), const), pl.BlockSpec((D_MODEL, D_MODEL), const),
            pl.BlockSpec((1, D_MODEL), const)],
        out_specs=pl.BlockSpec((tm, D_MODEL), row),
        compiler_params=_params("arbitrary"),
        name="tail_sample",
    )(oa, za, ob, zb, oc, zc, gt, x, wpa, wpb, wpc, wout, gpost)


def _sample_tables(rel_bias, wb, past, T):
    def mult_of(dl, real):
        m = np.zeros(dl.shape, np.float32)
        for (w, d) in PATTERNS:
            m += (real & (dl % d == 0) & (dl <= w)).astype(np.float32)
        return m

    def tables(dl, real):
        row = np.arange(QROWS)[:, None]
        real = real & (dl >= 0) & (past + row - dl >= 0) & (wb + row - dl >= 0)
        m = mult_of(dl, real)
        dlc = np.clip(dl, 0, MAX_DIST)
        m[T:] = m[0]
        bias = jnp.where(jnp.asarray(m > 0)[None], _bias_by_bucket(rel_bias, _t5_bucket_ids(dlc)), NEG)
        return bias, jnp.asarray(m)

    i = np.arange(QROWS)[:, None]
    pos = np.arange(wb)[None, :]
    cache_b, cache_m = tables(wb + i - pos, np.ones((QROWS, wb), bool))
    j = np.arange(QROWS)[None, :]
    new_b, new_m = tables(i - j, np.broadcast_to(j < T, (QROWS, QROWS)))
    return cache_b, cache_m, new_b, new_m


def _sample_attn_kernel(q_ref, kt_ref, vt_ref, kn_ref, vn_ref, cb_ref, cm_ref, nb_ref, nm_ref,
                        qc_ref, mk_ref, mv_ref, xm_ref, oa_ref, oc_ref, *, T):
    zpad = jnp.zeros((QROWS - T, EA), F32)
    q8 = jnp.concatenate([q_ref[0], zpad], axis=0).astype(BF16)
    kn8 = jnp.concatenate([kn_ref[0], zpad], axis=0).astype(BF16)
    vn8 = jnp.concatenate([vn_ref[0], zpad], axis=0).astype(BF16)
    outs = []
    for h in range(HA):
        hs = slice(h * DA, (h + 1) * DA)
        qh = q8[:, hs]
        s = jnp.dot(qh, kt_ref[0, h].astype(BF16), preferred_element_type=F32) + cb_ref[h]
        sn = lax.dot_general(qh, kn8[:, hs], NT_DIMS, preferred_element_type=F32) + nb_ref[h]
        m = jnp.maximum(jnp.max(s, axis=-1, keepdims=True), jnp.max(sn, axis=-1, keepdims=True))
        p = jnp.exp(s - m) * cm_ref[...]
        pn = jnp.exp(sn - m) * nm_ref[...]
        l = jnp.sum(p, axis=-1, keepdims=True) + jnp.sum(pn, axis=-1, keepdims=True)
        o = lax.dot_general(p.astype(BF16), vt_ref[0, h].astype(BF16), NT_DIMS, preferred_element_type=F32)
        o = o + jnp.dot(pn.astype(BF16), vn8[:, hs], preferred_element_type=F32)
        outs.append(o / l)
    oa_ref[0] = jnp.concatenate(outs, axis=-1)[:T]
    cpad = jnp.zeros((QROWS - T, EC), F32)
    qc8 = jnp.concatenate([qc_ref[0], cpad], axis=0)
    qc_heads = jnp.concatenate([qc8[:, h * DC:(h + 1) * DC] for h in range(HC)], axis=0).astype(BF16)
    sc = lax.dot_general(qc_heads, mk_ref[0].astype(BF16), NT_DIMS, preferred_element_type=F32) + xm_ref[...]
    pc = jnp.exp(sc - jnp.max(sc, axis=-1, keepdims=True))
    lc = jnp.sum(pc, axis=-1, keepdims=True)
    oc = jnp.dot(pc.astype(BF16), mv_ref[0].astype(BF16), preferred_element_type=F32) / lc
    oc_ref[0] = jnp.concatenate([oc[h * QROWS:h * QROWS + T] for h in range(HC)], axis=-1)


def _sample_attn(q, k_new, v_new, cache_kt, cache_vt, tables, qc, mem_k, mem_v):
    Bd, T, _ = q.shape
    wb = cache_kt.shape[-1]
    cache_b, cache_m, new_b, new_m = tables
    per_b3 = lambda b: (b, 0, 0)
    per_b4 = lambda b: (b, 0, 0, 0)
    own_head = jnp.asarray(np.where(
        np.arange(HC * QROWS)[:, None] // QROWS == np.arange(N_MEM * HC)[None, :] % HC, 0.0, NEG
    ).astype(np.float32))
    return pl.pallas_call(
        functools.partial(_sample_attn_kernel, T=T),
        out_shape=(jax.ShapeDtypeStruct((Bd, T, EA), F32), jax.ShapeDtypeStruct((Bd, T, EC), F32)),
        grid=(Bd,),
        in_specs=[pl.BlockSpec((1, T, EA), per_b3),
                  pl.BlockSpec((1, HA, DA, wb), per_b4), pl.BlockSpec((1, HA, DA, wb), per_b4),
                  pl.BlockSpec((1, T, EA), per_b3), pl.BlockSpec((1, T, EA), per_b3),
                  pl.BlockSpec((HA, QROWS, wb), lambda b: (0, 0, 0)),
                  pl.BlockSpec((QROWS, wb), lambda b: (0, 0)),
                  pl.BlockSpec((HA, QROWS, QROWS), lambda b: (0, 0, 0)),
                  pl.BlockSpec((QROWS, QROWS), lambda b: (0, 0)),
                  pl.BlockSpec((1, T, EC), per_b3),
                  pl.BlockSpec((1, N_MEM * HC, DC), per_b3), pl.BlockSpec((1, N_MEM * HC, DC), per_b3),
                  pl.BlockSpec((HC * QROWS, N_MEM * HC), lambda b: (0, 0))],
        out_specs=(pl.BlockSpec((1, T, EA), per_b3), pl.BlockSpec((1, T, EC), per_b3)),
        compiler_params=_params("arbitrary"),
        name="sample_attn",
    )(q, cache_kt, cache_vt, k_new, v_new, cache_b, cache_m, new_b, new_m, qc, mem_k, mem_v, own_head)


def _conv_sample_kernel(st_ref, glu_ref, cw_ref, cb_ref, lg_ref, lb_ref, o_ref, *, T, rc):
    n_hist = CONV_W - 1
    Bd = st_ref.shape[1]
    for i in range(T):
        def chunk(c, carry, i=i):
            rows = pl.ds(pl.multiple_of(c * rc, rc), rc)
            acc = jnp.zeros((rc, EB), F32)
            for j in range(CONV_W):
                src = st_ref[i + j, rows, :] if i + j < n_hist else glu_ref[i + j - n_hist, rows, :]
                acc = acc + src * cw_ref[j:j + 1, :]
            o_ref[i, rows, :] = _layernorm_silu(acc + cb_ref[...], lg_ref[...], lb_ref[...])
            return carry
        lax.fori_loop(0, Bd // rc, chunk, 0)


def _conv_sample(state_t, glu_t, cw, cb, lg, lb, rc):
    T, Bd, _ = glu_t.shape
    return pl.pallas_call(
        functools.partial(_conv_sample_kernel, T=T, rc=rc),
        out_shape=jax.ShapeDtypeStruct((T, Bd, EB), F32),
        compiler_params=pltpu.CompilerParams(vmem_limit_bytes=V7X_VMEM_LIMIT),
        name="conv_sample",
    )(state_t, glu_t, cw, cb, lg, lb)


def kernel(x_prompt, x_sample, mem_prompt, cache_k_win, cache_v_win, state_conv, cache_k_mem, cache_v_mem,
           rel_bias, g_pre, w_in, g_mem, w_mem_kv, conv_w, conv_b, ln_g, ln_b, w_proj_a, w_proj_b,
           w_proj_c, w_out, g_post):
    depth = g_pre.shape[0]
    assert depth == 1, "single-layer step"
    B, S, _ = x_prompt.shape
    Bd, T, _ = x_sample.shape
    wb = cache_k_win.shape[2]
    past = wb
    assert wb == MAX_DIST and S % (16 * QB) == 0 and T <= QROWS

    l = 0
    w_in_b = w_in[l].astype(BF16)
    wpa, wpb, wpc = w_proj_a[l].astype(BF16), w_proj_b[l].astype(BF16), w_proj_c[l].astype(BF16)
    wout = w_out[l].astype(BF16)
    gpre, gpost = g_pre[l][None], g_post[l][None]
    cb, lg, lb = conv_b[l][None], ln_g[l][None], ln_b[l][None]
    cw = conv_w[l]

    (qkv, k_keep, v_keep, za, glu, zb, qc, zc, gt, glu_tail) = _inproj_prompt(x_prompt, gpre, w_in_b, tm=256)
    mk_f, mv_f, mk_b, mv_b = _memkv(mem_prompt.reshape(B * N_MEM, D_MODEL), g_mem[l][None],
                                    w_mem_kv[l].astype(BF16), tm=512)
    oa = _band_attn(qkv, _band_tables(rel_bias), unroll=8)
    cw_tiles = jnp.broadcast_to(cw[:, None, :], (CONV_W, 8, EB))
    obg = _conv_prompt(glu, zb, cw_tiles, cb, lg, lb, tm=512, rc=32, rn=64)
    y_p = _tail_prompt(oa, za, obg, qc, zc, mk_b.reshape(B, N_MEM, EC), mv_b.reshape(B, N_MEM, EC),
                       gt, x_prompt, wpa, wpb, wpc, wout, gpost, tm=256)

    (q_s, k_s, v_s, za_s, glu_s, zb_s, qc_s, zc_s, gt_s) = _inproj_sample(
        x_sample.reshape(Bd * T, D_MODEL), gpre, w_in_b, tm=Bd * T)
    cache_kt = jnp.transpose(cache_k_win[l], (0, 2, 3, 1))
    cache_vt = jnp.transpose(cache_v_win[l], (0, 2, 3, 1))
    oa_s, oc_s = _sample_attn(q_s.reshape(Bd, T, EA), k_s.reshape(Bd, T, EA), v_s.reshape(Bd, T, EA),
                              cache_kt, cache_vt, _sample_tables(rel_bias, wb, past, T),
                              qc_s.reshape(Bd, T, EC), cache_k_mem[l].reshape(Bd, N_MEM * HC, DC),
                              cache_v_mem[l].reshape(Bd, N_MEM * HC, DC))
    state_t = jnp.transpose(state_conv[l], (1, 0, 2))
    glu_t = jnp.transpose(glu_s.reshape(Bd, T, EB), (1, 0, 2))
    ob_t = _conv_sample(state_t, glu_t, cw, cb, lg, lb, rc=32)
    ob_s = jnp.transpose(ob_t, (1, 0, 2)).reshape(Bd * T, EB)
    y_s = _tail_sample(oa_s.reshape(Bd * T, EA), za_s, ob_s, zb_s,
                       oc_s.reshape(Bd * T, EC), zc_s, gt_s, x_sample.reshape(Bd * T, D_MODEL),
                       wpa, wpb, wpc, wout, gpost, tm=Bd * T)

    conv_state_s = jnp.transpose(jnp.concatenate([state_t[T:], glu_t], axis=0), (1, 0, 2))
    return (y_p, y_s.reshape(Bd, T, D_MODEL),
            k_keep.reshape(1, B, wb, HA, DA), v_keep.reshape(1, B, wb, HA, DA),
            glu_tail[:, HALO - (CONV_W - 1):][None],
            mk_f.reshape(1, B, N_MEM, HC, DC), mv_f.reshape(1, B, N_MEM, HC, DC),
            k_s.reshape(1, Bd, T, HA, DA), v_s.reshape(1, Bd, T, HA, DA),
            conv_state_s[None])
```

```python
import functools

import jax
import jax.numpy as jnp
import numpy as np
from jax import lax
from jax.experimental import pallas as pl
from jax.experimental.pallas import tpu as pltpu

F32 = jnp.float32
BF16 = jnp.bfloat16

D_MODEL = 1024
HA, DA = 12, 64
EA = HA * DA
NHP = HA // 2
PATTERNS = ((128, 1), (512, 4), (2048, 16))
NW = 128
QB = 128
EB = 768
CONV_W = 31
HC, DC = 4, 128
EC = HC * DC
N_MEM = 256
N_BUCKETS = 32
MAX_DIST = 2048
EPS = 1e-6
NEG = -1e30
LOG2E = 1.4426950408889634

C_Q, C_K, C_V, C_ZA = 0, EA, 2 * EA, 3 * EA
C_U = 4 * EA
C_G = C_U + EB
C_ZB = C_U + 2 * EB
C_QC = C_ZB + EB
C_ZC = C_QC + EC
C_GT = C_ZC + EC
IN_COLS = C_GT + 3 * D_MODEL

V7X_VMEM_LIMIT = 56 * 1024 * 1024
HALO = 32
QROWS = 8
NT_DIMS = (((1,), (1,)), ((), ()))


def _t5_bucket_ids(n):
    exact = N_BUCKETS // 2
    nf = np.maximum(n, 1).astype(np.float32)
    scale = np.float32(N_BUCKETS - exact) / np.log(np.float32(MAX_DIST) / np.float32(exact))
    large = exact + (np.log(nf / np.float32(exact)) * scale).astype(np.int32)
    large = np.minimum(large, N_BUCKETS - 1)
    return np.where(n < exact, n, large).astype(np.int32)


def _bias_by_bucket(rel_bias, ids):
    onehot = (ids[None] == np.arange(N_BUCKETS).reshape((-1,) + (1,) * ids.ndim)).astype(np.float32)
    return jnp.tensordot(rel_bias.astype(F32).T, jnp.asarray(onehot), axes=1,
                         precision=lax.Precision.HIGHEST)


def _sigmoid(x):
    return 1.0 / (1.0 + jnp.exp(-x))


def _silu(x):
    return x * _sigmoid(x)


def _rmsnorm_f32(x, g):
    return x * lax.rsqrt(jnp.mean(x * x, axis=-1, keepdims=True) + EPS) * g


def _params(*sem):
    return pltpu.CompilerParams(dimension_semantics=sem, vmem_limit_bytes=V7X_VMEM_LIMIT)


def _inproj_prompt_kernel(x_ref, g_ref, w_ref, qkv_ref, kf_ref, vf_ref, za_ref, glu_ref, zb_ref,
                          qc_ref, zc_ref, gt_ref, tail_ref):
    xn = _rmsnorm_f32(x_ref[0], g_ref[...]).astype(BF16)

    def mm(lo, n):
        return jnp.dot(xn, w_ref[:, lo:lo + n], preferred_element_type=F32)

    def put_pairs(res, base):
        for c in range(NHP):
            qkv_ref[0, base + c] = res[:, c * 128:(c + 1) * 128].astype(BF16)

    put_pairs(mm(C_Q, EA) * (DA ** -0.5 * LOG2E), 0)
    k = mm(C_K, EA)
    put_pairs(k, NHP)
    kf_ref[0] = k
    v = mm(C_V, EA)
    put_pairs(v, 2 * NHP)
    vf_ref[0] = v
    za_ref[0] = mm(C_ZA, EA).astype(BF16)
    glu = mm(C_U, EB) * _sigmoid(mm(C_G, EB))
    glu_ref[0] = glu.astype(BF16)
    tail_ref[0] = glu[glu.shape[0] - HALO:, :]
    zb_ref[0] = mm(C_ZB, EB).astype(BF16)
    qc_ref[0] = (mm(C_QC, EC) * (DC ** -0.5)).astype(BF16)
    zc_ref[0] = mm(C_ZC, EC).astype(BF16)
    for c in range(3):
        gt_ref[0, :, c * D_MODEL:(c + 1) * D_MODEL] = mm(C_GT + c * D_MODEL, D_MODEL).astype(BF16)


def _inproj_prompt(x, g, w, tm):
    B, S, _ = x.shape
    nt = S // tm
    wb = min(MAX_DIST, S)
    first_kept = (S - wb) // tm
    row = lambda b, i: (b, i, 0)
    kept = lambda b, i: (b, jnp.maximum(i - first_kept, 0), 0)
    out_shape = (
        jax.ShapeDtypeStruct((B, 3 * NHP, S, 128), BF16),
        jax.ShapeDtypeStruct((B, wb, EA), F32),
        jax.ShapeDtypeStruct((B, wb, EA), F32),
        jax.ShapeDtypeStruct((B, S, EA), BF16),
        jax.ShapeDtypeStruct((B, S, EB), BF16),
        jax.ShapeDtypeStruct((B, S, EB), BF16),
        jax.ShapeDtypeStruct((B, S, EC), BF16),
        jax.ShapeDtypeStruct((B, S, EC), BF16),
        jax.ShapeDtypeStruct((B, S, 3 * D_MODEL), BF16),
        jax.ShapeDtypeStruct((B, HALO, EB), F32),
    )
    out_specs = (
        pl.BlockSpec((1, 3 * NHP, tm, 128), lambda b, i: (b, 0, i, 0)),
        pl.BlockSpec((1, tm, EA), kept),
        pl.BlockSpec((1, tm, EA), kept),
        pl.BlockSpec((1, tm, EA), row),
        pl.BlockSpec((1, tm, EB), row),
        pl.BlockSpec((1, tm, EB), row),
        pl.BlockSpec((1, tm, EC), row),
        pl.BlockSpec((1, tm, EC), row),
        pl.BlockSpec((1, tm, 3 * D_MODEL), row),
        pl.BlockSpec((1, HALO, EB), lambda b, i: (b, 0, 0)),
    )
    return pl.pallas_call(
        _inproj_prompt_kernel,
        out_shape=out_shape,
        grid=(B, nt),
        in_specs=[
            pl.BlockSpec((1, tm, D_MODEL), row),
            pl.BlockSpec((1, D_MODEL), lambda b, i: (0, 0)),
            pl.BlockSpec((D_MODEL, IN_COLS), lambda b, i: (0, 0), pipeline_mode=pl.Buffered(1)),
        ],
        out_specs=out_specs,
        compiler_params=_params("arbitrary", "arbitrary"),
        name="inproj_prompt",
    )(x, g, w)


def _inproj_sample_kernel(x_ref, g_ref, w_ref, q_ref, k_ref, v_ref, za_ref, glu_ref, zb_ref,
                          qc_ref, zc_ref, gt_ref):
    xn = _rmsnorm_f32(x_ref[...], g_ref[...]).astype(BF16)

    def mm(lo, n):
        return jnp.dot(xn, w_ref[:, lo:lo + n], preferred_element_type=F32)

    q_ref[...] = mm(C_Q, EA) * (DA ** -0.5)
    k_ref[...] = mm(C_K, EA)
    v_ref[...] = mm(C_V, EA)
    za_ref[...] = mm(C_ZA, EA).astype(BF16)
    glu_ref[...] = mm(C_U, EB) * _sigmoid(mm(C_G, EB))
    zb_ref[...] = mm(C_ZB, EB).astype(BF16)
    qc_ref[...] = mm(C_QC, EC) * (DC ** -0.5)
    zc_ref[...] = mm(C_ZC, EC).astype(BF16)
    for c in range(3):
        gt_ref[:, c * D_MODEL:(c + 1) * D_MODEL] = mm(C_GT + c * D_MODEL, D_MODEL).astype(BF16)


def _inproj_sample(x, g, w, tm):
    M = x.shape[0]
    row = lambda i: (i, 0)
    widths = (EA, EA, EA, EA, EB, EB, EC, EC, 3 * D_MODEL)
    dtypes = (F32, F32, F32, BF16, F32, BF16, F32, BF16, BF16)
    return pl.pallas_call(
        _inproj_sample_kernel,
        out_shape=tuple(jax.ShapeDtypeStruct((M, n), dt) for n, dt in zip(widths, dtypes)),
        grid=(M // tm,),
        in_specs=[
            pl.BlockSpec((tm, D_MODEL), row),
            pl.BlockSpec((1, D_MODEL), lambda i: (0, 0)),
            pl.BlockSpec((D_MODEL, IN_COLS), lambda i: (0, 0), pipeline_mode=pl.Buffered(1)),
        ],
        out_specs=tuple(pl.BlockSpec((tm, n), row) for n in widths),
        compiler_params=_params("arbitrary"),
        name="inproj_sample",
    )(x, g, w)


def _memkv_kernel(m_ref, g_ref, w_ref, kf_ref, vf_ref, kb_ref, vb_ref):
    xn = _rmsnorm_f32(m_ref[...], g_ref[...]).astype(BF16)
    k = jnp.dot(xn, w_ref[:, :EC], preferred_element_type=F32)
    v = jnp.dot(xn, w_ref[:, EC:], preferred_element_type=F32)
    kf_ref[...] = k
    vf_ref[...] = v
    kb_ref[...] = k.astype(BF16)
    vb_ref[...] = v.astype(BF16)


def _memkv(mem, g, w, tm):
    M = mem.shape[0]
    row = lambda i: (i, 0)
    return pl.pallas_call(
        _memkv_kernel,
        out_shape=(jax.ShapeDtypeStruct((M, EC), F32), jax.ShapeDtypeStruct((M, EC), F32),
                   jax.ShapeDtypeStruct((M, EC), BF16), jax.ShapeDtypeStruct((M, EC), BF16)),
        grid=(M // tm,),
        in_specs=[pl.BlockSpec((tm, D_MODEL), row),
                  pl.BlockSpec((1, D_MODEL), lambda i: (0, 0)),
                  pl.BlockSpec((D_MODEL, 2 * EC), lambda i: (0, 0))],
        out_specs=tuple(pl.BlockSpec((tm, EC), row) for _ in range(4)),
        compiler_params=_params("arbitrary"),
        name="memkv",
    )(mem, g, w)


def _band_tables(rel_bias):
    iq = np.arange(QB)[:, None]
    ik = np.arange(2 * QB)[None, :]
    dist = iq - ik + QB
    band = (dist >= 0) & (dist <= NW)
    out = []
    for (_, d) in PATTERNS:
        bias = LOG2E * _bias_by_bucket(rel_bias, _t5_bucket_ids(np.clip(dist, 0, NW) * d))
        later = jnp.where(jnp.asarray(band)[None], bias, NEG)
        first = jnp.where(jnp.asarray(band & (ik >= QB))[None], bias, NEG)
        t = jnp.stack([first, later], axis=1)
        out.append(t.reshape(NHP, 2, 2, QB, 2 * QB).transpose(0, 2, 1, 3, 4)
                   .reshape(NHP, 2, 2 * QB, 2 * QB))
    return jnp.stack(out)


def _band_attn_kernel(q_ref, k_ref, v_ref, t_ref, oa_ref,
                      nat32, p4_32, p4_ref, p16_ref, a4_ref, l4_ref, m4_ref, a16_ref, l16_ref, m16_ref,
                      *, S, unroll):
    n_units = S // QB
    lane = lax.broadcasted_iota(jnp.int32, (QB, 128), 1)
    head0 = lane < DA
    keep0 = head0.astype(F32).astype(BF16)
    keep1 = (1.0 - head0.astype(F32)).astype(BF16)
    ones = jnp.ones((2 * QB, 128), BF16)
    n4 = S // 4

    for t, src in enumerate((q_ref, k_ref, v_ref)):
        def widen(c, carry, src=src):
            r0 = pl.multiple_of(c * 256, 256)
            nat32[pl.ds(r0, 256), :] = src[0, 0, pl.ds(r0, 256), :].astype(F32)
            return carry
        lax.fori_loop(0, S // 256, widen, 0)
        for r in range(4):
            def by4(c, carry, r=r, t=t):
                u0 = pl.multiple_of(c * 256, 256)
                x = nat32[pl.ds(4 * u0 + r, 256, stride=4), :]
                p4_32[pl.ds(r * n4 + u0, 256), :] = x
                p4_ref[t, pl.ds(r * n4 + u0, 256), :] = x.astype(BF16)
                return carry
            lax.fori_loop(0, n4 // 256, by4, 0)
        for r4 in range(4):
            for s in range(4):
                x = p4_32[pl.ds(r4 * n4 + s, n4 // 4, stride=4), :]
                p16_ref[t, pl.ds((4 * s + r4) * (n4 // 4), n4 // 4), :] = x.astype(BF16)

    def unit(u, pat, d, qsrc, ksrc, vsrc):
        nqb = n_units // d
        row = pl.multiple_of(u * QB, QB)
        prow = pl.multiple_of(jnp.maximum(u - 1, 0) * QB, QB)
        q2 = qsrc[pl.ds(row, QB), :]
        kk = jnp.concatenate([ksrc[pl.ds(prow, QB), :], ksrc[pl.ds(row, QB), :]], axis=0)
        vv = jnp.concatenate([vsrc[pl.ds(prow, QB), :], vsrc[pl.ds(row, QB), :]], axis=0)
        qs = jnp.concatenate([q2 * keep0, q2 * keep1], axis=0)
        s = lax.dot_general(qs, kk, NT_DIMS, preferred_element_type=F32)
        s = s + t_ref[pat, 0, jnp.minimum(u & (nqb - 1), 1)]
        m = jnp.max(s, axis=-1, keepdims=True)
        p = jnp.exp2(s - m).astype(BF16)
        oe = jnp.dot(p, jnp.concatenate([vv, ones], axis=1), preferred_element_type=F32)
        mb = jnp.broadcast_to(m, (2 * QB, 128))
        return (jnp.where(head0, oe[:QB, :128], oe[QB:, :128]),
                jnp.where(head0, oe[:QB, 128:], oe[QB:, 128:]),
                jnp.where(head0, mb[:QB], mb[QB:]))

    def strided_pattern(pat, d, src, acc_ref, den_ref, max_ref):
        nqb = n_units // d

        def body(u, carry):
            acc, den, mx = unit(u, pat, d, src.at[0], src.at[1], src.at[2])
            start = u // nqb + (u & (nqb - 1)) * (d * QB)
            acc_ref[pl.ds(start, QB, stride=d), :] = acc
            den_ref[pl.ds(start, QB, stride=d), :] = den
            max_ref[pl.ds(start, QB, stride=d), :] = mx
            return carry
        lax.fori_loop(0, n_units, body, 0, unroll=unroll)

    strided_pattern(2, 16, p16_ref, a16_ref, l16_ref, m16_ref)
    strided_pattern(1, 4, p4_ref, a4_ref, l4_ref, m4_ref)

    def dense(u, carry):
        a1, l1, m1 = unit(u, 0, 1, q_ref.at[0, 0], k_ref.at[0, 0], v_ref.at[0, 0])
        rows = pl.ds(pl.multiple_of(u * QB, QB), QB)
        m4, m16 = m4_ref[rows, :], m16_ref[rows, :]
        mx = jnp.maximum(jnp.maximum(m1, m4), m16)
        e1, e4, e16 = jnp.exp2(m1 - mx), jnp.exp2(m4 - mx), jnp.exp2(m16 - mx)
        num = a1 * e1 + a4_ref[rows, :] * e4 + a16_ref[rows, :] * e16
        den = l1 * e1 + l4_ref[rows, :] * e4 + l16_ref[rows, :] * e16
        oa_ref[0, 0, rows, :] = (num / den).astype(BF16)
        return carry
    lax.fori_loop(0, n_units, dense, 0, unroll=unroll)


def _band_attn(qkv, tables, unroll):
    B, _, S, _ = qkv.shape
    blk = (1, 1, S, 128)
    return pl.pallas_call(
        functools.partial(_band_attn_kernel, S=S, unroll=unroll),
        out_shape=jax.ShapeDtypeStruct((B, NHP, S, 128), BF16),
        grid=(B, NHP),
        in_specs=[pl.BlockSpec(blk, lambda b, h: (b, h, 0, 0)),
                  pl.BlockSpec(blk, lambda b, h: (b, NHP + h, 0, 0)),
                  pl.BlockSpec(blk, lambda b, h: (b, 2 * NHP + h, 0, 0)),
                  pl.BlockSpec((3, 1, 2, 2 * QB, 2 * QB), lambda b, h: (0, h, 0, 0, 0))],
        out_specs=pl.BlockSpec(blk, lambda b, h: (b, h, 0, 0)),
        scratch_shapes=[pltpu.VMEM((S, 128), F32), pltpu.VMEM((S, 128), F32),
                        pltpu.VMEM((3, S, 128), BF16), pltpu.VMEM((3, S, 128), BF16)]
                       + [pltpu.VMEM((S, 128), F32)] * 6,
        compiler_params=_params("arbitrary", "arbitrary"),
        name="band_attn",
    )(qkv, qkv, qkv, tables)


def _layernorm_silu(c, g, b):
    mu = jnp.mean(c, axis=-1, keepdims=True)
    cc = c - mu
    var = jnp.mean(cc * cc, axis=-1, keepdims=True)
    return _silu(cc * lax.rsqrt(var + EPS) * g + b)


FIRST_TAP = HALO - (CONV_W - 1)


def _conv_stage(glu_ref, halo_ref, seq_start, full_ref, tm):
    halo = halo_ref[0].astype(F32)
    full_ref[0, 0:HALO, :] = jnp.where(seq_start, jnp.zeros_like(halo), halo)
    full_ref[0, HALO:HALO + tm, :] = glu_ref[0].astype(F32)
    n_shift = HALO + tm - 8
    for s in range(1, 8):
        full_ref[s, 0:n_shift, :] = full_ref[0, s:s + n_shift, :]


def _conv_taps(c, full_ref, cw_ref, conv_ref, rc):
    base = pl.multiple_of(c * rc, rc)
    accs = [jnp.zeros((8, EB), F32) for _ in range(rc // 8)]
    for j in range(CONV_W):
        off = FIRST_TAP + j
        w8 = cw_ref[j]
        for g in range(rc // 8):
            accs[g] = accs[g] + full_ref[off % 8, pl.ds(base + (off // 8) * 8 + 8 * g, 8), :] * w8
    conv_ref[pl.ds(base, rc), :] = jnp.concatenate(accs, axis=0)


def _conv_norm_gate(c, conv_ref, zb_ref, cb_ref, lg_ref, lb_ref, o_ref, rn):
    rows = pl.ds(pl.multiple_of(c * rn, rn), rn)
    ob = _layernorm_silu(conv_ref[rows, :] + cb_ref[...], lg_ref[...], lb_ref[...])
    o_ref[0, rows, :] = (ob * _silu(zb_ref[0, rows, :].astype(F32))).astype(BF16)


def _gated_tail(a_g, b_g, c_g, gt_ref, x, wpa_ref, wpb_ref, wpc_ref, wout_ref, gpost_ref):
    def gate(k):
        return _sigmoid(gt_ref[:, k * D_MODEL:(k + 1) * D_MODEL].astype(F32))

    mix = gate(0) * jnp.dot(a_g, wpa_ref[...], preferred_element_type=F32)
    mix = mix + gate(1) * jnp.dot(b_g, wpb_ref[...], preferred_element_type=F32)
    mix = mix + gate(2) * jnp.dot(c_g, wpc_ref[...], preferred_element_type=F32)
    z = jnp.dot(mix.astype(BF16), wout_ref[...], preferred_element_type=F32)
    return x + _rmsnorm_f32(z, gpost_ref[...])


def _tail_prompt_kernel(oa_ref, za_ref, obg_ref, qc_ref, zc_ref, mk_ref, mv_ref, gt_ref, x_ref,
                        wpa_ref, wpb_ref, wpc_ref, wout_ref, gpost_ref, y_ref):
    a_g = jnp.concatenate(
        [(oa_ref[0, c].astype(F32) * _silu(za_ref[0, :, c * 128:(c + 1) * 128].astype(F32))).astype(BF16)
         for c in range(NHP)], axis=-1)
    pieces = []
    for h in range(HC):
        hs = slice(h * DC, (h + 1) * DC)
        s = lax.dot_general(qc_ref[0, :, hs], mk_ref[0, :, hs], NT_DIMS, preferred_element_type=F32)
        p = jnp.exp(s - jnp.max(s, axis=-1, keepdims=True))
        l = jnp.sum(p, axis=-1, keepdims=True)
        oc = jnp.dot(p.astype(BF16), mv_ref[0, :, hs], preferred_element_type=F32) / l
        pieces.append((oc * _silu(zc_ref[0, :, hs].astype(F32))).astype(BF16))
    c_g = jnp.concatenate(pieces, axis=-1)
    y_ref[0] = _gated_tail(a_g, obg_ref[0], c_g, gt_ref.at[0], x_ref[0],
                           wpa_ref, wpb_ref, wpc_ref, wout_ref, gpost_ref)


def _tail_prompt(oa, za, obg, qc, zc, mk, mv, gt, x, wpa, wpb, wpc, wout, gpost, tm):
    B, S, _ = x.shape
    row = lambda b, i: (b, i, 0)
    const = lambda b, i: (0, 0)
    return pl.pallas_call(
        _tail_prompt_kernel,
        out_shape=jax.ShapeDtypeStruct((B, S, D_MODEL), F32),
        grid=(B, S // tm),
        in_specs=[
            pl.BlockSpec((1, NHP, tm, 128), lambda b, i: (b, 0, i, 0)),
            pl.BlockSpec((1, tm, EA), row), pl.BlockSpec((1, tm, EB), row),
            pl.BlockSpec((1, tm, EC), row), pl.BlockSpec((1, tm, EC), row),
            pl.BlockSpec((1, N_MEM, EC), lambda b, i: (b, 0, 0)),
            pl.BlockSpec((1, N_MEM, EC), lambda b, i: (b, 0, 0)),
            pl.BlockSpec((1, tm, 3 * D_MODEL), row), pl.BlockSpec((1, tm, D_MODEL), row),
            pl.BlockSpec((EA, D_MODEL), const), pl.BlockSpec((EB, D_MODEL), const),
            pl.BlockSpec((EC, D_MODEL), const), pl.BlockSpec((D_MODEL, D_MODEL), const),
            pl.BlockSpec((1, D_MODEL), const)],
        out_specs=pl.BlockSpec((1, tm, D_MODEL), row),
        compiler_params=_params("arbitrary", "arbitrary"),
        name="tail_prompt",
    )(oa, za, obg, qc, zc, mk, mv, gt, x, wpa, wpb, wpc, wout, gpost)


def _tail_sample_kernel(oa_ref, za_ref, ob_ref, zb_ref, oc_ref, zc_ref, gt_ref, x_ref,
                        wpa_ref, wpb_ref, wpc_ref, wout_ref, gpost_ref, y_ref):
    a_g = (oa_ref[...] * _silu(za_ref[...].astype(F32))).astype(BF16)
    b_g = (ob_ref[...] * _silu(zb_ref[...].astype(F32))).astype(BF16)
    c_g = (oc_ref[...] * _silu(zc_ref[...].astype(F32))).astype(BF16)
    y_ref[...] = _gated_tail(a_g, b_g, c_g, gt_ref, x_ref[...],
                             wpa_ref, wpb_ref, wpc_ref, wout_ref, gpost_ref)


def _tail_sample(oa, za, ob, zb, oc, zc, gt, x, wpa, wpb, wpc, wout, gpost, tm):
    M = x.shape[0]
    row = lambda i: (i, 0)
    const = lambda i: (0, 0)
    widths = (EA, EA, EB, EB, EC, EC, 3 * D_MODEL, D_MODEL)
    return pl.pallas_call(
        _tail_sample_kernel,
        out_shape=jax.ShapeDtypeStruct((M, D_MODEL), F32),
        grid=(M // tm,),
        in_specs=[pl.BlockSpec((tm, n), row) for n in widths] + [
            pl.BlockSpec((EA, D_MODEL), const), pl.BlockSpec((EB, D_MODEL), const),
            pl.BlockSpec((EC, D_MODEL), const), pl.BlockSpec((D_MODEL, D_MODEL), const),
            pl.BlockSpec((1, D_MODEL), const)],
        out_specs=pl.BlockSpec((tm, D_MODEL), row),
        compiler_params=_params("arbitrary"),
        name="tail_sample",
    )(oa, za, ob, zb, oc, zc, gt, x, wpa, wpb, wpc, wout, gpost)


def _sample_tables(rel_bias, wb, past, T):
    def mult_of(dl, real):
        m = np.zeros(dl.shape, np.float32)
        for (w, d) in PATTERNS:
            m += (real & (dl % d == 0) & (dl <= w)).astype(np.float32)
        return m

    def tables(dl, real):
        row = np.arange(QROWS)[:, None]
        real = real & (dl >= 0) & (past + row - dl >= 0) & (wb + row - dl >= 0)
        m = mult_of(dl, real)
        dlc = np.clip(dl, 0, MAX_DIST)
        m[T:] = m[0]
        bias = jnp.where(jnp.asarray(m > 0)[None], _bias_by_bucket(rel_bias, _t5_bucket_ids(dlc)), NEG)
        return bias, jnp.asarray(m)

    i = np.arange(QROWS)[:, None]
    pos = np.arange(wb)[None, :]
    cache_b, cache_m = tables(wb + i - pos, np.ones((QROWS, wb), bool))
    j = np.arange(QROWS)[None, :]
    new_b, new_m = tables(i - j, np.broadcast_to(j < T, (QROWS, QROWS)))
    return cache_b, cache_m, new_b, new_m


def _pad_rows(x, rows):
    return jnp.concatenate([x, jnp.zeros((rows - x.shape[0], x.shape[1]), x.dtype)], axis=0)


def _sample_heads(heads, qh_ref, knh_ref, vnh_ref, kt_ref, vt_ref, cb_ref, cm_ref, nb_ref, nm_ref, oh_ref):
    scores = []
    for h in heads:
        qh = qh_ref[h].astype(BF16)
        s = jnp.dot(qh, kt_ref[0, h].astype(BF16), preferred_element_type=F32) + cb_ref[h]
        sn = lax.dot_general(qh, knh_ref[h].astype(BF16), NT_DIMS, preferred_element_type=F32) + nb_ref[h]
        scores.append((s, sn))
    probs = []
    for s, sn in scores:
        m = jnp.maximum(jnp.max(s, axis=-1, keepdims=True), jnp.max(sn, axis=-1, keepdims=True))
        p = jnp.exp(s - m) * cm_ref[...]
        pn = jnp.exp(sn - m) * nm_ref[...]
        l = jnp.sum(p, axis=-1, keepdims=True) + jnp.sum(pn, axis=-1, keepdims=True)
        probs.append((p.astype(BF16), pn.astype(BF16), l))
    for h, (p, pn, l) in zip(heads, probs):
        o = lax.dot_general(p, vt_ref[0, h].astype(BF16), NT_DIMS, preferred_element_type=F32)
        o = o + jnp.dot(pn, vnh_ref[h].astype(BF16), preferred_element_type=F32)
        oh_ref[h] = o / l


def _sample_cross_attn(qc_ref, mk_ref, mv_ref, xm_ref, oc_ref, T):
    qc8 = _pad_rows(qc_ref[0], QROWS)
    qc_heads = jnp.concatenate([qc8[:, h * DC:(h + 1) * DC] for h in range(HC)], axis=0).astype(BF16)
    sc = lax.dot_general(qc_heads, mk_ref[0].astype(BF16), NT_DIMS, preferred_element_type=F32) + xm_ref[...]
    pc = jnp.exp(sc - jnp.max(sc, axis=-1, keepdims=True))
    lc = jnp.sum(pc, axis=-1, keepdims=True)
    oc = jnp.dot(pc.astype(BF16), mv_ref[0].astype(BF16), preferred_element_type=F32) / lc
    oc_ref[0] = jnp.concatenate([oc[h * QROWS:h * QROWS + T] for h in range(HC)], axis=-1)


def _conv_attn_kernel(glu_ref, halo_ref, zb_ref, cw_ref, cb_ref, lg_ref, lb_ref,
                      q_ref, kt_ref, vt_ref, kn_ref, vn_ref, tb_ref, tm_ref, nb_ref, nm_ref,
                      qc_ref, mk_ref, mv_ref, xm_ref,
                      obg_ref, oa_ref, oc_ref,
                      full_ref, conv_ref, qh_ref, knh_ref, vnh_ref, oh_ref,
                      *, tm, rc, rn, T, tiles_per_seq):
    g = pl.program_id(0)
    _conv_stage(glu_ref, halo_ref, g % tiles_per_seq == 0, full_ref, tm)
    q8, kn8, vn8 = _pad_rows(q_ref[0], QROWS), _pad_rows(kn_ref[0], QROWS), _pad_rows(vn_ref[0], QROWS)
    for h in range(HA):
        hs = slice(h * DA, (h + 1) * DA)
        qh_ref[h], knh_ref[h], vnh_ref[h] = q8[:, hs], kn8[:, hs], vn8[:, hs]
    _sample_cross_attn(qc_ref, mk_ref, mv_ref, xm_ref, oc_ref, T)
    n_chunks = tm // rc
    heads_per_chunk = HA // n_chunks

    def body(c, carry):
        _conv_taps(c, full_ref, cw_ref, conv_ref, rc)
        _sample_heads([c * heads_per_chunk + k for k in range(heads_per_chunk)],
                      qh_ref, knh_ref, vnh_ref, kt_ref, vt_ref, tb_ref, tm_ref, nb_ref, nm_ref, oh_ref)
        return carry
    lax.fori_loop(0, n_chunks, body, 0)

    for c in range(tm // rn):
        _conv_norm_gate(c, conv_ref, zb_ref, cb_ref, lg_ref, lb_ref, obg_ref, rn)
    oa_ref[0] = jnp.concatenate([oh_ref[h][:T] for h in range(HA)], axis=-1)


def _conv_prompt_sample_attn(glu, zb, cw, cb, lg, lb, q, k_new, v_new, cache_kt, cache_vt, tables,
                             qc, mem_k, mem_v, tm, rc, rn):
    B, S, _ = glu.shape
    Bd, T, _ = q.shape
    wb = cache_kt.shape[-1]
    tiles_per_seq = S // tm
    assert B * tiles_per_seq == Bd and HA % (tm // rc) == 0
    cache_b, cache_m, new_b, new_m = tables
    own_head = jnp.asarray(np.where(
        np.arange(HC * QROWS)[:, None] // QROWS == np.arange(N_MEM * HC)[None, :] % HC, 0.0, NEG
    ).astype(np.float32))
    tile = lambda g: (g // tiles_per_seq, g % tiles_per_seq, 0)
    halo = lambda g: (g // tiles_per_seq, jnp.maximum((g % tiles_per_seq) * (tm // HALO) - 1, 0), 0)
    per_b3 = lambda g: (g, 0, 0)
    per_b4 = lambda g: (g, 0, 0, 0)
    c2 = lambda g: (0, 0)
    c3 = lambda g: (0, 0, 0)
    head_scratch = pltpu.VMEM((HA, QROWS, DA), F32)
    return pl.pallas_call(
        functools.partial(_conv_attn_kernel, tm=tm, rc=rc, rn=rn, T=T, tiles_per_seq=tiles_per_seq),
        out_shape=(jax.ShapeDtypeStruct((B, S, EB), BF16),
                   jax.ShapeDtypeStruct((Bd, T, EA), F32), jax.ShapeDtypeStruct((Bd, T, EC), F32)),
        grid=(Bd,),
        in_specs=[pl.BlockSpec((1, tm, EB), tile), pl.BlockSpec((1, HALO, EB), halo),
                  pl.BlockSpec((1, tm, EB), tile),
                  pl.BlockSpec((CONV_W, 8, EB), c3),
                  pl.BlockSpec((1, EB), c2), pl.BlockSpec((1, EB), c2), pl.BlockSpec((1, EB), c2),
                  pl.BlockSpec((1, T, EA), per_b3),
                  pl.BlockSpec((1, HA, DA, wb), per_b4), pl.BlockSpec((1, HA, DA, wb), per_b4),
                  pl.BlockSpec((1, T, EA), per_b3), pl.BlockSpec((1, T, EA), per_b3),
                  pl.BlockSpec((HA, QROWS, wb), c3), pl.BlockSpec((QROWS, wb), c2),
                  pl.BlockSpec((HA, QROWS, QROWS), c3), pl.BlockSpec((QROWS, QROWS), c2),
                  pl.BlockSpec((1, T, EC), per_b3),
                  pl.BlockSpec((1, N_MEM * HC, DC), per_b3), pl.BlockSpec((1, N_MEM * HC, DC), per_b3),
                  pl.BlockSpec((HC * QROWS, N_MEM * HC), c2)],
        out_specs=(pl.BlockSpec((1, tm, EB), tile),
                   pl.BlockSpec((1, T, EA), per_b3), pl.BlockSpec((1, T, EC), per_b3)),
        scratch_shapes=[pltpu.VMEM((8, HALO + tm, EB), F32), pltpu.VMEM((tm, EB), F32),
                        head_scratch, head_scratch, head_scratch, head_scratch],
        compiler_params=_params("arbitrary"),
        name="conv_prompt_sample_attn",
    )(glu, glu, zb, cw, cb, lg, lb, q, cache_kt, cache_vt, k_new, v_new,
      cache_b, cache_m, new_b, new_m, qc, mem_k, mem_v, own_head)


def _conv_sample_kernel(st_ref, glu_ref, cw_ref, cb_ref, lg_ref, lb_ref, o_ref, *, T, rc):
    n_hist = CONV_W - 1
    Bd = st_ref.shape[1]
    for i in range(T):
        def chunk(c, carry, i=i):
            rows = pl.ds(pl.multiple_of(c * rc, rc), rc)
            acc = jnp.zeros((rc, EB), F32)
            for j in range(CONV_W):
                src = st_ref[i + j, rows, :] if i + j < n_hist else glu_ref[i + j - n_hist, rows, :]
                acc = acc + src * cw_ref[j:j + 1, :]
            o_ref[i, rows, :] = _layernorm_silu(acc + cb_ref[...], lg_ref[...], lb_ref[...])
            return carry
        lax.fori_loop(0, Bd // rc, chunk, 0)


def _conv_sample(state_t, glu_t, cw, cb, lg, lb, rc):
    T, Bd, _ = glu_t.shape
    return pl.pallas_call(
        functools.partial(_conv_sample_kernel, T=T, rc=rc),
        out_shape=jax.ShapeDtypeStruct((T, Bd, EB), F32),
        compiler_params=pltpu.CompilerParams(vmem_limit_bytes=V7X_VMEM_LIMIT),
        name="conv_sample",
    )(state_t, glu_t, cw, cb, lg, lb)


def kernel(x_prompt, x_sample, mem_prompt, cache_k_win, cache_v_win, state_conv, cache_k_mem, cache_v_mem,
           rel_bias, g_pre, w_in, g_mem, w_mem_kv, conv_w, conv_b, ln_g, ln_b, w_proj_a, w_proj_b,
           w_proj_c, w_out, g_post):
    depth = g_pre.shape[0]
    assert depth == 1, "single-layer step"
    B, S, _ = x_prompt.shape
    Bd, T, _ = x_sample.shape
    wb = cache_k_win.shape[2]
    past = wb
    assert wb == MAX_DIST and S % (16 * QB) == 0 and T <= QROWS

    l = 0
    w_in_b = w_in[l].astype(BF16)
    wpa, wpb, wpc = w_proj_a[l].astype(BF16), w_proj_b[l].astype(BF16), w_proj_c[l].astype(BF16)
    wout = w_out[l].astype(BF16)
    gpre, gpost = g_pre[l][None], g_post[l][None]
    cb, lg, lb = conv_b[l][None], ln_g[l][None], ln_b[l][None]
    cw = conv_w[l]

    (qkv, k_keep, v_keep, za, glu, zb, qc, zc, gt, glu_tail) = _inproj_prompt(x_prompt, gpre, w_in_b, tm=256)
    mk_f, mv_f, mk_b, mv_b = _memkv(mem_prompt.reshape(B * N_MEM, D_MODEL), g_mem[l][None],
                                    w_mem_kv[l].astype(BF16), tm=512)
    oa = _band_attn(qkv, _band_tables(rel_bias), unroll=8)

    (q_s, k_s, v_s, za_s, glu_s, zb_s, qc_s, zc_s, gt_s) = _inproj_sample(
        x_sample.reshape(Bd * T, D_MODEL), gpre, w_in_b, tm=Bd * T)
    cache_kt = jnp.transpose(cache_k_win[l], (0, 2, 3, 1))
    cache_vt = jnp.transpose(cache_v_win[l], (0, 2, 3, 1))
    cw_tiles = jnp.broadcast_to(cw[:, None, :], (CONV_W, 8, EB))
    obg, oa_s, oc_s = _conv_prompt_sample_attn(
        glu, zb, cw_tiles, cb, lg, lb,
        q_s.reshape(Bd, T, EA), k_s.reshape(Bd, T, EA), v_s.reshape(Bd, T, EA),
        cache_kt, cache_vt, _sample_tables(rel_bias, wb, past, T),
        qc_s.reshape(Bd, T, EC), cache_k_mem[l].reshape(Bd, N_MEM * HC, DC),
        cache_v_mem[l].reshape(Bd, N_MEM * HC, DC), tm=(B * S) // Bd, rc=32, rn=64)
    y_p = _tail_prompt(oa, za, obg, qc, zc, mk_b.reshape(B, N_MEM, EC), mv_b.reshape(B, N_MEM, EC),
                       gt, x_prompt, wpa, wpb, wpc, wout, gpost, tm=256)
    state_t = jnp.transpose(state_conv[l], (1, 0, 2))
    glu_t = jnp.transpose(glu_s.reshape(Bd, T, EB), (1, 0, 2))
    ob_t = _conv_sample(state_t, glu_t, cw, cb, lg, lb, rc=32)
    ob_s = jnp.transpose(ob_t, (1, 0, 2)).reshape(Bd * T, EB)
    y_s = _tail_sample(oa_s.reshape(Bd * T, EA), za_s, ob_s, zb_s,
                       oc_s.reshape(Bd * T, EC), zc_s, gt_s, x_sample.reshape(Bd * T, D_MODEL),
                       wpa, wpb, wpc, wout, gpost, tm=Bd * T)

    conv_state_s = jnp.transpose(jnp.concatenate([state_t[T:], glu_t], axis=0), (1, 0, 2))
    return (y_p, y_s.reshape(Bd, T, D_MODEL),
            k_keep.reshape(1, B, wb, HA, DA), v_keep.reshape(1, B, wb, HA, DA),
            glu_tail[:, HALO - (CONV_W - 1):][None],
            mk_f.reshape(1, B, N_MEM, HC, DC), mv_f.reshape(1, B, N_MEM, HC, DC),
            k_s.reshape(1, Bd, T, HA, DA), v_s.reshape(1, Bd, T, HA, DA),
            conv_state_s[None])
```

```python
import functools

import jax
import jax.numpy as jnp
import numpy as np
from jax import lax
from jax.experimental import pallas as pl
from jax.experimental.pallas import tpu as pltpu

F32 = jnp.float32
BF16 = jnp.bfloat16

D_MODEL = 1024
HA, DA = 12, 64
EA = HA * DA
NHP = HA // 2
PATTERNS = ((128, 1), (512, 4), (2048, 16))
NW = 128
QB = 128
EB = 768
CONV_W = 31
HC, DC = 4, 128
EC = HC * DC
N_MEM = 256
N_BUCKETS = 32
MAX_DIST = 2048
EPS = 1e-6
NEG = -1e30
LOG2E = 1.4426950408889634

C_Q, C_K, C_V, C_ZA = 0, EA, 2 * EA, 3 * EA
C_U = 4 * EA
C_G = C_U + EB
C_ZB = C_U + 2 * EB
C_QC = C_ZB + EB
C_ZC = C_QC + EC
C_GT = C_ZC + EC
IN_COLS = C_GT + 3 * D_MODEL

V7X_VMEM_LIMIT = 56 * 1024 * 1024
HALO = 32
QROWS = 8
NT_DIMS = (((1,), (1,)), ((), ()))


def _t5_bucket_ids(n):
    exact = N_BUCKETS // 2
    nf = np.maximum(n, 1).astype(np.float32)
    scale = np.float32(N_BUCKETS - exact) / np.log(np.float32(MAX_DIST) / np.float32(exact))
    large = exact + (np.log(nf / np.float32(exact)) * scale).astype(np.int32)
    large = np.minimum(large, N_BUCKETS - 1)
    return np.where(n < exact, n, large).astype(np.int32)


def _bias_by_bucket(rel_bias, ids):
    onehot = (ids[None] == np.arange(N_BUCKETS).reshape((-1,) + (1,) * ids.ndim)).astype(np.float32)
    return jnp.tensordot(rel_bias.astype(F32).T, jnp.asarray(onehot), axes=1,
                         precision=lax.Precision.HIGHEST)


def _sigmoid(x):
    return 1.0 / (1.0 + jnp.exp(-x))


def _silu(x):
    return x * _sigmoid(x)


def _rmsnorm_f32(x, g):
    return x * lax.rsqrt(jnp.mean(x * x, axis=-1, keepdims=True) + EPS) * g


def _params(*sem):
    return pltpu.CompilerParams(dimension_semantics=sem, vmem_limit_bytes=V7X_VMEM_LIMIT)


def _inproj_prompt_kernel(x_ref, g_ref, w_ref, qkv_ref, kf_ref, vf_ref, za_ref, glu_ref, zb_ref,
                          qc_ref, zc_ref, gt_ref, tail_ref):
    xn = _rmsnorm_f32(x_ref[0], g_ref[...]).astype(BF16)

    def mm(lo, n):
        return jnp.dot(xn, w_ref[:, lo:lo + n], preferred_element_type=F32)

    def put_pairs(res, base):
        for c in range(NHP):
            qkv_ref[0, base + c] = res[:, c * 128:(c + 1) * 128].astype(BF16)

    put_pairs(mm(C_Q, EA) * (DA ** -0.5 * LOG2E), 0)
    k = mm(C_K, EA)
    put_pairs(k, NHP)
    kf_ref[0] = k
    v = mm(C_V, EA)
    put_pairs(v, 2 * NHP)
    vf_ref[0] = v
    za_ref[0] = mm(C_ZA, EA).astype(BF16)
    glu = mm(C_U, EB) * _sigmoid(mm(C_G, EB))
    glu_ref[0] = glu.astype(BF16)
    tail_ref[0] = glu[glu.shape[0] - HALO:, :]
    zb_ref[0] = mm(C_ZB, EB).astype(BF16)
    qc_ref[0] = (mm(C_QC, EC) * (DC ** -0.5)).astype(BF16)
    zc_ref[0] = mm(C_ZC, EC).astype(BF16)
    for c in range(3):
        gt_ref[0, :, c * D_MODEL:(c + 1) * D_MODEL] = mm(C_GT + c * D_MODEL, D_MODEL).astype(BF16)


def _inproj_prompt(x, g, w, tm):
    B, S, _ = x.shape
    nt = S // tm
    wb = min(MAX_DIST, S)
    first_kept = (S - wb) // tm
    row = lambda b, i: (b, i, 0)
    kept = lambda b, i: (b, jnp.maximum(i - first_kept, 0), 0)
    out_shape = (
        jax.ShapeDtypeStruct((B, 3 * NHP, S, 128), BF16),
        jax.ShapeDtypeStruct((B, wb, EA), F32),
        jax.ShapeDtypeStruct((B, wb, EA), F32),
        jax.ShapeDtypeStruct((B, S, EA), BF16),
        jax.ShapeDtypeStruct((B, S, EB), BF16),
        jax.ShapeDtypeStruct((B, S, EB), BF16),
        jax.ShapeDtypeStruct((B, S, EC), BF16),
        jax.ShapeDtypeStruct((B, S, EC), BF16),
        jax.ShapeDtypeStruct((B, S, 3 * D_MODEL), BF16),
        jax.ShapeDtypeStruct((B, HALO, EB), F32),
    )
    out_specs = (
        pl.BlockSpec((1, 3 * NHP, tm, 128), lambda b, i: (b, 0, i, 0)),
        pl.BlockSpec((1, tm, EA), kept),
        pl.BlockSpec((1, tm, EA), kept),
        pl.BlockSpec((1, tm, EA), row),
        pl.BlockSpec((1, tm, EB), row),
        pl.BlockSpec((1, tm, EB), row),
        pl.BlockSpec((1, tm, EC), row),
        pl.BlockSpec((1, tm, EC), row),
        pl.BlockSpec((1, tm, 3 * D_MODEL), row),
        pl.BlockSpec((1, HALO, EB), lambda b, i: (b, 0, 0)),
    )
    return pl.pallas_call(
        _inproj_prompt_kernel,
        out_shape=out_shape,
        grid=(B, nt),
        in_specs=[
            pl.BlockSpec((1, tm, D_MODEL), row),
            pl.BlockSpec((1, D_MODEL), lambda b, i: (0, 0)),
            pl.BlockSpec((D_MODEL, IN_COLS), lambda b, i: (0, 0), pipeline_mode=pl.Buffered(1)),
        ],
        out_specs=out_specs,
        compiler_params=_params("arbitrary", "arbitrary"),
        name="inproj_prompt",
    )(x, g, w)


def _inproj_sample_kernel(x_ref, g_ref, w_ref, q_ref, k_ref, v_ref, za_ref, glu_ref, zb_ref,
                          qc_ref, zc_ref, gt_ref):
    xn = _rmsnorm_f32(x_ref[...], g_ref[...]).astype(BF16)

    def mm(lo, n):
        return jnp.dot(xn, w_ref[:, lo:lo + n], preferred_element_type=F32)

    q_ref[...] = mm(C_Q, EA) * (DA ** -0.5)
    k_ref[...] = mm(C_K, EA)
    v_ref[...] = mm(C_V, EA)
    za_ref[...] = mm(C_ZA, EA).astype(BF16)
    glu_ref[...] = mm(C_U, EB) * _sigmoid(mm(C_G, EB))
    zb_ref[...] = mm(C_ZB, EB).astype(BF16)
    qc_ref[...] = mm(C_QC, EC) * (DC ** -0.5)
    zc_ref[...] = mm(C_ZC, EC).astype(BF16)
    for c in range(3):
        gt_ref[:, c * D_MODEL:(c + 1) * D_MODEL] = mm(C_GT + c * D_MODEL, D_MODEL).astype(BF16)


def _inproj_sample(x, g, w, tm):
    M = x.shape[0]
    row = lambda i: (i, 0)
    widths = (EA, EA, EA, EA, EB, EB, EC, EC, 3 * D_MODEL)
    dtypes = (F32, F32, F32, BF16, F32, BF16, F32, BF16, BF16)
    return pl.pallas_call(
        _inproj_sample_kernel,
        out_shape=tuple(jax.ShapeDtypeStruct((M, n), dt) for n, dt in zip(widths, dtypes)),
        grid=(M // tm,),
        in_specs=[
            pl.BlockSpec((tm, D_MODEL), row),
            pl.BlockSpec((1, D_MODEL), lambda i: (0, 0)),
            pl.BlockSpec((D_MODEL, IN_COLS), lambda i: (0, 0), pipeline_mode=pl.Buffered(1)),
        ],
        out_specs=tuple(pl.BlockSpec((tm, n), row) for n in widths),
        compiler_params=_params("arbitrary"),
        name="inproj_sample",
    )(x, g, w)


def _memkv_kernel(m_ref, g_ref, w_ref, kf_ref, vf_ref, kb_ref, vb_ref):
    xn = _rmsnorm_f32(m_ref[...], g_ref[...]).astype(BF16)
    k = jnp.dot(xn, w_ref[:, :EC], preferred_element_type=F32)
    v = jnp.dot(xn, w_ref[:, EC:], preferred_element_type=F32)
    kf_ref[...] = k
    vf_ref[...] = v
    kb_ref[...] = k.astype(BF16)
    vb_ref[...] = v.astype(BF16)


def _memkv(mem, g, w, tm):
    M = mem.shape[0]
    row = lambda i: (i, 0)
    return pl.pallas_call(
        _memkv_kernel,
        out_shape=(jax.ShapeDtypeStruct((M, EC), F32), jax.ShapeDtypeStruct((M, EC), F32),
                   jax.ShapeDtypeStruct((M, EC), BF16), jax.ShapeDtypeStruct((M, EC), BF16)),
        grid=(M // tm,),
        in_specs=[pl.BlockSpec((tm, D_MODEL), row),
                  pl.BlockSpec((1, D_MODEL), lambda i: (0, 0)),
                  pl.BlockSpec((D_MODEL, 2 * EC), lambda i: (0, 0))],
        out_specs=tuple(pl.BlockSpec((tm, EC), row) for _ in range(4)),
        compiler_params=_params("arbitrary"),
        name="memkv",
    )(mem, g, w)


def _band_tables(rel_bias):
    iq = np.arange(QB)[:, None]
    ik = np.arange(2 * QB)[None, :]
    dist = iq - ik + QB
    band = (dist >= 0) & (dist <= NW)
    out = []
    for (_, d) in PATTERNS:
        bias = LOG2E * _bias_by_bucket(rel_bias, _t5_bucket_ids(np.clip(dist, 0, NW) * d))
        later = jnp.where(jnp.asarray(band)[None], bias, NEG)
        first = jnp.where(jnp.asarray(band & (ik >= QB))[None], bias, NEG)
        t = jnp.stack([first, later], axis=1)
        out.append(t.reshape(NHP, 2, 2, QB, 2 * QB).transpose(0, 2, 1, 3, 4)
                   .reshape(NHP, 2, 2 * QB, 2 * QB))
    return jnp.stack(out)


def _band_attn_kernel(q_ref, k_ref, v_ref, t_ref, oa_ref,
                      nat32, p4_32, p4_ref, p16_ref, a4_ref, l4_ref, m4_ref, a16_ref, l16_ref, m16_ref,
                      *, S, unroll):
    n_units = S // QB
    lane = lax.broadcasted_iota(jnp.int32, (QB, 128), 1)
    head0 = lane < DA
    keep0 = head0.astype(F32).astype(BF16)
    keep1 = (1.0 - head0.astype(F32)).astype(BF16)
    ones = jnp.ones((2 * QB, 128), BF16)
    n4 = S // 4

    for t, src in enumerate((q_ref, k_ref, v_ref)):
        def widen(c, carry, src=src):
            r0 = pl.multiple_of(c * 256, 256)
            nat32[pl.ds(r0, 256), :] = src[0, 0, pl.ds(r0, 256), :].astype(F32)
            return carry
        lax.fori_loop(0, S // 256, widen, 0)
        for r in range(4):
            def by4(c, carry, r=r, t=t):
                u0 = pl.multiple_of(c * 256, 256)
                x = nat32[pl.ds(4 * u0 + r, 256, stride=4), :]
                p4_32[pl.ds(r * n4 + u0, 256), :] = x
                p4_ref[t, pl.ds(r * n4 + u0, 256), :] = x.astype(BF16)
                return carry
            lax.fori_loop(0, n4 // 256, by4, 0)
        for r4 in range(4):
            for s in range(4):
                x = p4_32[pl.ds(r4 * n4 + s, n4 // 4, stride=4), :]
                p16_ref[t, pl.ds((4 * s + r4) * (n4 // 4), n4 // 4), :] = x.astype(BF16)

    def unit(u, pat, d, qsrc, ksrc, vsrc):
        nqb = n_units // d
        row = pl.multiple_of(u * QB, QB)
        prow = pl.multiple_of(jnp.maximum(u - 1, 0) * QB, QB)
        q2 = qsrc[pl.ds(row, QB), :]
        kk = jnp.concatenate([ksrc[pl.ds(prow, QB), :], ksrc[pl.ds(row, QB), :]], axis=0)
        vv = jnp.concatenate([vsrc[pl.ds(prow, QB), :], vsrc[pl.ds(row, QB), :]], axis=0)
        qs = jnp.concatenate([q2 * keep0, q2 * keep1], axis=0)
        s = lax.dot_general(qs, kk, NT_DIMS, preferred_element_type=F32)
        s = s + t_ref[pat, 0, jnp.minimum(u & (nqb - 1), 1)]
        m = jnp.max(s, axis=-1, keepdims=True)
        p = jnp.exp2(s - m).astype(BF16)
        oe = jnp.dot(p, jnp.concatenate([vv, ones], axis=1), preferred_element_type=F32)
        mb = jnp.broadcast_to(m, (2 * QB, 128))
        return (jnp.where(head0, oe[:QB, :128], oe[QB:, :128]),
                jnp.where(head0, oe[:QB, 128:], oe[QB:, 128:]),
                jnp.where(head0, mb[:QB], mb[QB:]))

    def strided_pattern(pat, d, src, acc_ref, den_ref, max_ref):
        nqb = n_units // d

        def body(u, carry):
            acc, den, mx = unit(u, pat, d, src.at[0], src.at[1], src.at[2])
            start = u // nqb + (u & (nqb - 1)) * (d * QB)
            acc_ref[pl.ds(start, QB, stride=d), :] = acc
            den_ref[pl.ds(start, QB, stride=d), :] = den
            max_ref[pl.ds(start, QB, stride=d), :] = mx
            return carry
        lax.fori_loop(0, n_units, body, 0, unroll=unroll)

    strided_pattern(2, 16, p16_ref, a16_ref, l16_ref, m16_ref)
    strided_pattern(1, 4, p4_ref, a4_ref, l4_ref, m4_ref)

    def dense(u, carry):
        a1, l1, m1 = unit(u, 0, 1, q_ref.at[0, 0], k_ref.at[0, 0], v_ref.at[0, 0])
        rows = pl.ds(pl.multiple_of(u * QB, QB), QB)
        m4, m16 = m4_ref[rows, :], m16_ref[rows, :]
        mx = jnp.maximum(jnp.maximum(m1, m4), m16)
        e1, e4, e16 = jnp.exp2(m1 - mx), jnp.exp2(m4 - mx), jnp.exp2(m16 - mx)
        num = a1 * e1 + a4_ref[rows, :] * e4 + a16_ref[rows, :] * e16
        den = l1 * e1 + l4_ref[rows, :] * e4 + l16_ref[rows, :] * e16
        oa_ref[0, 0, rows, :] = (num / den).astype(BF16)
        return carry
    lax.fori_loop(0, n_units, dense, 0, unroll=unroll)


def _band_attn(qkv, tables, unroll):
    B, _, S, _ = qkv.shape
    blk = (1, 1, S, 128)
    return pl.pallas_call(
        functools.partial(_band_attn_kernel, S=S, unroll=unroll),
        out_shape=jax.ShapeDtypeStruct((B, NHP, S, 128), BF16),
        grid=(B, NHP),
        in_specs=[pl.BlockSpec(blk, lambda b, h: (b, h, 0, 0)),
                  pl.BlockSpec(blk, lambda b, h: (b, NHP + h, 0, 0)),
                  pl.BlockSpec(blk, lambda b, h: (b, 2 * NHP + h, 0, 0)),
                  pl.BlockSpec((3, 1, 2, 2 * QB, 2 * QB), lambda b, h: (0, h, 0, 0, 0))],
        out_specs=pl.BlockSpec(blk, lambda b, h: (b, h, 0, 0)),
        scratch_shapes=[pltpu.VMEM((S, 128), F32), pltpu.VMEM((S, 128), F32),
                        pltpu.VMEM((3, S, 128), BF16), pltpu.VMEM((3, S, 128), BF16)]
                       + [pltpu.VMEM((S, 128), F32)] * 6,
        compiler_params=_params("arbitrary", "arbitrary"),
        name="band_attn",
    )(qkv, qkv, qkv, tables)


def _layernorm_silu(c, g, b):
    mu = jnp.mean(c, axis=-1, keepdims=True)
    cc = c - mu
    var = jnp.mean(cc * cc, axis=-1, keepdims=True)
    return _silu(cc * lax.rsqrt(var + EPS) * g + b)


FIRST_TAP = HALO - (CONV_W - 1)


def _conv_stage(glu_ref, halo_ref, seq_start, full_ref, tm):
    halo = halo_ref[0].astype(F32)
    full_ref[0, 0:HALO, :] = jnp.where(seq_start, jnp.zeros_like(halo), halo)
    full_ref[0, HALO:HALO + tm, :] = glu_ref[0].astype(F32)
    n_shift = HALO + tm - 8
    for s in range(1, 8):
        full_ref[s, 0:n_shift, :] = full_ref[0, s:s + n_shift, :]


def _conv_taps(c, full_ref, cw_ref, conv_ref, rc):
    base = c * rc
    accs = [jnp.zeros((8, EB), F32) for _ in range(rc // 8)]
    for j in range(CONV_W):
        off = FIRST_TAP + j
        w8 = cw_ref[j]
        for g in range(rc // 8):
            accs[g] = accs[g] + full_ref[off % 8, pl.ds(base + (off // 8) * 8 + 8 * g, 8), :] * w8
    conv_ref[pl.ds(base, rc), :] = jnp.concatenate(accs, axis=0)


def _conv_norm_gate(c, conv_ref, zb_ref, cb_ref, lg_ref, lb_ref, o_ref, rn):
    rows = pl.ds(c * rn, rn)
    ob = _layernorm_silu(conv_ref[rows, :] + cb_ref[...], lg_ref[...], lb_ref[...])
    o_ref[0, rows, :] = (ob * _silu(zb_ref[0, rows, :].astype(F32))).astype(BF16)


def _gated_tail(a_g, b_g, c_g, gt_ref, x, wpa_ref, wpb_ref, wpc_ref, wout_ref, gpost_ref):
    def gate(k):
        return _sigmoid(gt_ref[:, k * D_MODEL:(k + 1) * D_MODEL].astype(F32))

    mix = gate(0) * jnp.dot(a_g, wpa_ref[...], preferred_element_type=F32)
    mix = mix + gate(1) * jnp.dot(b_g, wpb_ref[...], preferred_element_type=F32)
    mix = mix + gate(2) * jnp.dot(c_g, wpc_ref[...], preferred_element_type=F32)
    z = jnp.dot(mix.astype(BF16), wout_ref[...], preferred_element_type=F32)
    return x + _rmsnorm_f32(z, gpost_ref[...])


def _tail_prompt_kernel(oa_ref, za_ref, obg_ref, qc_ref, zc_ref, mk_ref, mv_ref, gt_ref, x_ref,
                        wpa_ref, wpb_ref, wpc_ref, wout_ref, gpost_ref, y_ref):
    a_g = jnp.concatenate(
        [(oa_ref[0, c].astype(F32) * _silu(za_ref[0, :, c * 128:(c + 1) * 128].astype(F32))).astype(BF16)
         for c in range(NHP)], axis=-1)
    pieces = []
    for h in range(HC):
        hs = slice(h * DC, (h + 1) * DC)
        s = lax.dot_general(qc_ref[0, :, hs], mk_ref[0, :, hs], NT_DIMS, preferred_element_type=F32)
        p = jnp.exp(s - jnp.max(s, axis=-1, keepdims=True))
        l = jnp.sum(p, axis=-1, keepdims=True)
        oc = jnp.dot(p.astype(BF16), mv_ref[0, :, hs], preferred_element_type=F32) / l
        pieces.append((oc * _silu(zc_ref[0, :, hs].astype(F32))).astype(BF16))
    c_g = jnp.concatenate(pieces, axis=-1)
    y_ref[0] = _gated_tail(a_g, obg_ref[0], c_g, gt_ref.at[0], x_ref[0],
                           wpa_ref, wpb_ref, wpc_ref, wout_ref, gpost_ref)


def _tail_prompt(oa, za, obg, qc, zc, mk, mv, gt, x, wpa, wpb, wpc, wout, gpost, tm):
    B, S, _ = x.shape
    row = lambda b, i: (b, i, 0)
    const = lambda b, i: (0, 0)
    return pl.pallas_call(
        _tail_prompt_kernel,
        out_shape=jax.ShapeDtypeStruct((B, S, D_MODEL), F32),
        grid=(B, S // tm),
        in_specs=[
            pl.BlockSpec((1, NHP, tm, 128), lambda b, i: (b, 0, i, 0)),
            pl.BlockSpec((1, tm, EA), row), pl.BlockSpec((1, tm, EB), row),
            pl.BlockSpec((1, tm, EC), row), pl.BlockSpec((1, tm, EC), row),
            pl.BlockSpec((1, N_MEM, EC), lambda b, i: (b, 0, 0)),
            pl.BlockSpec((1, N_MEM, EC), lambda b, i: (b, 0, 0)),
            pl.BlockSpec((1, tm, 3 * D_MODEL), row), pl.BlockSpec((1, tm, D_MODEL), row),
            pl.BlockSpec((EA, D_MODEL), const), pl.BlockSpec((EB, D_MODEL), const),
            pl.BlockSpec((EC, D_MODEL), const), pl.BlockSpec((D_MODEL, D_MODEL), const),
            pl.BlockSpec((1, D_MODEL), const)],
        out_specs=pl.BlockSpec((1, tm, D_MODEL), row),
        compiler_params=_params("arbitrary", "arbitrary"),
        name="tail_prompt",
    )(oa, za, obg, qc, zc, mk, mv, gt, x, wpa, wpb, wpc, wout, gpost)


def _tail_sample_kernel(oa_ref, za_ref, ob_ref, zb_ref, oc_ref, zc_ref, gt_ref, x_ref,
                        wpa_ref, wpb_ref, wpc_ref, wout_ref, gpost_ref, y_ref):
    a_g = (oa_ref[...] * _silu(za_ref[...].astype(F32))).astype(BF16)
    b_g = (ob_ref[...] * _silu(zb_ref[...].astype(F32))).astype(BF16)
    c_g = (oc_ref[...] * _silu(zc_ref[...].astype(F32))).astype(BF16)
    y_ref[...] = _gated_tail(a_g, b_g, c_g, gt_ref, x_ref[...],
                             wpa_ref, wpb_ref, wpc_ref, wout_ref, gpost_ref)


def _tail_sample(oa, za, ob, zb, oc, zc, gt, x, wpa, wpb, wpc, wout, gpost, tm):
    M = x.shape[0]
    row = lambda i: (i, 0)
    const = lambda i: (0, 0)
    widths = (EA, EA, EB, EB, EC, EC, 3 * D_MODEL, D_MODEL)
    return pl.pallas_call(
        _tail_sample_kernel,
        out_shape=jax.ShapeDtypeStruct((M, D_MODEL), F32),
        grid=(M // tm,),
        in_specs=[pl.BlockSpec((tm, n), row) for n in widths] + [
            pl.BlockSpec((EA, D_MODEL), const), pl.BlockSpec((EB, D_MODEL), const),
            pl.BlockSpec((EC, D_MODEL), const), pl.BlockSpec((D_MODEL, D_MODEL), const),
            pl.BlockSpec((1, D_MODEL), const)],
        out_specs=pl.BlockSpec((tm, D_MODEL), row),
        compiler_params=_params("arbitrary"),
        name="tail_sample",
    )(oa, za, ob, zb, oc, zc, gt, x, wpa, wpb, wpc, wout, gpost)


def _sample_tables(rel_bias, wb, past, T):
    def mult_of(dl, real):
        m = np.zeros(dl.shape, np.float32)
        for (w, d) in PATTERNS:
            m += (real & (dl % d == 0) & (dl <= w)).astype(np.float32)
        return m

    def tables(dl, real):
        row = np.arange(QROWS)[:, None]
        real = real & (dl >= 0) & (past + row - dl >= 0) & (wb + row - dl >= 0)
        m = mult_of(dl, real)
        dlc = np.clip(dl, 0, MAX_DIST)
        m[T:] = m[0]
        bias = jnp.where(jnp.asarray(m > 0)[None], _bias_by_bucket(rel_bias, _t5_bucket_ids(dlc)), NEG)
        return bias, jnp.asarray(m)

    i = np.arange(QROWS)[:, None]
    pos = np.arange(wb)[None, :]
    cache_b, cache_m = tables(wb + i - pos, np.ones((QROWS, wb), bool))
    j = np.arange(QROWS)[None, :]
    new_b, new_m = tables(i - j, np.broadcast_to(j < T, (QROWS, QROWS)))
    return cache_b, cache_m, new_b, new_m


def _pad_rows(x, rows):
    return jnp.concatenate([x, jnp.zeros((rows - x.shape[0], x.shape[1]), x.dtype)], axis=0)


def _sample_heads(heads, qh_ref, knh_ref, vnh_ref, kt_ref, vt_ref, cb_ref, cm_ref, nb_ref, nm_ref, oh_ref):
    scores = []
    for h in heads:
        qh = qh_ref[h].astype(BF16)
        s = jnp.dot(qh, kt_ref[0, h].astype(BF16), preferred_element_type=F32) + cb_ref[h]
        sn = lax.dot_general(qh, knh_ref[h].astype(BF16), NT_DIMS, preferred_element_type=F32) + nb_ref[h]
        scores.append((s, sn))
    probs = []
    for s, sn in scores:
        m = jnp.maximum(jnp.max(s, axis=-1, keepdims=True), jnp.max(sn, axis=-1, keepdims=True))
        p = jnp.exp(s - m) * cm_ref[...]
        pn = jnp.exp(sn - m) * nm_ref[...]
        l = jnp.sum(p, axis=-1, keepdims=True) + jnp.sum(pn, axis=-1, keepdims=True)
        probs.append((p.astype(BF16), pn.astype(BF16), l))
    for h, (p, pn, l) in zip(heads, probs):
        o = lax.dot_general(p, vt_ref[0, h].astype(BF16), NT_DIMS, preferred_element_type=F32)
        o = o + jnp.dot(pn, vnh_ref[h].astype(BF16), preferred_element_type=F32)
        oh_ref[h] = o / l


def _sample_cross_attn(qc_ref, mk_ref, mv_ref, xm_ref, oc_ref, T):
    qc8 = _pad_rows(qc_ref[0], QROWS)
    qc_heads = jnp.concatenate([qc8[:, h * DC:(h + 1) * DC] for h in range(HC)], axis=0).astype(BF16)
    sc = lax.dot_general(qc_heads, mk_ref[0].astype(BF16), NT_DIMS, preferred_element_type=F32) + xm_ref[...]
    pc = jnp.exp(sc - jnp.max(sc, axis=-1, keepdims=True))
    lc = jnp.sum(pc, axis=-1, keepdims=True)
    oc = jnp.dot(pc.astype(BF16), mv_ref[0].astype(BF16), preferred_element_type=F32) / lc
    oc_ref[0] = jnp.concatenate([oc[h * QROWS:h * QROWS + T] for h in range(HC)], axis=-1)


def _conv_attn_kernel(glu_ref, halo_ref, zb_ref, cw_ref, cb_ref, lg_ref, lb_ref,
                      q_ref, kt_ref, vt_ref, kn_ref, vn_ref, tb_ref, tm_ref, nb_ref, nm_ref,
                      qc_ref, mk_ref, mv_ref, xm_ref,
                      obg_ref, oa_ref, oc_ref,
                      full_ref, conv_ref, qh_ref, knh_ref, vnh_ref, oh_ref,
                      *, tm, rc, rn, T, tiles_per_seq):
    g = pl.program_id(0)
    _conv_stage(glu_ref, halo_ref, g % tiles_per_seq == 0, full_ref, tm)
    q8, kn8, vn8 = _pad_rows(q_ref[0], QROWS), _pad_rows(kn_ref[0], QROWS), _pad_rows(vn_ref[0], QROWS)
    for h in range(HA):
        hs = slice(h * DA, (h + 1) * DA)
        qh_ref[h], knh_ref[h], vnh_ref[h] = q8[:, hs], kn8[:, hs], vn8[:, hs]
    _sample_cross_attn(qc_ref, mk_ref, mv_ref, xm_ref, oc_ref, T)
    for c in range(tm // rc):
        _conv_taps(c, full_ref, cw_ref, conv_ref, rc)
    _sample_heads(list(range(HA)), qh_ref, knh_ref, vnh_ref, kt_ref, vt_ref,
                  tb_ref, tm_ref, nb_ref, nm_ref, oh_ref)
    for c in range(tm // rn):
        _conv_norm_gate(c, conv_ref, zb_ref, cb_ref, lg_ref, lb_ref, obg_ref, rn)
    oa_ref[0] = jnp.concatenate([oh_ref[h][:T] for h in range(HA)], axis=-1)


def _conv_prompt_sample_attn(glu, zb, cw, cb, lg, lb, q, k_new, v_new, cache_kt, cache_vt, tables,
                             qc, mem_k, mem_v, tm, rc, rn):
    B, S, _ = glu.shape
    Bd, T, _ = q.shape
    wb = cache_kt.shape[-1]
    tiles_per_seq = S // tm
    assert B * tiles_per_seq == Bd and HA % (tm // rc) == 0
    cache_b, cache_m, new_b, new_m = tables
    own_head = jnp.asarray(np.where(
        np.arange(HC * QROWS)[:, None] // QROWS == np.arange(N_MEM * HC)[None, :] % HC, 0.0, NEG
    ).astype(np.float32))
    tile = lambda g: (g // tiles_per_seq, g % tiles_per_seq, 0)
    halo = lambda g: (g // tiles_per_seq, jnp.maximum((g % tiles_per_seq) * (tm // HALO) - 1, 0), 0)
    per_b3 = lambda g: (g, 0, 0)
    per_b4 = lambda g: (g, 0, 0, 0)
    c2 = lambda g: (0, 0)
    c3 = lambda g: (0, 0, 0)
    head_scratch = pltpu.VMEM((HA, QROWS, DA), F32)
    return pl.pallas_call(
        functools.partial(_conv_attn_kernel, tm=tm, rc=rc, rn=rn, T=T, tiles_per_seq=tiles_per_seq),
        out_shape=(jax.ShapeDtypeStruct((B, S, EB), BF16),
                   jax.ShapeDtypeStruct((Bd, T, EA), F32), jax.ShapeDtypeStruct((Bd, T, EC), F32)),
        grid=(Bd,),
        in_specs=[pl.BlockSpec((1, tm, EB), tile), pl.BlockSpec((1, HALO, EB), halo),
                  pl.BlockSpec((1, tm, EB), tile),
                  pl.BlockSpec((CONV_W, 8, EB), c3),
                  pl.BlockSpec((1, EB), c2), pl.BlockSpec((1, EB), c2), pl.BlockSpec((1, EB), c2),
                  pl.BlockSpec((1, T, EA), per_b3),
                  pl.BlockSpec((1, HA, DA, wb), per_b4), pl.BlockSpec((1, HA, DA, wb), per_b4),
                  pl.BlockSpec((1, T, EA), per_b3), pl.BlockSpec((1, T, EA), per_b3),
                  pl.BlockSpec((HA, QROWS, wb), c3), pl.BlockSpec((QROWS, wb), c2),
                  pl.BlockSpec((HA, QROWS, QROWS), c3), pl.BlockSpec((QROWS, QROWS), c2),
                  pl.BlockSpec((1, T, EC), per_b3),
                  pl.BlockSpec((1, N_MEM * HC, DC), per_b3), pl.BlockSpec((1, N_MEM * HC, DC), per_b3),
                  pl.BlockSpec((HC * QROWS, N_MEM * HC), c2)],
        out_specs=(pl.BlockSpec((1, tm, EB), tile),
                   pl.BlockSpec((1, T, EA), per_b3), pl.BlockSpec((1, T, EC), per_b3)),
        scratch_shapes=[pltpu.VMEM((8, HALO + tm, EB), F32), pltpu.VMEM((tm, EB), F32),
                        head_scratch, head_scratch, head_scratch, head_scratch],
        compiler_params=_params("arbitrary"),
        name="conv_prompt_sample_attn",
    )(glu, glu, zb, cw, cb, lg, lb, q, cache_kt, cache_vt, k_new, v_new,
      cache_b, cache_m, new_b, new_m, qc, mem_k, mem_v, own_head)


def _conv_sample_kernel(st_ref, glu_ref, cw_ref, cb_ref, lg_ref, lb_ref, o_ref, *, T, rc):
    n_hist = CONV_W - 1
    Bd = st_ref.shape[1]
    for i in range(T):
        def chunk(c, carry, i=i):
            rows = pl.ds(pl.multiple_of(c * rc, rc), rc)
            acc = jnp.zeros((rc, EB), F32)
            for j in range(CONV_W):
                src = st_ref[i + j, rows, :] if i + j < n_hist else glu_ref[i + j - n_hist, rows, :]
                acc = acc + src * cw_ref[j:j + 1, :]
            o_ref[i, rows, :] = _layernorm_silu(acc + cb_ref[...], lg_ref[...], lb_ref[...])
            return carry
        lax.fori_loop(0, Bd // rc, chunk, 0)


def _conv_sample(state_t, glu_t, cw, cb, lg, lb, rc):
    T, Bd, _ = glu_t.shape
    return pl.pallas_call(
        functools.partial(_conv_sample_kernel, T=T, rc=rc),
        out_shape=jax.ShapeDtypeStruct((T, Bd, EB), F32),
        compiler_params=pltpu.CompilerParams(vmem_limit_bytes=V7X_VMEM_LIMIT),
        name="conv_sample",
    )(state_t, glu_t, cw, cb, lg, lb)


def kernel(x_prompt, x_sample, mem_prompt, cache_k_win, cache_v_win, state_conv, cache_k_mem, cache_v_mem,
           rel_bias, g_pre, w_in, g_mem, w_mem_kv, conv_w, conv_b, ln_g, ln_b, w_proj_a, w_proj_b,
           w_proj_c, w_out, g_post):
    depth = g_pre.shape[0]
    assert depth == 1, "single-layer step"
    B, S, _ = x_prompt.shape
    Bd, T, _ = x_sample.shape
    wb = cache_k_win.shape[2]
    past = wb
    assert wb == MAX_DIST and S % (16 * QB) == 0 and T <= QROWS

    l = 0
    w_in_b = w_in[l].astype(BF16)
    wpa, wpb, wpc = w_proj_a[l].astype(BF16), w_proj_b[l].astype(BF16), w_proj_c[l].astype(BF16)
    wout = w_out[l].astype(BF16)
    gpre, gpost = g_pre[l][None], g_post[l][None]
    cb, lg, lb = conv_b[l][None], ln_g[l][None], ln_b[l][None]
    cw = conv_w[l]

    (qkv, k_keep, v_keep, za, glu, zb, qc, zc, gt, glu_tail) = _inproj_prompt(x_prompt, gpre, w_in_b, tm=256)
    mk_f, mv_f, mk_b, mv_b = _memkv(mem_prompt.reshape(B * N_MEM, D_MODEL), g_mem[l][None],
                                    w_mem_kv[l].astype(BF16), tm=512)
    oa = _band_attn(qkv, _band_tables(rel_bias), unroll=8)

    (q_s, k_s, v_s, za_s, glu_s, zb_s, qc_s, zc_s, gt_s) = _inproj_sample(
        x_sample.reshape(Bd * T, D_MODEL), gpre, w_in_b, tm=Bd * T)
    cache_kt = jnp.transpose(cache_k_win[l], (0, 2, 3, 1))
    cache_vt = jnp.transpose(cache_v_win[l], (0, 2, 3, 1))
    cw_tiles = jnp.broadcast_to(cw[:, None, :], (CONV_W, 8, EB))
    obg, oa_s, oc_s = _conv_prompt_sample_attn(
        glu, zb, cw_tiles, cb, lg, lb,
        q_s.reshape(Bd, T, EA), k_s.reshape(Bd, T, EA), v_s.reshape(Bd, T, EA),
        cache_kt, cache_vt, _sample_tables(rel_bias, wb, past, T),
        qc_s.reshape(Bd, T, EC), cache_k_mem[l].reshape(Bd, N_MEM * HC, DC),
        cache_v_mem[l].reshape(Bd, N_MEM * HC, DC), tm=(B * S) // Bd, rc=32, rn=64)
    y_p = _tail_prompt(oa, za, obg, qc, zc, mk_b.reshape(B, N_MEM, EC), mv_b.reshape(B, N_MEM, EC),
                       gt, x_prompt, wpa, wpb, wpc, wout, gpost, tm=256)
    state_t = jnp.transpose(state_conv[l], (1, 0, 2))
    glu_t = jnp.transpose(glu_s.reshape(Bd, T, EB), (1, 0, 2))
    ob_t = _conv_sample(state_t, glu_t, cw, cb, lg, lb, rc=32)
    ob_s = jnp.transpose(ob_t, (1, 0, 2)).reshape(Bd * T, EB)
    y_s = _tail_sample(oa_s.reshape(Bd * T, EA), za_s, ob_s, zb_s,
                       oc_s.reshape(Bd * T, EC), zc_s, gt_s, x_sample.reshape(Bd * T, D_MODEL),
                       wpa, wpb, wpc, wout, gpost, tm=Bd * T)

    conv_state_s = jnp.transpose(jnp.concatenate([state_t[T:], glu_t], axis=0), (1, 0, 2))
    return (y_p, y_s.reshape(Bd, T, D_MODEL),
            k_keep.reshape(1, B, wb, HA, DA), v_keep.reshape(1, B, wb, HA, DA),
            glu_tail[:, HALO - (CONV_W - 1):][None],
            mk_f.reshape(1, B, N_MEM, HC, DC), mv_f.reshape(1, B, N_MEM, HC, DC),
            k_s.reshape(1, Bd, T, HA, DA), v_s.reshape(1, Bd, T, HA, DA),
            conv_state_s[None])
```

```python
import functools

import jax
import jax.numpy as jnp
import numpy as np
from jax import lax
from jax.experimental import pallas as pl
from jax.experimental.pallas import tpu as pltpu

F32 = jnp.float32
BF16 = jnp.bfloat16

D_MODEL = 1024
HA, DA = 12, 64
EA = HA * DA
NHP = HA // 2
PATTERNS = ((128, 1), (512, 4), (2048, 16))
NW = 128
QB = 128
EB = 768
CONV_W = 31
HC, DC = 4, 128
EC = HC * DC
N_MEM = 256
N_BUCKETS = 32
MAX_DIST = 2048
EPS = 1e-6
NEG = -1e30
LOG2E = 1.4426950408889634

C_Q, C_K, C_V, C_ZA = 0, EA, 2 * EA, 3 * EA
C_U = 4 * EA
C_G = C_U + EB
C_ZB = C_U + 2 * EB
C_QC = C_ZB + EB
C_ZC = C_QC + EC
C_GT = C_ZC + EC
IN_COLS = C_GT + 3 * D_MODEL

V7X_VMEM_LIMIT = 56 * 1024 * 1024
HALO = 32
QROWS = 8
NT_DIMS = (((1,), (1,)), ((), ()))


def _t5_bucket_ids(n):
    exact = N_BUCKETS // 2
    nf = np.maximum(n, 1).astype(np.float32)
    scale = np.float32(N_BUCKETS - exact) / np.log(np.float32(MAX_DIST) / np.float32(exact))
    large = exact + (np.log(nf / np.float32(exact)) * scale).astype(np.int32)
    large = np.minimum(large, N_BUCKETS - 1)
    return np.where(n < exact, n, large).astype(np.int32)


def _bias_by_bucket(rel_bias, ids):
    onehot = (ids[None] == np.arange(N_BUCKETS).reshape((-1,) + (1,) * ids.ndim)).astype(np.float32)
    return jnp.tensordot(rel_bias.astype(F32).T, jnp.asarray(onehot), axes=1,
                         precision=lax.Precision.HIGHEST)


def _sigmoid(x):
    return 1.0 / (1.0 + jnp.exp(-x))


def _silu(x):
    return x * _sigmoid(x)


def _rmsnorm_f32(x, g):
    return x * lax.rsqrt(jnp.mean(x * x, axis=-1, keepdims=True) + EPS) * g


def _params(*sem):
    return pltpu.CompilerParams(dimension_semantics=sem, vmem_limit_bytes=V7X_VMEM_LIMIT)


def _inproj_prompt_kernel(x_ref, g_ref, w_ref, qkv_ref, kf_ref, vf_ref, za_ref, glu_ref, zb_ref,
                          qc_ref, zc_ref, gt_ref, tail_ref):
    xn = _rmsnorm_f32(x_ref[0], g_ref[...]).astype(BF16)

    def mm(lo, n):
        return jnp.dot(xn, w_ref[:, lo:lo + n], preferred_element_type=F32)

    def put_pairs(res, base):
        for c in range(NHP):
            qkv_ref[0, base + c] = res[:, c * 128:(c + 1) * 128].astype(BF16)

    put_pairs(mm(C_Q, EA) * (DA ** -0.5 * LOG2E), 0)
    k = mm(C_K, EA)
    put_pairs(k, NHP)
    kf_ref[0] = k
    v = mm(C_V, EA)
    put_pairs(v, 2 * NHP)
    vf_ref[0] = v
    za_ref[0] = mm(C_ZA, EA).astype(BF16)
    glu = mm(C_U, EB) * _sigmoid(mm(C_G, EB))
    glu_ref[0] = glu.astype(BF16)
    tail_ref[0] = glu[glu.shape[0] - HALO:, :]
    zb_ref[0] = mm(C_ZB, EB).astype(BF16)
    qc_ref[0] = (mm(C_QC, EC) * (DC ** -0.5)).astype(BF16)
    zc_ref[0] = mm(C_ZC, EC).astype(BF16)
    for c in range(3):
        gt_ref[0, :, c * D_MODEL:(c + 1) * D_MODEL] = mm(C_GT + c * D_MODEL, D_MODEL).astype(BF16)


def _inproj_prompt(x, g, w, tm):
    B, S, _ = x.shape
    nt = S // tm
    wb = min(MAX_DIST, S)
    first_kept = (S - wb) // tm
    row = lambda b, i: (b, i, 0)
    kept = lambda b, i: (b, jnp.maximum(i - first_kept, 0), 0)
    out_shape = (
        jax.ShapeDtypeStruct((B, 3 * NHP, S, 128), BF16),
        jax.ShapeDtypeStruct((B, wb, EA), F32),
        jax.ShapeDtypeStruct((B, wb, EA), F32),
        jax.ShapeDtypeStruct((B, S, EA), BF16),
        jax.ShapeDtypeStruct((B, S, EB), BF16),
        jax.ShapeDtypeStruct((B, S, EB), BF16),
        jax.ShapeDtypeStruct((B, S, EC), BF16),
        jax.ShapeDtypeStruct((B, S, EC), BF16),
        jax.ShapeDtypeStruct((B, S, 3 * D_MODEL), BF16),
        jax.ShapeDtypeStruct((B, HALO, EB), F32),
    )
    out_specs = (
        pl.BlockSpec((1, 3 * NHP, tm, 128), lambda b, i: (b, 0, i, 0)),
        pl.BlockSpec((1, tm, EA), kept),
        pl.BlockSpec((1, tm, EA), kept),
        pl.BlockSpec((1, tm, EA), row),
        pl.BlockSpec((1, tm, EB), row),
        pl.BlockSpec((1, tm, EB), row),
        pl.BlockSpec((1, tm, EC), row),
        pl.BlockSpec((1, tm, EC), row),
        pl.BlockSpec((1, tm, 3 * D_MODEL), row),
        pl.BlockSpec((1, HALO, EB), lambda b, i: (b, 0, 0)),
    )
    return pl.pallas_call(
        _inproj_prompt_kernel,
        out_shape=out_shape,
        grid=(B, nt),
        in_specs=[
            pl.BlockSpec((1, tm, D_MODEL), row),
            pl.BlockSpec((1, D_MODEL), lambda b, i: (0, 0)),
            pl.BlockSpec((D_MODEL, IN_COLS), lambda b, i: (0, 0), pipeline_mode=pl.Buffered(1)),
        ],
        out_specs=out_specs,
        compiler_params=_params("arbitrary", "arbitrary"),
        name="inproj_prompt",
    )(x, g, w)


def _inproj_sample_kernel(x_ref, g_ref, w_ref, q_ref, k_ref, v_ref, za_ref, glu_ref, zb_ref,
                          qc_ref, zc_ref, gt_ref):
    xn = _rmsnorm_f32(x_ref[...], g_ref[...]).astype(BF16)

    def mm(lo, n):
        return jnp.dot(xn, w_ref[:, lo:lo + n], preferred_element_type=F32)

    q_ref[...] = mm(C_Q, EA) * (DA ** -0.5)
    k_ref[...] = mm(C_K, EA)
    v_ref[...] = mm(C_V, EA)
    za_ref[...] = mm(C_ZA, EA).astype(BF16)
    glu_ref[...] = mm(C_U, EB) * _sigmoid(mm(C_G, EB))
    zb_ref[...] = mm(C_ZB, EB).astype(BF16)
    qc_ref[...] = mm(C_QC, EC) * (DC ** -0.5)
    zc_ref[...] = mm(C_ZC, EC).astype(BF16)
    for c in range(3):
        gt_ref[:, c * D_MODEL:(c + 1) * D_MODEL] = mm(C_GT + c * D_MODEL, D_MODEL).astype(BF16)


def _inproj_sample(x, g, w, tm):
    M = x.shape[0]
    row = lambda i: (i, 0)
    widths = (EA, EA, EA, EA, EB, EB, EC, EC, 3 * D_MODEL)
    dtypes = (F32, F32, F32, BF16, F32, BF16, F32, BF16, BF16)
    return pl.pallas_call(
        _inproj_sample_kernel,
        out_shape=tuple(jax.ShapeDtypeStruct((M, n), dt) for n, dt in zip(widths, dtypes)),
        grid=(M // tm,),
        in_specs=[
            pl.BlockSpec((tm, D_MODEL), row),
            pl.BlockSpec((1, D_MODEL), lambda i: (0, 0)),
            pl.BlockSpec((D_MODEL, IN_COLS), lambda i: (0, 0), pipeline_mode=pl.Buffered(1)),
        ],
        out_specs=tuple(pl.BlockSpec((tm, n), row) for n in widths),
        compiler_params=_params("arbitrary"),
        name="inproj_sample",
    )(x, g, w)


def _memkv_kernel(m_ref, g_ref, w_ref, kf_ref, vf_ref, kb_ref, vb_ref):
    xn = _rmsnorm_f32(m_ref[...], g_ref[...]).astype(BF16)
    k = jnp.dot(xn, w_ref[:, :EC], preferred_element_type=F32)
    v = jnp.dot(xn, w_ref[:, EC:], preferred_element_type=F32)
    kf_ref[...] = k
    vf_ref[...] = v
    kb_ref[...] = k.astype(BF16)
    vb_ref[...] = v.astype(BF16)


def _memkv(mem, g, w, tm):
    M = mem.shape[0]
    row = lambda i: (i, 0)
    return pl.pallas_call(
        _memkv_kernel,
        out_shape=(jax.ShapeDtypeStruct((M, EC), F32), jax.ShapeDtypeStruct((M, EC), F32),
                   jax.ShapeDtypeStruct((M, EC), BF16), jax.ShapeDtypeStruct((M, EC), BF16)),
        grid=(M // tm,),
        in_specs=[pl.BlockSpec((tm, D_MODEL), row),
                  pl.BlockSpec((1, D_MODEL), lambda i: (0, 0)),
                  pl.BlockSpec((D_MODEL, 2 * EC), lambda i: (0, 0))],
        out_specs=tuple(pl.BlockSpec((tm, EC), row) for _ in range(4)),
        compiler_params=_params("arbitrary"),
        name="memkv",
    )(mem, g, w)


def _band_tables(rel_bias):
    iq = np.arange(QB)[:, None]
    ik = np.arange(2 * QB)[None, :]
    dist = iq - ik + QB
    band = (dist >= 0) & (dist <= NW)
    out = []
    for (_, d) in PATTERNS:
        bias = LOG2E * _bias_by_bucket(rel_bias, _t5_bucket_ids(np.clip(dist, 0, NW) * d))
        later = jnp.where(jnp.asarray(band)[None], bias, NEG)
        first = jnp.where(jnp.asarray(band & (ik >= QB))[None], bias, NEG)
        t = jnp.stack([first, later], axis=1)
        out.append(t.reshape(NHP, 2, 2, QB, 2 * QB).transpose(0, 2, 1, 3, 4)
                   .reshape(NHP, 2, 2 * QB, 2 * QB))
    return jnp.stack(out)


def _band_attn_kernel(q_ref, k_ref, v_ref, t_ref, oa_ref,
                      nat32, p4_32, p4_ref, p16_ref, a4_ref, l4_ref, m4_ref, a16_ref, l16_ref, m16_ref,
                      *, S, unroll):
    n_units = S // QB
    lane = lax.broadcasted_iota(jnp.int32, (QB, 128), 1)
    head0 = lane < DA
    keep0 = head0.astype(F32).astype(BF16)
    keep1 = (1.0 - head0.astype(F32)).astype(BF16)
    ones = jnp.ones((2 * QB, 128), BF16)
    n4 = S // 4

    for t, src in enumerate((q_ref, k_ref, v_ref)):
        def widen(c, carry, src=src):
            r0 = pl.multiple_of(c * 256, 256)
            nat32[pl.ds(r0, 256), :] = src[0, 0, pl.ds(r0, 256), :].astype(F32)
            return carry
        lax.fori_loop(0, S // 256, widen, 0)
        for r in range(4):
            def by4(c, carry, r=r, t=t):
                u0 = pl.multiple_of(c * 256, 256)
                x = nat32[pl.ds(4 * u0 + r, 256, stride=4), :]
                p4_32[pl.ds(r * n4 + u0, 256), :] = x
                p4_ref[t, pl.ds(r * n4 + u0, 256), :] = x.astype(BF16)
                return carry
            lax.fori_loop(0, n4 // 256, by4, 0)
        for r4 in range(4):
            for s in range(4):
                x = p4_32[pl.ds(r4 * n4 + s, n4 // 4, stride=4), :]
                p16_ref[t, pl.ds((4 * s + r4) * (n4 // 4), n4 // 4), :] = x.astype(BF16)

    def unit(u, pat, d, qsrc, ksrc, vsrc):
        nqb = n_units // d
        row = pl.multiple_of(u * QB, QB)
        prow = pl.multiple_of(jnp.maximum(u - 1, 0) * QB, QB)
        q2 = qsrc[pl.ds(row, QB), :]
        kk = jnp.concatenate([ksrc[pl.ds(prow, QB), :], ksrc[pl.ds(row, QB), :]], axis=0)
        vv = jnp.concatenate([vsrc[pl.ds(prow, QB), :], vsrc[pl.ds(row, QB), :]], axis=0)
        qs = jnp.concatenate([q2 * keep0, q2 * keep1], axis=0)
        s = lax.dot_general(qs, kk, NT_DIMS, preferred_element_type=F32)
        s = s + t_ref[pat, 0, jnp.minimum(u & (nqb - 1), 1)]
        m = jnp.max(s, axis=-1, keepdims=True)
        p = jnp.exp2(s - m).astype(BF16)
        oe = jnp.dot(p, jnp.concatenate([vv, ones], axis=1), preferred_element_type=F32)
        mb = jnp.broadcast_to(m, (2 * QB, 128))
        return (jnp.where(head0, oe[:QB, :128], oe[QB:, :128]),
                jnp.where(head0, oe[:QB, 128:], oe[QB:, 128:]),
                jnp.where(head0, mb[:QB], mb[QB:]))

    def strided_pattern(pat, d, src, acc_ref, den_ref, max_ref):
        nqb = n_units // d

        def body(u, carry):
            acc, den, mx = unit(u, pat, d, src.at[0], src.at[1], src.at[2])
            start = u // nqb + (u & (nqb - 1)) * (d * QB)
            acc_ref[pl.ds(start, QB, stride=d), :] = acc
            den_ref[pl.ds(start, QB, stride=d), :] = den
            max_ref[pl.ds(start, QB, stride=d), :] = mx
            return carry
        lax.fori_loop(0, n_units, body, 0, unroll=unroll)

    strided_pattern(2, 16, p16_ref, a16_ref, l16_ref, m16_ref)
    strided_pattern(1, 4, p4_ref, a4_ref, l4_ref, m4_ref)

    def dense(u, carry):
        a1, l1, m1 = unit(u, 0, 1, q_ref.at[0, 0], k_ref.at[0, 0], v_ref.at[0, 0])
        rows = pl.ds(pl.multiple_of(u * QB, QB), QB)
        m4, m16 = m4_ref[rows, :], m16_ref[rows, :]
        mx = jnp.maximum(jnp.maximum(m1, m4), m16)
        e1, e4, e16 = jnp.exp2(m1 - mx), jnp.exp2(m4 - mx), jnp.exp2(m16 - mx)
        num = a1 * e1 + a4_ref[rows, :] * e4 + a16_ref[rows, :] * e16
        den = l1 * e1 + l4_ref[rows, :] * e4 + l16_ref[rows, :] * e16
        oa_ref[0, 0, rows, :] = (num / den).astype(BF16)
        return carry
    lax.fori_loop(0, n_units, dense, 0, unroll=unroll)


def _band_attn(qkv, tables, unroll):
    B, _, S, _ = qkv.shape
    blk = (1, 1, S, 128)
    return pl.pallas_call(
        functools.partial(_band_attn_kernel, S=S, unroll=unroll),
        out_shape=jax.ShapeDtypeStruct((B, NHP, S, 128), BF16),
        grid=(B, NHP),
        in_specs=[pl.BlockSpec(blk, lambda b, h: (b, h, 0, 0)),
                  pl.BlockSpec(blk, lambda b, h: (b, NHP + h, 0, 0)),
                  pl.BlockSpec(blk, lambda b, h: (b, 2 * NHP + h, 0, 0)),
                  pl.BlockSpec((3, 1, 2, 2 * QB, 2 * QB), lambda b, h: (0, h, 0, 0, 0))],
        out_specs=pl.BlockSpec(blk, lambda b, h: (b, h, 0, 0)),
        scratch_shapes=[pltpu.VMEM((S, 128), F32), pltpu.VMEM((S, 128), F32),
                        pltpu.VMEM((3, S, 128), BF16), pltpu.VMEM((3, S, 128), BF16)]
                       + [pltpu.VMEM((S, 128), F32)] * 6,
        compiler_params=_params("arbitrary", "arbitrary"),
        name="band_attn",
    )(qkv, qkv, qkv, tables)


def _layernorm_silu(c, g, b):
    mu = jnp.mean(c, axis=-1, keepdims=True)
    cc = c - mu
    var = jnp.mean(cc * cc, axis=-1, keepdims=True)
    return _silu(cc * lax.rsqrt(var + EPS) * g + b)


FIRST_TAP = HALO - (CONV_W - 1)


def _conv_stage(glu_ref, halo_ref, seq_start, full_ref, tm):
    halo = halo_ref[0].astype(F32)
    full_ref[0, 0:HALO, :] = jnp.where(seq_start, jnp.zeros_like(halo), halo)
    full_ref[0, HALO:HALO + tm, :] = glu_ref[0].astype(F32)
    n_shift = HALO + tm - 8
    for s in range(1, 8):
        full_ref[s, 0:n_shift, :] = full_ref[0, s:s + n_shift, :]


def _conv_taps(c, full_ref, cw_ref, conv_ref, rc):
    base = c * rc
    accs = [jnp.zeros((8, EB), F32) for _ in range(rc // 8)]
    for j in range(CONV_W):
        off = FIRST_TAP + j
        w8 = cw_ref[j]
        for g in range(rc // 8):
            accs[g] = accs[g] + full_ref[off % 8, pl.ds(base + (off // 8) * 8 + 8 * g, 8), :] * w8
    conv_ref[pl.ds(base, rc), :] = jnp.concatenate(accs, axis=0)


def _conv_norm_gate(c, conv_ref, zb_ref, cb_ref, lg_ref, lb_ref, o_ref, rn):
    rows = pl.ds(c * rn, rn)
    ob = _layernorm_silu(conv_ref[rows, :] + cb_ref[...], lg_ref[...], lb_ref[...])
    o_ref[0, rows, :] = (ob * _silu(zb_ref[0, rows, :].astype(F32))).astype(BF16)


def _gated_tail(a_g, b_g, c_g, gt_ref, x, wpa_ref, wpb_ref, wpc_ref, wout_ref, gpost_ref):
    def gate(k):
        return _sigmoid(gt_ref[:, k * D_MODEL:(k + 1) * D_MODEL].astype(F32))

    mix = gate(0) * jnp.dot(a_g, wpa_ref[...], preferred_element_type=F32)
    mix = mix + gate(1) * jnp.dot(b_g, wpb_ref[...], preferred_element_type=F32)
    mix = mix + gate(2) * jnp.dot(c_g, wpc_ref[...], preferred_element_type=F32)
    z = jnp.dot(mix.astype(BF16), wout_ref[...], preferred_element_type=F32)
    return x + _rmsnorm_f32(z, gpost_ref[...])


def _tail_prompt_kernel(oa_ref, za_ref, obg_ref, qc_ref, zc_ref, mk_ref, mv_ref, gt_ref, x_ref,
                        wpa_ref, wpb_ref, wpc_ref, wout_ref, gpost_ref, y_ref):
    a_g = jnp.concatenate(
        [(oa_ref[0, c].astype(F32) * _silu(za_ref[0, :, c * 128:(c + 1) * 128].astype(F32))).astype(BF16)
         for c in range(NHP)], axis=-1)
    pieces = []
    for h in range(HC):
        hs = slice(h * DC, (h + 1) * DC)
        s = lax.dot_general(qc_ref[0, :, hs], mk_ref[0, :, hs], NT_DIMS, preferred_element_type=F32)
        p = jnp.exp(s - jnp.max(s, axis=-1, keepdims=True))
        l = jnp.sum(p, axis=-1, keepdims=True)
        oc = jnp.dot(p.astype(BF16), mv_ref[0, :, hs], preferred_element_type=F32) / l
        pieces.append((oc * _silu(zc_ref[0, :, hs].astype(F32))).astype(BF16))
    c_g = jnp.concatenate(pieces, axis=-1)
    y_ref[0] = _gated_tail(a_g, obg_ref[0], c_g, gt_ref.at[0], x_ref[0],
                           wpa_ref, wpb_ref, wpc_ref, wout_ref, gpost_ref)


def _tail_prompt(oa, za, obg, qc, zc, mk, mv, gt, x, wpa, wpb, wpc, wout, gpost, tm):
    B, S, _ = x.shape
    row = lambda b, i: (b, i, 0)
    const = lambda b, i: (0, 0)
    return pl.pallas_call(
        _tail_prompt_kernel,
        out_shape=jax.ShapeDtypeStruct((B, S, D_MODEL), F32),
        grid=(B, S // tm),
        in_specs=[
            pl.BlockSpec((1, NHP, tm, 128), lambda b, i: (b, 0, i, 0)),
            pl.BlockSpec((1, tm, EA), row), pl.BlockSpec((1, tm, EB), row),
            pl.BlockSpec((1, tm, EC), row), pl.BlockSpec((1, tm, EC), row),
            pl.BlockSpec((1, N_MEM, EC), lambda b, i: (b, 0, 0)),
            pl.BlockSpec((1, N_MEM, EC), lambda b, i: (b, 0, 0)),
            pl.BlockSpec((1, tm, 3 * D_MODEL), row), pl.BlockSpec((1, tm, D_MODEL), row),
            pl.BlockSpec((EA, D_MODEL), const), pl.BlockSpec((EB, D_MODEL), const),
            pl.BlockSpec((EC, D_MODEL), const), pl.BlockSpec((D_MODEL, D_MODEL), const),
            pl.BlockSpec((1, D_MODEL), const)],
        out_specs=pl.BlockSpec((1, tm, D_MODEL), row),
        compiler_params=_params("arbitrary", "arbitrary"),
        name="tail_prompt",
    )(oa, za, obg, qc, zc, mk, mv, gt, x, wpa, wpb, wpc, wout, gpost)


def _tail_sample_kernel(oa_ref, za_ref, ob_ref, zb_ref, oc_ref, zc_ref, gt_ref, x_ref,
                        wpa_ref, wpb_ref, wpc_ref, wout_ref, gpost_ref, y_ref):
    a_g = (oa_ref[...] * _silu(za_ref[...].astype(F32))).astype(BF16)
    b_g = (ob_ref[...] * _silu(zb_ref[...].astype(F32))).astype(BF16)
    c_g = (oc_ref[...] * _silu(zc_ref[...].astype(F32))).astype(BF16)
    y_ref[...] = _gated_tail(a_g, b_g, c_g, gt_ref, x_ref[...],
                             wpa_ref, wpb_ref, wpc_ref, wout_ref, gpost_ref)


def _tail_sample(oa, za, ob, zb, oc, zc, gt, x, wpa, wpb, wpc, wout, gpost, tm):
    M = x.shape[0]
    row = lambda i: (i, 0)
    const = lambda i: (0, 0)
    widths = (EA, EA, EB, EB, EC, EC, 3 * D_MODEL, D_MODEL)
    return pl.pallas_call(
        _tail_sample_kernel,
        out_shape=jax.ShapeDtypeStruct((M, D_MODEL), F32),
        grid=(M // tm,),
        in_specs=[pl.BlockSpec((tm, n), row) for n in widths] + [
            pl.BlockSpec((EA, D_MODEL), const), pl.BlockSpec((EB, D_MODEL), const),
            pl.BlockSpec((EC, D_MODEL), const), pl.BlockSpec((D_MODEL, D_MODEL), const),
            pl.BlockSpec((1, D_MODEL), const)],
        out_specs=pl.BlockSpec((tm, D_MODEL), row),
        compiler_params=_params("arbitrary"),
        name="tail_sample",
    )(oa, za, ob, zb, oc, zc, gt, x, wpa, wpb, wpc, wout, gpost)


def _sample_tables(rel_bias, wb, past, T):
    def mult_of(dl, real):
        m = np.zeros(dl.shape, np.float32)
        for (w, d) in PATTERNS:
            m += (real & (dl % d == 0) & (dl <= w)).astype(np.float32)
        return m

    def tables(dl, real):
        row = np.arange(QROWS)[:, None]
        real = real & (dl >= 0) & (past + row - dl >= 0) & (wb + row - dl >= 0)
        m = mult_of(dl, real)
        dlc = np.clip(dl, 0, MAX_DIST)
        m[T:] = m[0]
        bias = jnp.where(jnp.asarray(m > 0)[None], _bias_by_bucket(rel_bias, _t5_bucket_ids(dlc)), NEG)
        return bias, jnp.asarray(m)

    i = np.arange(QROWS)[:, None]
    pos = np.arange(wb)[None, :]
    cache_b, cache_m = tables(wb + i - pos, np.ones((QROWS, wb), bool))
    j = np.arange(QROWS)[None, :]
    new_b, new_m = tables(i - j, np.broadcast_to(j < T, (QROWS, QROWS)))
    return cache_b, cache_m, new_b, new_m


def _pad_rows(x, rows):
    return jnp.concatenate([x, jnp.zeros((rows - x.shape[0], x.shape[1]), x.dtype)], axis=0)


def _sample_heads(heads, qh_ref, knh_ref, vnh_ref, kt_ref, vt_ref, cb_ref, cm_ref, nb_ref, nm_ref, oh_ref):
    scores = []
    for h in heads:
        qh = qh_ref[h].astype(BF16)
        s = jnp.dot(qh, kt_ref[0, h].astype(BF16), preferred_element_type=F32) + cb_ref[h]
        sn = lax.dot_general(qh, knh_ref[h].astype(BF16), NT_DIMS, preferred_element_type=F32) + nb_ref[h]
        scores.append((s, sn))
    probs = []
    for s, sn in scores:
        m = jnp.maximum(jnp.max(s, axis=-1, keepdims=True), jnp.max(sn, axis=-1, keepdims=True))
        p = jnp.exp(s - m) * cm_ref[...]
        pn = jnp.exp(sn - m) * nm_ref[...]
        l = jnp.sum(p, axis=-1, keepdims=True) + jnp.sum(pn, axis=-1, keepdims=True)
        probs.append((p.astype(BF16), pn.astype(BF16), l))
    for h, (p, pn, l) in zip(heads, probs):
        o = lax.dot_general(p, vt_ref[0, h].astype(BF16), NT_DIMS, preferred_element_type=F32)
        o = o + jnp.dot(pn, vnh_ref[h].astype(BF16), preferred_element_type=F32)
        oh_ref[h] = o / l


def _sample_cross_attn(qc_ref, mk_ref, mv_ref, xm_ref, oc_ref, T):
    qc8 = _pad_rows(qc_ref[0], QROWS)
    qc_heads = jnp.concatenate([qc8[:, h * DC:(h + 1) * DC] for h in range(HC)], axis=0).astype(BF16)
    sc = lax.dot_general(qc_heads, mk_ref[0].astype(BF16), NT_DIMS, preferred_element_type=F32) + xm_ref[...]
    pc = jnp.exp(sc - jnp.max(sc, axis=-1, keepdims=True))
    lc = jnp.sum(pc, axis=-1, keepdims=True)
    oc = jnp.dot(pc.astype(BF16), mv_ref[0].astype(BF16), preferred_element_type=F32) / lc
    oc_ref[0] = jnp.concatenate([oc[h * QROWS:h * QROWS + T] for h in range(HC)], axis=-1)


def _conv_attn_kernel(glu_ref, halo_ref, zb_ref, cw_ref, cb_ref, lg_ref, lb_ref,
                      q_ref, kt_ref, vt_ref, kn_ref, vn_ref, tb_ref, tm_ref, nb_ref, nm_ref,
                      qc_ref, mk_ref, mv_ref, xm_ref,
                      obg_ref, oa_ref, oc_ref,
                      full_ref, conv_ref, qh_ref, knh_ref, vnh_ref, oh_ref,
                      *, tm, rc, rn, T, tiles_per_seq):
    g = pl.program_id(0)
    _conv_stage(glu_ref, halo_ref, g % tiles_per_seq == 0, full_ref, tm)
    q8, kn8, vn8 = _pad_rows(q_ref[0], QROWS), _pad_rows(kn_ref[0], QROWS), _pad_rows(vn_ref[0], QROWS)
    for h in range(HA):
        hs = slice(h * DA, (h + 1) * DA)
        qh_ref[h], knh_ref[h], vnh_ref[h] = q8[:, hs], kn8[:, hs], vn8[:, hs]
    _sample_cross_attn(qc_ref, mk_ref, mv_ref, xm_ref, oc_ref, T)
    for c in range(tm // rc):
        _conv_taps(c, full_ref, cw_ref, conv_ref, rc)
    _sample_heads(list(range(HA)), qh_ref, knh_ref, vnh_ref, kt_ref, vt_ref,
                  tb_ref, tm_ref, nb_ref, nm_ref, oh_ref)
    for c in range(tm // rn):
        _conv_norm_gate(c, conv_ref, zb_ref, cb_ref, lg_ref, lb_ref, obg_ref, rn)
    oa_ref[0] = jnp.concatenate([oh_ref[h][:T] for h in range(HA)], axis=-1)


def _conv_prompt_sample_attn(glu, zb, cw, cb, lg, lb, q, k_new, v_new, cache_kt, cache_vt, tables,
                             qc, mem_k, mem_v, tm, rc, rn):
    B, S, _ = glu.shape
    Bd, T, _ = q.shape
    wb = cache_kt.shape[-1]
    tiles_per_seq = S // tm
    assert B * tiles_per_seq == Bd and HA % (tm // rc) == 0
    cache_b, cache_m, new_b, new_m = tables
    own_head = jnp.asarray(np.where(
        np.arange(HC * QROWS)[:, None] // QROWS == np.arange(N_MEM * HC)[None, :] % HC, 0.0, NEG
    ).astype(np.float32))
    tile = lambda g: (g // tiles_per_seq, g % tiles_per_seq, 0)
    halo = lambda g: (g // tiles_per_seq, jnp.maximum((g % tiles_per_seq) * (tm // HALO) - 1, 0), 0)
    per_b3 = lambda g: (g, 0, 0)
    per_b4 = lambda g: (g, 0, 0, 0)
    c2 = lambda g: (0, 0)
    c3 = lambda g: (0, 0, 0)
    head_scratch = pltpu.VMEM((HA, QROWS, DA), F32)
    return pl.pallas_call(
        functools.partial(_conv_attn_kernel, tm=tm, rc=rc, rn=rn, T=T, tiles_per_seq=tiles_per_seq),
        out_shape=(jax.ShapeDtypeStruct((B, S, EB), BF16),
                   jax.ShapeDtypeStruct((Bd, T, EA), F32), jax.ShapeDtypeStruct((Bd, T, EC), F32)),
        grid=(Bd,),
        in_specs=[pl.BlockSpec((1, tm, EB), tile), pl.BlockSpec((1, HALO, EB), halo),
                  pl.BlockSpec((1, tm, EB), tile),
                  pl.BlockSpec((CONV_W, 8, EB), c3),
                  pl.BlockSpec((1, EB), c2), pl.BlockSpec((1, EB), c2), pl.BlockSpec((1, EB), c2),
                  pl.BlockSpec((1, T, EA), per_b3),
                  pl.BlockSpec((1, HA, DA, wb), per_b4), pl.BlockSpec((1, HA, DA, wb), per_b4),
                  pl.BlockSpec((1, T, EA), per_b3), pl.BlockSpec((1, T, EA), per_b3),
                  pl.BlockSpec((HA, QROWS, wb), c3), pl.BlockSpec((QROWS, wb), c2),
                  pl.BlockSpec((HA, QROWS, QROWS), c3), pl.BlockSpec((QROWS, QROWS), c2),
                  pl.BlockSpec((1, T, EC), per_b3),
                  pl.BlockSpec((1, N_MEM * HC, DC), per_b3), pl.BlockSpec((1, N_MEM * HC, DC), per_b3),
                  pl.BlockSpec((HC * QROWS, N_MEM * HC), c2)],
        out_specs=(pl.BlockSpec((1, tm, EB), tile),
                   pl.BlockSpec((1, T, EA), per_b3), pl.BlockSpec((1, T, EC), per_b3)),
        scratch_shapes=[pltpu.VMEM((8, HALO + tm, EB), F32), pltpu.VMEM((tm, EB), F32),
                        head_scratch, head_scratch, head_scratch, head_scratch],
        compiler_params=_params("arbitrary"),
        name="conv_prompt_sample_attn",
    )(glu, glu, zb, cw, cb, lg, lb, q, cache_kt, cache_vt, k_new, v_new,
      cache_b, cache_m, new_b, new_m, qc, mem_k, mem_v, own_head)


def _conv_sample_kernel(st_ref, glu_ref, cw_ref, cb_ref, lg_ref, lb_ref, o_ref, *, T, rc):
    n_hist = CONV_W - 1
    Bd = st_ref.shape[1]
    for i in range(T):
        def chunk(c, carry, i=i):
            rows = pl.ds(pl.multiple_of(c * rc, rc), rc)
            acc = jnp.zeros((rc, EB), F32)
            for j in range(CONV_W):
                src = st_ref[i + j, rows, :] if i + j < n_hist else glu_ref[i + j - n_hist, rows, :]
                acc = acc + src * cw_ref[j:j + 1, :]
            o_ref[i, rows, :] = _layernorm_silu(acc + cb_ref[...], lg_ref[...], lb_ref[...])
            return carry
        lax.fori_loop(0, Bd // rc, chunk, 0)


def _conv_sample(state_t, glu_t, cw, cb, lg, lb, rc):
    T, Bd, _ = glu_t.shape
    return pl.pallas_call(
        functools.partial(_conv_sample_kernel, T=T, rc=rc),
        out_shape=jax.ShapeDtypeStruct((T, Bd, EB), F32),
        compiler_params=pltpu.CompilerParams(vmem_limit_bytes=V7X_VMEM_LIMIT),
        name="conv_sample",
    )(state_t, glu_t, cw, cb, lg, lb)


def kernel(x_prompt, x_sample, mem_prompt, cache_k_win, cache_v_win, state_conv, cache_k_mem, cache_v_mem,
           rel_bias, g_pre, w_in, g_mem, w_mem_kv, conv_w, conv_b, ln_g, ln_b, w_proj_a, w_proj_b,
           w_proj_c, w_out, g_post):
    depth = g_pre.shape[0]
    assert depth == 1, "single-layer step"
    B, S, _ = x_prompt.shape
    Bd, T, _ = x_sample.shape
    wb = cache_k_win.shape[2]
    past = wb
    assert wb == MAX_DIST and S % (16 * QB) == 0 and T <= QROWS

    l = 0
    w_in_b = w_in[l].astype(BF16)
    wpa, wpb, wpc = w_proj_a[l].astype(BF16), w_proj_b[l].astype(BF16), w_proj_c[l].astype(BF16)
    wout = w_out[l].astype(BF16)
    gpre, gpost = g_pre[l][None], g_post[l][None]
    cb, lg, lb = conv_b[l][None], ln_g[l][None], ln_b[l][None]
    cw = conv_w[l]

    (qkv, k_keep, v_keep, za, glu, zb, qc, zc, gt, glu_tail) = _inproj_prompt(x_prompt, gpre, w_in_b, tm=256)
    mk_f, mv_f, mk_b, mv_b = _memkv(mem_prompt.reshape(B * N_MEM, D_MODEL), g_mem[l][None],
                                    w_mem_kv[l].astype(BF16), tm=512)
    oa = _band_attn(qkv, _band_tables(rel_bias), unroll=32)

    (q_s, k_s, v_s, za_s, glu_s, zb_s, qc_s, zc_s, gt_s) = _inproj_sample(
        x_sample.reshape(Bd * T, D_MODEL), gpre, w_in_b, tm=Bd * T)
    cache_kt = jnp.transpose(cache_k_win[l], (0, 2, 3, 1))
    cache_vt = jnp.transpose(cache_v_win[l], (0, 2, 3, 1))
    cw_tiles = jnp.broadcast_to(cw[:, None, :], (CONV_W, 8, EB))
    obg, oa_s, oc_s = _conv_prompt_sample_attn(
        glu, zb, cw_tiles, cb, lg, lb,
        q_s.reshape(Bd, T, EA), k_s.reshape(Bd, T, EA), v_s.reshape(Bd, T, EA),
        cache_kt, cache_vt, _sample_tables(rel_bias, wb, past, T),
        qc_s.reshape(Bd, T, EC), cache_k_mem[l].reshape(Bd, N_MEM * HC, DC),
        cache_v_mem[l].reshape(Bd, N_MEM * HC, DC), tm=(B * S) // Bd, rc=32, rn=64)
    y_p = _tail_prompt(oa, za, obg, qc, zc, mk_b.reshape(B, N_MEM, EC), mv_b.reshape(B, N_MEM, EC),
                       gt, x_prompt, wpa, wpb, wpc, wout, gpost, tm=256)
    state_t = jnp.transpose(state_conv[l], (1, 0, 2))
    glu_t = jnp.transpose(glu_s.reshape(Bd, T, EB), (1, 0, 2))
    ob_t = _conv_sample(state_t, glu_t, cw, cb, lg, lb, rc=32)
    ob_s = jnp.transpose(ob_t, (1, 0, 2)).reshape(Bd * T, EB)
    y_s = _tail_sample(oa_s.reshape(Bd * T, EA), za_s, ob_s, zb_s,
                       oc_s.reshape(Bd * T, EC), zc_s, gt_s, x_sample.reshape(Bd * T, D_MODEL),
                       wpa, wpb, wpc, wout, gpost, tm=Bd * T)

    conv_state_s = jnp.transpose(jnp.concatenate([state_t[T:], glu_t], axis=0), (1, 0, 2))
    return (y_p, y_s.reshape(Bd, T, D_MODEL),
            k_keep.reshape(1, B, wb, HA, DA), v_keep.reshape(1, B, wb, HA, DA),
            glu_tail[:, HALO - (CONV_W - 1):][None],
            mk_f.reshape(1, B, N_MEM, HC, DC), mv_f.reshape(1, B, N_MEM, HC, DC),
            k_s.reshape(1, Bd, T, HA, DA), v_s.reshape(1, Bd, T, HA, DA),
            conv_state_s[None])
```

```python
import functools

import jax
import jax.numpy as jnp
import numpy as np
from jax import lax
from jax.experimental import pallas as pl
from jax.experimental.pallas import tpu as pltpu

F32 = jnp.float32
BF16 = jnp.bfloat16

D_MODEL = 1024
HA, DA = 12, 64
EA = HA * DA
NHP = HA // 2
PATTERNS = ((128, 1), (512, 4), (2048, 16))
NW = 128
QB = 128
EB = 768
CONV_W = 31
HC, DC = 4, 128
EC = HC * DC
N_MEM = 256
N_BUCKETS = 32
MAX_DIST = 2048
EPS = 1e-6
NEG = -1e30
LOG2E = 1.4426950408889634

C_Q, C_K, C_V, C_ZA = 0, EA, 2 * EA, 3 * EA
C_U = 4 * EA
C_G = C_U + EB
C_ZB = C_U + 2 * EB
C_QC = C_ZB + EB
C_ZC = C_QC + EC
C_GT = C_ZC + EC
IN_COLS = C_GT + 3 * D_MODEL

V7X_VMEM_LIMIT = 56 * 1024 * 1024
HALO = 32
QROWS = 8
NT_DIMS = (((1,), (1,)), ((), ()))


def _t5_bucket_ids(n):
    exact = N_BUCKETS // 2
    nf = np.maximum(n, 1).astype(np.float32)
    scale = np.float32(N_BUCKETS - exact) / np.log(np.float32(MAX_DIST) / np.float32(exact))
    large = exact + (np.log(nf / np.float32(exact)) * scale).astype(np.int32)
    large = np.minimum(large, N_BUCKETS - 1)
    return np.where(n < exact, n, large).astype(np.int32)


def _bias_by_bucket(rel_bias, ids):
    onehot = (ids[None] == np.arange(N_BUCKETS).reshape((-1,) + (1,) * ids.ndim)).astype(np.float32)
    return jnp.tensordot(rel_bias.astype(F32).T, jnp.asarray(onehot), axes=1,
                         precision=lax.Precision.HIGHEST)


def _sigmoid(x):
    return 1.0 / (1.0 + jnp.exp(-x))


def _silu(x):
    return x * _sigmoid(x)


def _rmsnorm_f32(x, g):
    return x * lax.rsqrt(jnp.mean(x * x, axis=-1, keepdims=True) + EPS) * g


def _params(*sem):
    return pltpu.CompilerParams(dimension_semantics=sem, vmem_limit_bytes=V7X_VMEM_LIMIT)


def _inproj_prompt_kernel(x_ref, g_ref, w_ref, qkv_ref, kf_ref, vf_ref, za_ref, glu_ref, zb_ref,
                          qc_ref, zc_ref, gt_ref, tail_ref):
    xn = _rmsnorm_f32(x_ref[0], g_ref[...]).astype(BF16)

    def mm(lo, n):
        return jnp.dot(xn, w_ref[:, lo:lo + n], preferred_element_type=F32)

    def put_pairs(res, base):
        for c in range(NHP):
            qkv_ref[0, base + c] = res[:, c * 128:(c + 1) * 128].astype(BF16)

    put_pairs(mm(C_Q, EA) * (DA ** -0.5 * LOG2E), 0)
    k = mm(C_K, EA)
    put_pairs(k, NHP)
    kf_ref[0] = k
    v = mm(C_V, EA)
    put_pairs(v, 2 * NHP)
    vf_ref[0] = v
    za_ref[0] = mm(C_ZA, EA).astype(BF16)
    glu = mm(C_U, EB) * _sigmoid(mm(C_G, EB))
    glu_ref[0] = glu.astype(BF16)
    tail_ref[0] = glu[glu.shape[0] - HALO:, :]
    zb_ref[0] = mm(C_ZB, EB).astype(BF16)
    qc_ref[0] = (mm(C_QC, EC) * (DC ** -0.5)).astype(BF16)
    zc_ref[0] = mm(C_ZC, EC).astype(BF16)
    for c in range(3):
        gt_ref[0, :, c * D_MODEL:(c + 1) * D_MODEL] = mm(C_GT + c * D_MODEL, D_MODEL).astype(BF16)


def _inproj_prompt(x, g, w, tm):
    B, S, _ = x.shape
    nt = S // tm
    wb = min(MAX_DIST, S)
    first_kept = (S - wb) // tm
    row = lambda b, i: (b, i, 0)
    kept = lambda b, i: (b, jnp.maximum(i - first_kept, 0), 0)
    out_shape = (
        jax.ShapeDtypeStruct((B, 3 * NHP, S, 128), BF16),
        jax.ShapeDtypeStruct((B, wb, EA), F32),
        jax.ShapeDtypeStruct((B, wb, EA), F32),
        jax.ShapeDtypeStruct((B, S, EA), BF16),
        jax.ShapeDtypeStruct((B, S, EB), BF16),
        jax.ShapeDtypeStruct((B, S, EB), BF16),
        jax.ShapeDtypeStruct((B, S, EC), BF16),
        jax.ShapeDtypeStruct((B, S, EC), BF16),
        jax.ShapeDtypeStruct((B, S, 3 * D_MODEL), BF16),
        jax.ShapeDtypeStruct((B, HALO, EB), F32),
    )
    out_specs = (
        pl.BlockSpec((1, 3 * NHP, tm, 128), lambda b, i: (b, 0, i, 0)),
        pl.BlockSpec((1, tm, EA), kept),
        pl.BlockSpec((1, tm, EA), kept),
        pl.BlockSpec((1, tm, EA), row),
        pl.BlockSpec((1, tm, EB), row),
        pl.BlockSpec((1, tm, EB), row),
        pl.BlockSpec((1, tm, EC), row),
        pl.BlockSpec((1, tm, EC), row),
        pl.BlockSpec((1, tm, 3 * D_MODEL), row),
        pl.BlockSpec((1, HALO, EB), lambda b, i: (b, 0, 0)),
    )
    return pl.pallas_call(
        _inproj_prompt_kernel,
        out_shape=out_shape,
        grid=(B, nt),
        in_specs=[
            pl.BlockSpec((1, tm, D_MODEL), row),
            pl.BlockSpec((1, D_MODEL), lambda b, i: (0, 0)),
            pl.BlockSpec((D_MODEL, IN_COLS), lambda b, i: (0, 0), pipeline_mode=pl.Buffered(1)),
        ],
        out_specs=out_specs,
        compiler_params=_params("arbitrary", "arbitrary"),
        name="inproj_prompt",
    )(x, g, w)


def _inproj_sample_kernel(x_ref, g_ref, w_ref, q_ref, k_ref, v_ref, za_ref, glu_ref, zb_ref,
                          qc_ref, zc_ref, gt_ref):
    xn = _rmsnorm_f32(x_ref[...], g_ref[...]).astype(BF16)

    def mm(lo, n):
        return jnp.dot(xn, w_ref[:, lo:lo + n], preferred_element_type=F32)

    q_ref[...] = mm(C_Q, EA) * (DA ** -0.5)
    k_ref[...] = mm(C_K, EA)
    v_ref[...] = mm(C_V, EA)
    za_ref[...] = mm(C_ZA, EA).astype(BF16)
    glu_ref[...] = mm(C_U, EB) * _sigmoid(mm(C_G, EB))
    zb_ref[...] = mm(C_ZB, EB).astype(BF16)
    qc_ref[...] = mm(C_QC, EC) * (DC ** -0.5)
    zc_ref[...] = mm(C_ZC, EC).astype(BF16)
    for c in range(3):
        gt_ref[:, c * D_MODEL:(c + 1) * D_MODEL] = mm(C_GT + c * D_MODEL, D_MODEL).astype(BF16)


def _inproj_sample(x, g, w, tm):
    M = x.shape[0]
    row = lambda i: (i, 0)
    widths = (EA, EA, EA, EA, EB, EB, EC, EC, 3 * D_MODEL)
    dtypes = (F32, F32, F32, BF16, F32, BF16, F32, BF16, BF16)
    return pl.pallas_call(
        _inproj_sample_kernel,
        out_shape=tuple(jax.ShapeDtypeStruct((M, n), dt) for n, dt in zip(widths, dtypes)),
        grid=(M // tm,),
        in_specs=[
            pl.BlockSpec((tm, D_MODEL), row),
            pl.BlockSpec((1, D_MODEL), lambda i: (0, 0)),
            pl.BlockSpec((D_MODEL, IN_COLS), lambda i: (0, 0), pipeline_mode=pl.Buffered(1)),
        ],
        out_specs=tuple(pl.BlockSpec((tm, n), row) for n in widths),
        compiler_params=_params("arbitrary"),
        name="inproj_sample",
    )(x, g, w)


def _memkv_kernel(m_ref, g_ref, w_ref, kf_ref, vf_ref, kb_ref, vb_ref):
    xn = _rmsnorm_f32(m_ref[...], g_ref[...]).astype(BF16)
    k = jnp.dot(xn, w_ref[:, :EC], preferred_element_type=F32)
    v = jnp.dot(xn, w_ref[:, EC:], preferred_element_type=F32)
    kf_ref[...] = k
    vf_ref[...] = v
    kb_ref[...] = k.astype(BF16)
    vb_ref[...] = v.astype(BF16)


def _memkv(mem, g, w, tm):
    M = mem.shape[0]
    row = lambda i: (i, 0)
    return pl.pallas_call(
        _memkv_kernel,
        out_shape=(jax.ShapeDtypeStruct((M, EC), F32), jax.ShapeDtypeStruct((M, EC), F32),
                   jax.ShapeDtypeStruct((M, EC), BF16), jax.ShapeDtypeStruct((M, EC), BF16)),
        grid=(M // tm,),
        in_specs=[pl.BlockSpec((tm, D_MODEL), row),
                  pl.BlockSpec((1, D_MODEL), lambda i: (0, 0)),
                  pl.BlockSpec((D_MODEL, 2 * EC), lambda i: (0, 0))],
        out_specs=tuple(pl.BlockSpec((tm, EC), row) for _ in range(4)),
        compiler_params=_params("arbitrary"),
        name="memkv",
    )(mem, g, w)


def _band_tables(rel_bias):
    iq = np.arange(QB)[:, None]
    ik = np.arange(2 * QB)[None, :]
    dist = iq - ik + QB
    band = (dist >= 0) & (dist <= NW)
    out = []
    for (_, d) in PATTERNS:
        bias = LOG2E * _bias_by_bucket(rel_bias, _t5_bucket_ids(np.clip(dist, 0, NW) * d))
        later = jnp.where(jnp.asarray(band)[None], bias, NEG)
        first = jnp.where(jnp.asarray(band & (ik >= QB))[None], bias, NEG)
        t = jnp.stack([first, later], axis=1)
        out.append(t.reshape(NHP, 2, 2, QB, 2 * QB).transpose(0, 2, 1, 3, 4)
                   .reshape(NHP, 2, 2 * QB, 2 * QB))
    return jnp.stack(out)


def _band_attn_kernel(q_ref, k_ref, v_ref, t_ref, oa_ref,
                      nat32, p4_32, p4_ref, p16_ref, a4_ref, l4_ref, m4_ref, a16_ref, l16_ref, m16_ref,
                      *, S, unroll):
    n_units = S // QB
    lane = lax.broadcasted_iota(jnp.int32, (QB, 128), 1)
    head0 = lane < DA
    keep0 = head0.astype(F32).astype(BF16)
    keep1 = (1.0 - head0.astype(F32)).astype(BF16)
    ones = jnp.ones((2 * QB, 128), BF16)
    n4 = S // 4

    for t, src in enumerate((q_ref, k_ref, v_ref)):
        def widen(c, carry, src=src):
            r0 = pl.multiple_of(c * 256, 256)
            nat32[pl.ds(r0, 256), :] = src[0, 0, pl.ds(r0, 256), :].astype(F32)
            return carry
        lax.fori_loop(0, S // 256, widen, 0)
        for r in range(4):
            def by4(c, carry, r=r, t=t):
                u0 = pl.multiple_of(c * 256, 256)
                x = nat32[pl.ds(4 * u0 + r, 256, stride=4), :]
                p4_32[pl.ds(r * n4 + u0, 256), :] = x
                p4_ref[t, pl.ds(r * n4 + u0, 256), :] = x.astype(BF16)
                return carry
            lax.fori_loop(0, n4 // 256, by4, 0)
        for r4 in range(4):
            for s in range(4):
                x = p4_32[pl.ds(r4 * n4 + s, n4 // 4, stride=4), :]
                p16_ref[t, pl.ds((4 * s + r4) * (n4 // 4), n4 // 4), :] = x.astype(BF16)

    def unit(u, pat, d, qsrc, ksrc, vsrc):
        nqb = n_units // d
        row = pl.multiple_of(u * QB, QB)
        prow = pl.multiple_of(jnp.maximum(u - 1, 0) * QB, QB)
        q2 = qsrc[pl.ds(row, QB), :]
        kk = jnp.concatenate([ksrc[pl.ds(prow, QB), :], ksrc[pl.ds(row, QB), :]], axis=0)
        vv = jnp.concatenate([vsrc[pl.ds(prow, QB), :], vsrc[pl.ds(row, QB), :]], axis=0)
        qs = jnp.concatenate([q2 * keep0, q2 * keep1], axis=0)
        s = lax.dot_general(qs, kk, NT_DIMS, preferred_element_type=F32)
        s = s + t_ref[pat, 0, jnp.minimum(u & (nqb - 1), 1)]
        m = jnp.max(s, axis=-1, keepdims=True)
        p = jnp.exp2(s - m).astype(BF16)
        oe = jnp.dot(p, jnp.concatenate([vv, ones], axis=1), preferred_element_type=F32)
        mb = jnp.broadcast_to(m, (2 * QB, 128))
        return (jnp.where(head0, oe[:QB, :128], oe[QB:, :128]),
                jnp.where(head0, oe[:QB, 128:], oe[QB:, 128:]),
                jnp.where(head0, mb[:QB], mb[QB:]))

    def strided_pattern(pat, d, src, acc_ref, den_ref, max_ref):
        nqb = n_units // d

        def body(u, carry):
            acc, den, mx = unit(u, pat, d, src.at[0], src.at[1], src.at[2])
            start = u // nqb + (u & (nqb - 1)) * (d * QB)
            acc_ref[pl.ds(start, QB, stride=d), :] = acc
            den_ref[pl.ds(start, QB, stride=d), :] = den
            max_ref[pl.ds(start, QB, stride=d), :] = mx
            return carry
        lax.fori_loop(0, n_units, body, 0, unroll=unroll)

    strided_pattern(2, 16, p16_ref, a16_ref, l16_ref, m16_ref)
    strided_pattern(1, 4, p4_ref, a4_ref, l4_ref, m4_ref)

    def dense(u, carry):
        a1, l1, m1 = unit(u, 0, 1, q_ref.at[0, 0], k_ref.at[0, 0], v_ref.at[0, 0])
        rows = pl.ds(pl.multiple_of(u * QB, QB), QB)
        m4, m16 = m4_ref[rows, :], m16_ref[rows, :]
        mx = jnp.maximum(jnp.maximum(m1, m4), m16)
        e1, e4, e16 = jnp.exp2(m1 - mx), jnp.exp2(m4 - mx), jnp.exp2(m16 - mx)
        num = a1 * e1 + a4_ref[rows, :] * e4 + a16_ref[rows, :] * e16
        den = l1 * e1 + l4_ref[rows, :] * e4 + l16_ref[rows, :] * e16
        oa_ref[0, 0, rows, :] = (num / den).astype(BF16)
        return carry
    lax.fori_loop(0, n_units, dense, 0, unroll=unroll)


def _band_attn(qkv, tables, unroll):
    B, _, S, _ = qkv.shape
    blk = (1, 1, S, 128)
    return pl.pallas_call(
        functools.partial(_band_attn_kernel, S=S, unroll=unroll),
        out_shape=jax.ShapeDtypeStruct((B, NHP, S, 128), BF16),
        grid=(B, NHP),
        in_specs=[pl.BlockSpec(blk, lambda b, h: (b, h, 0, 0)),
                  pl.BlockSpec(blk, lambda b, h: (b, NHP + h, 0, 0)),
                  pl.BlockSpec(blk, lambda b, h: (b, 2 * NHP + h, 0, 0)),
                  pl.BlockSpec((3, 1, 2, 2 * QB, 2 * QB), lambda b, h: (0, h, 0, 0, 0))],
        out_specs=pl.BlockSpec(blk, lambda b, h: (b, h, 0, 0)),
        scratch_shapes=[pltpu.VMEM((S, 128), F32), pltpu.VMEM((S, 128), F32),
                        pltpu.VMEM((3, S, 128), BF16), pltpu.VMEM((3, S, 128), BF16)]
                       + [pltpu.VMEM((S, 128), F32)] * 6,
        compiler_params=_params("arbitrary", "arbitrary"),
        name="band_attn",
    )(qkv, qkv, qkv, tables)


def _layernorm_silu(c, g, b):
    mu = jnp.mean(c, axis=-1, keepdims=True)
    cc = c - mu
    var = jnp.mean(cc * cc, axis=-1, keepdims=True)
    return _silu(cc * lax.rsqrt(var + EPS) * g + b)


FIRST_TAP = HALO - (CONV_W - 1)


def _conv_stage(glu_ref, halo_ref, seq_start, full_ref, tm):
    halo = halo_ref[0].astype(F32)
    full_ref[0, 0:HALO, :] = jnp.where(seq_start, jnp.zeros_like(halo), halo)
    full_ref[0, HALO:HALO + tm, :] = glu_ref[0].astype(F32)
    n_shift = HALO + tm - 8
    for s in range(1, 8):
        full_ref[s, 0:n_shift, :] = full_ref[0, s:s + n_shift, :]


def _conv_taps(c, full_ref, cw_ref, conv_ref, rc):
    base = c * rc
    accs = [jnp.zeros((8, EB), F32) for _ in range(rc // 8)]
    for j in range(CONV_W):
        off = FIRST_TAP + j
        w8 = cw_ref[j]
        for g in range(rc // 8):
            accs[g] = accs[g] + full_ref[off % 8, pl.ds(base + (off // 8) * 8 + 8 * g, 8), :] * w8
    conv_ref[pl.ds(base, rc), :] = jnp.concatenate(accs, axis=0)


def _conv_norm_gate(c, conv_ref, zb_ref, cb_ref, lg_ref, lb_ref, o_ref, rn):
    rows = pl.ds(c * rn, rn)
    ob = _layernorm_silu(conv_ref[rows, :] + cb_ref[...], lg_ref[...], lb_ref[...])
    o_ref[rows, :] = (ob * _silu(zb_ref[0, rows, :].astype(F32))).astype(BF16)


def _gated_tail(a_g, b_g, c_g, gt_ref, x, wpa_ref, wpb_ref, wpc_ref, wout_ref, gpost_ref):
    def gate(k):
        return _sigmoid(gt_ref[:, k * D_MODEL:(k + 1) * D_MODEL].astype(F32))

    mix = gate(0) * jnp.dot(a_g, wpa_ref[...], preferred_element_type=F32)
    mix = mix + gate(1) * jnp.dot(b_g, wpb_ref[...], preferred_element_type=F32)
    mix = mix + gate(2) * jnp.dot(c_g, wpc_ref[...], preferred_element_type=F32)
    z = jnp.dot(mix.astype(BF16), wout_ref[...], preferred_element_type=F32)
    return x + _rmsnorm_f32(z, gpost_ref[...])


def _tail_prompt_kernel(oa_ref, za_ref, obg_ref, qc_ref, zc_ref, mk_ref, mv_ref, gt_ref, x_ref,
                        wpa_ref, wpb_ref, wpc_ref, wout_ref, gpost_ref, y_ref, *, groups):
    tm = x_ref.shape[1]
    heads = [slice(h * DC, (h + 1) * DC) for h in range(HC)]
    rows = [pl.ds(r * (tm // groups), tm // groups) for r in range(groups)]

    def gate(r, k):
        return _sigmoid(gt_ref[0, r, k * D_MODEL:(k + 1) * D_MODEL].astype(F32))

    pb = [jnp.dot(obg_ref[0, r, :], wpb_ref[...], preferred_element_type=F32) for r in rows]
    scores = [[lax.dot_general(qc_ref[0, r, hs], mk_ref[0, :, hs], NT_DIMS, preferred_element_type=F32)
               for hs in heads] for r in rows]
    mix = []
    for r, pb_r in zip(rows, pb):
        a_g = jnp.concatenate(
            [(oa_ref[0, c, r, :].astype(F32) * _silu(za_ref[0, r, c * 128:(c + 1) * 128].astype(F32))).astype(BF16)
             for c in range(NHP)], axis=-1)
        mix.append(gate(r, 0) * jnp.dot(a_g, wpa_ref[...], preferred_element_type=F32) + gate(r, 1) * pb_r)
    c_g = []
    for r, sc in zip(rows, scores):
        pieces = []
        for s, hs in zip(sc, heads):
            p = jnp.exp(s - jnp.max(s, axis=-1, keepdims=True))
            l = jnp.sum(p, axis=-1, keepdims=True)
            oc = jnp.dot(p.astype(BF16), mv_ref[0, :, hs], preferred_element_type=F32) / l
            pieces.append((oc * _silu(zc_ref[0, r, hs].astype(F32))).astype(BF16))
        c_g.append(jnp.concatenate(pieces, axis=-1))
    mix = [m + gate(r, 2) * jnp.dot(c, wpc_ref[...], preferred_element_type=F32)
           for r, m, c in zip(rows, mix, c_g)]
    z = [jnp.dot(m.astype(BF16), wout_ref[...], preferred_element_type=F32) for m in mix]
    for r, z_r in zip(rows, z):
        y_ref[0, r, :] = x_ref[0, r, :] + _rmsnorm_f32(z_r, gpost_ref[...])


def _tail_prompt(oa, za, obg, qc, zc, mk, mv, gt, x, wpa, wpb, wpc, wout, gpost, tm, groups):
    B, S, _ = x.shape
    row = lambda b, i: (b, i, 0)
    const = lambda b, i: (0, 0)
    return pl.pallas_call(
        functools.partial(_tail_prompt_kernel, groups=groups),
        out_shape=jax.ShapeDtypeStruct((B, S, D_MODEL), F32),
        grid=(B, S // tm),
        in_specs=[
            pl.BlockSpec((1, NHP, tm, 128), lambda b, i: (b, 0, i, 0)),
            pl.BlockSpec((1, tm, EA), row), pl.BlockSpec((1, tm, EB), row),
            pl.BlockSpec((1, tm, EC), row), pl.BlockSpec((1, tm, EC), row),
            pl.BlockSpec((1, N_MEM, EC), lambda b, i: (b, 0, 0)),
            pl.BlockSpec((1, N_MEM, EC), lambda b, i: (b, 0, 0)),
            pl.BlockSpec((1, tm, 3 * D_MODEL), row), pl.BlockSpec((1, tm, D_MODEL), row),
            pl.BlockSpec((EA, D_MODEL), const, pipeline_mode=pl.Buffered(1)),
            pl.BlockSpec((EB, D_MODEL), const, pipeline_mode=pl.Buffered(1)),
            pl.BlockSpec((EC, D_MODEL), const, pipeline_mode=pl.Buffered(1)),
            pl.BlockSpec((D_MODEL, D_MODEL), const, pipeline_mode=pl.Buffered(1)),
            pl.BlockSpec((1, D_MODEL), const)],
        out_specs=pl.BlockSpec((1, tm, D_MODEL), row),
        compiler_params=_params("arbitrary", "arbitrary"),
        name="tail_prompt",
    )(oa, za, obg, qc, zc, mk, mv, gt, x, wpa, wpb, wpc, wout, gpost)


def _tail_sample_kernel(oa_ref, za_ref, ob_ref, zb_ref, oc_ref, zc_ref, gt_ref, x_ref,
                        wpa_ref, wpb_ref, wpc_ref, wout_ref, gpost_ref, y_ref):
    a_g = (oa_ref[...] * _silu(za_ref[...].astype(F32))).astype(BF16)
    b_g = (ob_ref[...] * _silu(zb_ref[...].astype(F32))).astype(BF16)
    c_g = (oc_ref[...] * _silu(zc_ref[...].astype(F32))).astype(BF16)
    y_ref[...] = _gated_tail(a_g, b_g, c_g, gt_ref, x_ref[...],
                             wpa_ref, wpb_ref, wpc_ref, wout_ref, gpost_ref)


def _tail_sample(oa, za, ob, zb, oc, zc, gt, x, wpa, wpb, wpc, wout, gpost, tm):
    M = x.shape[0]
    row = lambda i: (i, 0)
    const = lambda i: (0, 0)
    widths = (EA, EA, EB, EB, EC, EC, 3 * D_MODEL, D_MODEL)
    return pl.pallas_call(
        _tail_sample_kernel,
        out_shape=jax.ShapeDtypeStruct((M, D_MODEL), F32),
        grid=(M // tm,),
        in_specs=[pl.BlockSpec((tm, n), row) for n in widths] + [
            pl.BlockSpec((EA, D_MODEL), const), pl.BlockSpec((EB, D_MODEL), const),
            pl.BlockSpec((EC, D_MODEL), const), pl.BlockSpec((D_MODEL, D_MODEL), const),
            pl.BlockSpec((1, D_MODEL), const)],
        out_specs=pl.BlockSpec((tm, D_MODEL), row),
        compiler_params=_params("arbitrary"),
        name="tail_sample",
    )(oa, za, ob, zb, oc, zc, gt, x, wpa, wpb, wpc, wout, gpost)


def _sample_tables(rel_bias, wb, past, T):
    def mult_of(dl, real):
        m = np.zeros(dl.shape, np.float32)
        for (w, d) in PATTERNS:
            m += (real & (dl % d == 0) & (dl <= w)).astype(np.float32)
        return m

    def tables(dl, real):
        row = np.arange(QROWS)[:, None]
        real = real & (dl >= 0) & (past + row - dl >= 0) & (wb + row - dl >= 0)
        m = mult_of(dl, real)
        dlc = np.clip(dl, 0, MAX_DIST)
        m[T:] = m[0]
        bias = jnp.where(jnp.asarray(m > 0)[None], _bias_by_bucket(rel_bias, _t5_bucket_ids(dlc)), NEG)
        return bias, jnp.asarray(m)

    i = np.arange(QROWS)[:, None]
    pos = np.arange(wb)[None, :]
    cache_b, cache_m = tables(wb + i - pos, np.ones((QROWS, wb), bool))
    j = np.arange(QROWS)[None, :]
    new_b, new_m = tables(i - j, np.broadcast_to(j < T, (QROWS, QROWS)))
    return cache_b, cache_m, new_b, new_m


def _pad_rows(x, rows):
    return jnp.concatenate([x, jnp.zeros((rows - x.shape[0], x.shape[1]), x.dtype)], axis=0)


def _sample_heads(heads, qh_ref, knh_ref, vnh_ref, kt_ref, vt_ref, cb_ref, cm_ref, nb_ref, nm_ref, oh_ref):
    scores = []
    for h in heads:
        qh = qh_ref[h].astype(BF16)
        s = jnp.dot(qh, kt_ref[0, h].astype(BF16), preferred_element_type=F32) + cb_ref[h]
        sn = lax.dot_general(qh, knh_ref[h].astype(BF16), NT_DIMS, preferred_element_type=F32) + nb_ref[h]
        scores.append((s, sn))
    probs = []
    for s, sn in scores:
        m = jnp.maximum(jnp.max(s, axis=-1, keepdims=True), jnp.max(sn, axis=-1, keepdims=True))
        p = jnp.exp(s - m) * cm_ref[...]
        pn = jnp.exp(sn - m) * nm_ref[...]
        l = jnp.sum(p, axis=-1, keepdims=True) + jnp.sum(pn, axis=-1, keepdims=True)
        probs.append((p.astype(BF16), pn.astype(BF16), l))
    for h, (p, pn, l) in zip(heads, probs):
        o = lax.dot_general(p, vt_ref[0, h].astype(BF16), NT_DIMS, preferred_element_type=F32)
        o = o + jnp.dot(pn, vnh_ref[h].astype(BF16), preferred_element_type=F32)
        oh_ref[h] = o / l


def _sample_cross_attn(qc_ref, mk_ref, mv_ref, xm_ref, oc_ref, T):
    qc8 = _pad_rows(qc_ref[0], QROWS)
    qc_heads = jnp.concatenate([qc8[:, h * DC:(h + 1) * DC] for h in range(HC)], axis=0).astype(BF16)
    sc = lax.dot_general(qc_heads, mk_ref[0].astype(BF16), NT_DIMS, preferred_element_type=F32) + xm_ref[...]
    pc = jnp.exp(sc - jnp.max(sc, axis=-1, keepdims=True))
    lc = jnp.sum(pc, axis=-1, keepdims=True)
    oc = jnp.dot(pc.astype(BF16), mv_ref[0].astype(BF16), preferred_element_type=F32) / lc
    oc_ref[0] = jnp.concatenate([oc[h * QROWS:h * QROWS + T] for h in range(HC)], axis=-1)


def _conv_attn_tail_kernel(glu_ref, halo_ref, zb_ref, cw_ref, cb_ref, lg_ref, lb_ref,
                           q_ref, kt_ref, vt_ref, kn_ref, vn_ref, tb_ref, tm_ref, nb_ref, nm_ref,
                           qc_ref, mk_ref, mv_ref, xm_ref,
                           oa_ref, za_ref, pqc_ref, pzc_ref, pmk_ref, pmv_ref, gt_ref, x_ref,
                           wpa_ref, wpb_ref, wpc_ref, wout_ref, gpost_ref,
                           y_ref, oas_ref, ocs_ref,
                           full_ref, conv_ref, obg_ref, qh_ref, knh_ref, vnh_ref, oh_ref,
                           *, tm, rc, rn, T, tiles_per_seq):
    g = pl.program_id(0)
    heads = [slice(h * DC, (h + 1) * DC) for h in range(HC)]

    def gate(k):
        return _sigmoid(gt_ref[0, :, k * D_MODEL:(k + 1) * D_MODEL].astype(F32))

    _conv_stage(glu_ref, halo_ref, g % tiles_per_seq == 0, full_ref, tm)
    q8, kn8, vn8 = _pad_rows(q_ref[0], QROWS), _pad_rows(kn_ref[0], QROWS), _pad_rows(vn_ref[0], QROWS)
    for h in range(HA):
        hs = slice(h * DA, (h + 1) * DA)
        qh_ref[h], knh_ref[h], vnh_ref[h] = q8[:, hs], kn8[:, hs], vn8[:, hs]
    scores = [lax.dot_general(pqc_ref[0, :, hs], pmk_ref[0, :, hs], NT_DIMS, preferred_element_type=F32)
              for hs in heads]
    a_g = jnp.concatenate(
        [(oa_ref[0, c].astype(F32) * _silu(za_ref[0, :, c * 128:(c + 1) * 128].astype(F32))).astype(BF16)
         for c in range(NHP)], axis=-1)
    pa = jnp.dot(a_g, wpa_ref[...], preferred_element_type=F32)
    _sample_cross_attn(qc_ref, mk_ref, mv_ref, xm_ref, ocs_ref, T)
    for c in range(tm // rc):
        _conv_taps(c, full_ref, cw_ref, conv_ref, rc)
    _sample_heads(list(range(HA)), qh_ref, knh_ref, vnh_ref, kt_ref, vt_ref,
                  tb_ref, tm_ref, nb_ref, nm_ref, oh_ref)
    pieces = []
    for s, hs in zip(scores, heads):
        p = jnp.exp(s - jnp.max(s, axis=-1, keepdims=True))
        l = jnp.sum(p, axis=-1, keepdims=True)
        oc = jnp.dot(p.astype(BF16), pmv_ref[0, :, hs], preferred_element_type=F32) / l
        pieces.append((oc * _silu(pzc_ref[0, :, hs].astype(F32))).astype(BF16))
    pc = jnp.dot(jnp.concatenate(pieces, axis=-1), wpc_ref[...], preferred_element_type=F32)
    for c in range(tm // rn):
        _conv_norm_gate(c, conv_ref, zb_ref, cb_ref, lg_ref, lb_ref, obg_ref, rn)
    pb = jnp.dot(obg_ref[...], wpb_ref[...], preferred_element_type=F32)
    mix = gate(0) * pa + gate(1) * pb + gate(2) * pc
    z = jnp.dot(mix.astype(BF16), wout_ref[...], preferred_element_type=F32)
    y_ref[0] = x_ref[0] + _rmsnorm_f32(z, gpost_ref[...])
    oas_ref[0] = jnp.concatenate([oh_ref[h][:T] for h in range(HA)], axis=-1)


def _prompt_tail_sample_attn(glu, zb, cw, cb, lg, lb, q, k_new, v_new, cache_kt, cache_vt, tables,
                             qc, mem_k, mem_v, oa, za, pqc, pzc, pmk, pmv, gt, x,
                             wpa, wpb, wpc, wout, gpost, tm, rc, rn):
    B, S, _ = glu.shape
    Bd, T, _ = q.shape
    wb = cache_kt.shape[-1]
    tiles_per_seq = S // tm
    assert B * tiles_per_seq == Bd
    cache_b, cache_m, new_b, new_m = tables
    own_head = jnp.asarray(np.where(
        np.arange(HC * QROWS)[:, None] // QROWS == np.arange(N_MEM * HC)[None, :] % HC, 0.0, NEG
    ).astype(np.float32))
    tile = lambda g: (g // tiles_per_seq, g % tiles_per_seq, 0)
    halo = lambda g: (g // tiles_per_seq, jnp.maximum((g % tiles_per_seq) * (tm // HALO) - 1, 0), 0)
    per_b3 = lambda g: (g, 0, 0)
    per_b4 = lambda g: (g, 0, 0, 0)
    c2 = lambda g: (0, 0)
    c3 = lambda g: (0, 0, 0)
    head_scratch = pltpu.VMEM((HA, QROWS, DA), F32)
    pair_tile = lambda g: (g // tiles_per_seq, 0, g % tiles_per_seq, 0)
    per_seq = lambda g: (g // tiles_per_seq, 0, 0)
    once = pl.Buffered(1)
    return pl.pallas_call(
        functools.partial(_conv_attn_tail_kernel, tm=tm, rc=rc, rn=rn, T=T, tiles_per_seq=tiles_per_seq),
        out_shape=(jax.ShapeDtypeStruct((B, S, D_MODEL), F32),
                   jax.ShapeDtypeStruct((Bd, T, EA), F32), jax.ShapeDtypeStruct((Bd, T, EC), F32)),
        grid=(Bd,),
        in_specs=[pl.BlockSpec((1, tm, EB), tile), pl.BlockSpec((1, HALO, EB), halo),
                  pl.BlockSpec((1, tm, EB), tile),
                  pl.BlockSpec((CONV_W, 8, EB), c3),
                  pl.BlockSpec((1, EB), c2), pl.BlockSpec((1, EB), c2), pl.BlockSpec((1, EB), c2),
                  pl.BlockSpec((1, T, EA), per_b3),
                  pl.BlockSpec((1, HA, DA, wb), per_b4), pl.BlockSpec((1, HA, DA, wb), per_b4),
                  pl.BlockSpec((1, T, EA), per_b3), pl.BlockSpec((1, T, EA), per_b3),
                  pl.BlockSpec((HA, QROWS, wb), c3), pl.BlockSpec((QROWS, wb), c2),
                  pl.BlockSpec((HA, QROWS, QROWS), c3), pl.BlockSpec((QROWS, QROWS), c2),
                  pl.BlockSpec((1, T, EC), per_b3),
                  pl.BlockSpec((1, N_MEM * HC, DC), per_b3), pl.BlockSpec((1, N_MEM * HC, DC), per_b3),
                  pl.BlockSpec((HC * QROWS, N_MEM * HC), c2),
                  pl.BlockSpec((1, NHP, tm, 128), pair_tile), pl.BlockSpec((1, tm, EA), tile),
                  pl.BlockSpec((1, tm, EC), tile), pl.BlockSpec((1, tm, EC), tile),
                  pl.BlockSpec((1, N_MEM, EC), per_seq), pl.BlockSpec((1, N_MEM, EC), per_seq),
                  pl.BlockSpec((1, tm, 3 * D_MODEL), tile), pl.BlockSpec((1, tm, D_MODEL), tile),
                  pl.BlockSpec((EA, D_MODEL), c2, pipeline_mode=once),
                  pl.BlockSpec((EB, D_MODEL), c2, pipeline_mode=once),
                  pl.BlockSpec((EC, D_MODEL), c2, pipeline_mode=once),
                  pl.BlockSpec((D_MODEL, D_MODEL), c2, pipeline_mode=once),
                  pl.BlockSpec((1, D_MODEL), c2)],
        out_specs=(pl.BlockSpec((1, tm, D_MODEL), tile),
                   pl.BlockSpec((1, T, EA), per_b3), pl.BlockSpec((1, T, EC), per_b3)),
        scratch_shapes=[pltpu.VMEM((8, HALO + tm, EB), F32), pltpu.VMEM((tm, EB), F32),
                        pltpu.VMEM((tm, EB), BF16),
                        head_scratch, head_scratch, head_scratch, head_scratch],
        compiler_params=_params("arbitrary"),
        name="prompt_tail_sample_attn",
    )(glu, glu, zb, cw, cb, lg, lb, q, cache_kt, cache_vt, k_new, v_new,
      cache_b, cache_m, new_b, new_m, qc, mem_k, mem_v, own_head,
      oa, za, pqc, pzc, pmk, pmv, gt, x, wpa, wpb, wpc, wout, gpost)


def _conv_sample_kernel(st_ref, glu_ref, cw_ref, cb_ref, lg_ref, lb_ref, o_ref, *, T, rc):
    n_hist = CONV_W - 1
    Bd = st_ref.shape[1]
    for i in range(T):
        def chunk(c, carry, i=i):
            rows = pl.ds(pl.multiple_of(c * rc, rc), rc)
            acc = jnp.zeros((rc, EB), F32)
            for j in range(CONV_W):
                src = st_ref[i + j, rows, :] if i + j < n_hist else glu_ref[i + j - n_hist, rows, :]
                acc = acc + src * cw_ref[j:j + 1, :]
            o_ref[i, rows, :] = _layernorm_silu(acc + cb_ref[...], lg_ref[...], lb_ref[...])
            return carry
        lax.fori_loop(0, Bd // rc, chunk, 0)


def _conv_sample(state_t, glu_t, cw, cb, lg, lb, rc):
    T, Bd, _ = glu_t.shape
    return pl.pallas_call(
        functools.partial(_conv_sample_kernel, T=T, rc=rc),
        out_shape=jax.ShapeDtypeStruct((T, Bd, EB), F32),
        compiler_params=pltpu.CompilerParams(vmem_limit_bytes=V7X_VMEM_LIMIT),
        name="conv_sample",
    )(state_t, glu_t, cw, cb, lg, lb)


def kernel(x_prompt, x_sample, mem_prompt, cache_k_win, cache_v_win, state_conv, cache_k_mem, cache_v_mem,
           rel_bias, g_pre, w_in, g_mem, w_mem_kv, conv_w, conv_b, ln_g, ln_b, w_proj_a, w_proj_b,
           w_proj_c, w_out, g_post):
    depth = g_pre.shape[0]
    assert depth == 1, "single-layer step"
    B, S, _ = x_prompt.shape
    Bd, T, _ = x_sample.shape
    wb = cache_k_win.shape[2]
    past = wb
    assert wb == MAX_DIST and S % (16 * QB) == 0 and T <= QROWS

    l = 0
    w_in_b = w_in[l].astype(BF16)
    wpa, wpb, wpc = w_proj_a[l].astype(BF16), w_proj_b[l].astype(BF16), w_proj_c[l].astype(BF16)
    wout = w_out[l].astype(BF16)
    gpre, gpost = g_pre[l][None], g_post[l][None]
    cb, lg, lb = conv_b[l][None], ln_g[l][None], ln_b[l][None]
    cw = conv_w[l]

    (qkv, k_keep, v_keep, za, glu, zb, qc, zc, gt, glu_tail) = _inproj_prompt(x_prompt, gpre, w_in_b, tm=256)
    mk_f, mv_f, mk_b, mv_b = _memkv(mem_prompt.reshape(B * N_MEM, D_MODEL), g_mem[l][None],
                                    w_mem_kv[l].astype(BF16), tm=512)
    oa = _band_attn(qkv, _band_tables(rel_bias), unroll=32)

    (q_s, k_s, v_s, za_s, glu_s, zb_s, qc_s, zc_s, gt_s) = _inproj_sample(
        x_sample.reshape(Bd * T, D_MODEL), gpre, w_in_b, tm=Bd * T)
    cache_kt = jnp.transpose(cache_k_win[l], (0, 2, 3, 1))
    cache_vt = jnp.transpose(cache_v_win[l], (0, 2, 3, 1))
    cw_tiles = jnp.broadcast_to(cw[:, None, :], (CONV_W, 8, EB))
    y_p, oa_s, oc_s = _prompt_tail_sample_attn(
        glu, zb, cw_tiles, cb, lg, lb,
        q_s.reshape(Bd, T, EA), k_s.reshape(Bd, T, EA), v_s.reshape(Bd, T, EA),
        cache_kt, cache_vt, _sample_tables(rel_bias, wb, past, T),
        qc_s.reshape(Bd, T, EC), cache_k_mem[l].reshape(Bd, N_MEM * HC, DC),
        cache_v_mem[l].reshape(Bd, N_MEM * HC, DC),
        oa, za, qc, zc, mk_b.reshape(B, N_MEM, EC), mv_b.reshape(B, N_MEM, EC), gt, x_prompt,
        wpa, wpb, wpc, wout, gpost, tm=(B * S) // Bd, rc=32, rn=64)
    state_t = jnp.transpose(state_conv[l], (1, 0, 2))
    glu_t = jnp.transpose(glu_s.reshape(Bd, T, EB), (1, 0, 2))
    ob_t = _conv_sample(state_t, glu_t, cw, cb, lg, lb, rc=32)
    ob_s = jnp.transpose(ob_t, (1, 0, 2)).reshape(Bd * T, EB)
    y_s = _tail_sample(oa_s.reshape(Bd * T, EA), za_s, ob_s, zb_s,
                       oc_s.reshape(Bd * T, EC), zc_s, gt_s, x_sample.reshape(Bd * T, D_MODEL),
                       wpa, wpb, wpc, wout, gpost, tm=Bd * T)

    conv_state_s = jnp.transpose(jnp.concatenate([state_t[T:], glu_t], axis=0), (1, 0, 2))
    return (y_p, y_s.reshape(Bd, T, D_MODEL),
            k_keep.reshape(1, B, wb, HA, DA), v_keep.reshape(1, B, wb, HA, DA),
            glu_tail[:, HALO - (CONV_W - 1):][None],
            mk_f.reshape(1, B, N_MEM, HC, DC), mv_f.reshape(1, B, N_MEM, HC, DC),
            k_s.reshape(1, Bd, T, HA, DA), v_s.reshape(1, Bd, T, HA, DA),
            conv_state_s[None])
```

```python
import functools

import jax
import jax.numpy as jnp
import numpy as np
from jax import lax
from jax.experimental import pallas as pl
from jax.experimental.pallas import tpu as pltpu

F32 = jnp.float32
BF16 = jnp.bfloat16

D_MODEL = 1024
HA, DA = 12, 64
EA = HA * DA
NHP = HA // 2
PATTERNS = ((128, 1), (512, 4), (2048, 16))
NW = 128
QB = 128
EB = 768
CONV_W = 31
HC, DC = 4, 128
EC = HC * DC
N_MEM = 256
N_BUCKETS = 32
MAX_DIST = 2048
EPS = 1e-6
NEG = -1e30
LOG2E = 1.4426950408889634

C_Q, C_K, C_V, C_ZA = 0, EA, 2 * EA, 3 * EA
C_U = 4 * EA
C_G = C_U + EB
C_ZB = C_U + 2 * EB
C_QC = C_ZB + EB
C_ZC = C_QC + EC
C_GT = C_ZC + EC
IN_COLS = C_GT + 3 * D_MODEL

V7X_VMEM_LIMIT = 56 * 1024 * 1024
HALO = 32
QROWS = 8
NT_DIMS = (((1,), (1,)), ((), ()))


def _t5_bucket_ids(n):
    exact = N_BUCKETS // 2
    nf = np.maximum(n, 1).astype(np.float32)
    scale = np.float32(N_BUCKETS - exact) / np.log(np.float32(MAX_DIST) / np.float32(exact))
    large = exact + (np.log(nf / np.float32(exact)) * scale).astype(np.int32)
    large = np.minimum(large, N_BUCKETS - 1)
    return np.where(n < exact, n, large).astype(np.int32)


def _bias_by_bucket(rel_bias, ids):
    onehot = (ids[None] == np.arange(N_BUCKETS).reshape((-1,) + (1,) * ids.ndim)).astype(np.float32)
    return jnp.tensordot(rel_bias.astype(F32).T, jnp.asarray(onehot), axes=1,
                         precision=lax.Precision.HIGHEST)


def _sigmoid(x):
    return 1.0 / (1.0 + jnp.exp(-x))


def _silu(x):
    return x * _sigmoid(x)


def _rmsnorm_f32(x, g):
    return x * lax.rsqrt(jnp.mean(x * x, axis=-1, keepdims=True) + EPS) * g


def _params(*sem):
    return pltpu.CompilerParams(dimension_semantics=sem, vmem_limit_bytes=V7X_VMEM_LIMIT)


def _inproj_columns(xn, w_ref):
    def mm(lo, n):
        return jnp.dot(xn, w_ref[:, lo:lo + n], preferred_element_type=F32)
    return mm


def _inproj_kernel(xp_ref, xs_ref, g_ref, w_ref,
                   qkv_ref, kf_ref, vf_ref, za_ref, glu_ref, zb_ref, qc_ref, zc_ref, gt_ref, tail_ref,
                   sq_ref, sk_ref, sv_ref, sza_ref, sglu_ref, szb_ref, sqc_ref, szc_ref, sgt_ref,
                   *, prompt_steps):
    g = pl.program_id(0)
    sample_refs = (sq_ref, sk_ref, sv_ref, sza_ref, sglu_ref, szb_ref, sqc_ref, szc_ref, sgt_ref)

    @pl.when(g == 0)
    def _():
        for r in sample_refs:
            r[...] = jnp.zeros(r.shape, r.dtype)

    @pl.when(g < prompt_steps)
    def _():
        mm = _inproj_columns(_rmsnorm_f32(xp_ref[0], g_ref[...]).astype(BF16), w_ref)

        def put_pairs(res, base):
            for c in range(NHP):
                qkv_ref[0, base + c] = res[:, c * 128:(c + 1) * 128].astype(BF16)

        put_pairs(mm(C_Q, EA) * (DA ** -0.5 * LOG2E), 0)
        k = mm(C_K, EA)
        put_pairs(k, NHP)
        kf_ref[0] = k
        v = mm(C_V, EA)
        put_pairs(v, 2 * NHP)
        vf_ref[0] = v
        za_ref[0] = mm(C_ZA, EA).astype(BF16)
        glu = mm(C_U, EB) * _sigmoid(mm(C_G, EB))
        glu_ref[0] = glu.astype(BF16)
        tail_ref[0] = glu[glu.shape[0] - HALO:, :]
        zb_ref[0] = mm(C_ZB, EB).astype(BF16)
        qc_ref[0] = (mm(C_QC, EC) * (DC ** -0.5)).astype(BF16)
        zc_ref[0] = mm(C_ZC, EC).astype(BF16)
        for c in range(3):
            gt_ref[0, :, c * D_MODEL:(c + 1) * D_MODEL] = mm(C_GT + c * D_MODEL, D_MODEL).astype(BF16)

    @pl.when(g >= prompt_steps)
    def _():
        mm = _inproj_columns(_rmsnorm_f32(xs_ref[...], g_ref[...]).astype(BF16), w_ref)
        sq_ref[...] = mm(C_Q, EA) * (DA ** -0.5)
        sk_ref[...] = mm(C_K, EA)
        sv_ref[...] = mm(C_V, EA)
        sza_ref[...] = mm(C_ZA, EA).astype(BF16)
        sglu_ref[...] = mm(C_U, EB) * _sigmoid(mm(C_G, EB))
        szb_ref[...] = mm(C_ZB, EB).astype(BF16)
        sqc_ref[...] = mm(C_QC, EC) * (DC ** -0.5)
        szc_ref[...] = mm(C_ZC, EC).astype(BF16)
        for c in range(3):
            sgt_ref[:, c * D_MODEL:(c + 1) * D_MODEL] = mm(C_GT + c * D_MODEL, D_MODEL).astype(BF16)


def _inproj(x, xs, g, w, tm):
    B, S, _ = x.shape
    Ms = xs.shape[0]
    nt = S // tm
    P = B * nt
    wb = min(MAX_DIST, S)
    first_kept = (S - wb) // tm
    pb = lambda s: jnp.minimum(s, P - 1) // nt
    pi = lambda s: jnp.minimum(s, P - 1) % nt
    row = lambda s: (pb(s), pi(s), 0)
    kept = lambda s: (pb(s), jnp.maximum(pi(s) - first_kept, 0), 0)
    srow = lambda s: (jnp.maximum(s - P, 0), 0)
    const = lambda s: (0, 0)
    s_widths = (EA, EA, EA, EA, EB, EB, EC, EC, 3 * D_MODEL)
    s_dtypes = (F32, F32, F32, BF16, F32, BF16, F32, BF16, BF16)
    out_shape = (
        jax.ShapeDtypeStruct((B, 3 * NHP, S, 128), BF16),
        jax.ShapeDtypeStruct((B, wb, EA), F32),
        jax.ShapeDtypeStruct((B, wb, EA), F32),
        jax.ShapeDtypeStruct((B, S, EA), BF16),
        jax.ShapeDtypeStruct((B, S, EB), BF16),
        jax.ShapeDtypeStruct((B, S, EB), BF16),
        jax.ShapeDtypeStruct((B, S, EC), BF16),
        jax.ShapeDtypeStruct((B, S, EC), BF16),
        jax.ShapeDtypeStruct((B, S, 3 * D_MODEL), BF16),
        jax.ShapeDtypeStruct((B, HALO, EB), F32),
    ) + tuple(jax.ShapeDtypeStruct((Ms, n), dt) for n, dt in zip(s_widths, s_dtypes))
    out_specs = (
        pl.BlockSpec((1, 3 * NHP, tm, 128), lambda s: (pb(s), 0, pi(s), 0)),
        pl.BlockSpec((1, tm, EA), kept),
        pl.BlockSpec((1, tm, EA), kept),
        pl.BlockSpec((1, tm, EA), row),
        pl.BlockSpec((1, tm, EB), row),
        pl.BlockSpec((1, tm, EB), row),
        pl.BlockSpec((1, tm, EC), row),
        pl.BlockSpec((1, tm, EC), row),
        pl.BlockSpec((1, tm, 3 * D_MODEL), row),
        pl.BlockSpec((1, HALO, EB), lambda s: (pb(s), 0, 0)),
    ) + tuple(pl.BlockSpec((tm, n), srow) for n in s_widths)
    return pl.pallas_call(
        functools.partial(_inproj_kernel, prompt_steps=P),
        out_shape=out_shape,
        grid=(P + Ms // tm,),
        in_specs=[
            pl.BlockSpec((1, tm, D_MODEL), row),
            pl.BlockSpec((tm, D_MODEL), srow),
            pl.BlockSpec((1, D_MODEL), const),
            pl.BlockSpec((D_MODEL, IN_COLS), const, pipeline_mode=pl.Buffered(1)),
        ],
        out_specs=out_specs,
        compiler_params=_params("arbitrary"),
        name="inproj",
    )(x, xs, g, w)


def _memkv_kernel(m_ref, g_ref, w_ref, kf_ref, vf_ref, kb_ref, vb_ref):
    xn = _rmsnorm_f32(m_ref[...], g_ref[...]).astype(BF16)
    k = jnp.dot(xn, w_ref[:, :EC], preferred_element_type=F32)
    v = jnp.dot(xn, w_ref[:, EC:], preferred_element_type=F32)
    kf_ref[...] = k
    vf_ref[...] = v
    kb_ref[...] = k.astype(BF16)
    vb_ref[...] = v.astype(BF16)


def _memkv(mem, g, w, tm):
    M = mem.shape[0]
    row = lambda i: (i, 0)
    return pl.pallas_call(
        _memkv_kernel,
        out_shape=(jax.ShapeDtypeStruct((M, EC), F32), jax.ShapeDtypeStruct((M, EC), F32),
                   jax.ShapeDtypeStruct((M, EC), BF16), jax.ShapeDtypeStruct((M, EC), BF16)),
        grid=(M // tm,),
        in_specs=[pl.BlockSpec((tm, D_MODEL), row),
                  pl.BlockSpec((1, D_MODEL), lambda i: (0, 0)),
                  pl.BlockSpec((D_MODEL, 2 * EC), lambda i: (0, 0))],
        out_specs=tuple(pl.BlockSpec((tm, EC), row) for _ in range(4)),
        compiler_params=_params("arbitrary"),
        name="memkv",
    )(mem, g, w)


def _band_tables(rel_bias):
    iq = np.arange(QB)[:, None]
    ik = np.arange(2 * QB)[None, :]
    dist = iq - ik + QB
    band = (dist >= 0) & (dist <= NW)
    out = []
    for (_, d) in PATTERNS:
        bias = LOG2E * _bias_by_bucket(rel_bias, _t5_bucket_ids(np.clip(dist, 0, NW) * d))
        later = jnp.where(jnp.asarray(band)[None], bias, NEG)
        first = jnp.where(jnp.asarray(band & (ik >= QB))[None], bias, NEG)
        t = jnp.stack([first, later], axis=1)
        out.append(t.reshape(NHP, 2, 2, QB, 2 * QB).transpose(0, 2, 1, 3, 4)
                   .reshape(NHP, 2, 2 * QB, 2 * QB))
    return jnp.stack(out)


def _band_attn_kernel(q_ref, k_ref, v_ref, t_ref, oa_ref,
                      nat32, p4_32, p4_ref, p16_ref, a4_ref, l4_ref, m4_ref, a16_ref, l16_ref, m16_ref,
                      *, S):
    n_units = S // QB
    lane = lax.broadcasted_iota(jnp.int32, (QB, 128), 1)
    head0 = lane < DA
    keep0 = head0.astype(F32).astype(BF16)
    keep1 = (1.0 - head0.astype(F32)).astype(BF16)
    ones = jnp.ones((2 * QB, 128), BF16)
    n4 = S // 4

    for t, src in enumerate((q_ref, k_ref, v_ref)):
        def widen(c, carry, src=src):
            r0 = pl.multiple_of(c * 256, 256)
            nat32[pl.ds(r0, 256), :] = src[0, 0, pl.ds(r0, 256), :].astype(F32)
            return carry
        lax.fori_loop(0, S // 256, widen, 0)
        for r in range(4):
            def by4(c, carry, r=r, t=t):
                u0 = pl.multiple_of(c * 256, 256)
                x = nat32[pl.ds(4 * u0 + r, 256, stride=4), :]
                p4_32[pl.ds(r * n4 + u0, 256), :] = x
                p4_ref[t, pl.ds(r * n4 + u0, 256), :] = x.astype(BF16)
                return carry
            lax.fori_loop(0, n4 // 256, by4, 0)
        for r4 in range(4):
            for s in range(4):
                x = p4_32[pl.ds(r4 * n4 + s, n4 // 4, stride=4), :]
                p16_ref[t, pl.ds((4 * s + r4) * (n4 // 4), n4 // 4), :] = x.astype(BF16)

    def unit(u, pat, d, qsrc, ksrc, vsrc):
        nqb = n_units // d
        row = pl.multiple_of(u * QB, QB)
        prow = pl.multiple_of(jnp.maximum(u - 1, 0) * QB, QB)
        q2 = qsrc[pl.ds(row, QB), :]
        kk = jnp.concatenate([ksrc[pl.ds(prow, QB), :], ksrc[pl.ds(row, QB), :]], axis=0)
        vv = jnp.concatenate([vsrc[pl.ds(prow, QB), :], vsrc[pl.ds(row, QB), :]], axis=0)
        qs = jnp.concatenate([q2 * keep0, q2 * keep1], axis=0)
        s = lax.dot_general(qs, kk, NT_DIMS, preferred_element_type=F32)
        s = s + t_ref[pat, 0, jnp.minimum(u & (nqb - 1), 1)]
        m = jnp.max(s, axis=-1, keepdims=True)
        p = jnp.exp2(s - m).astype(BF16)
        oe = jnp.dot(p, jnp.concatenate([vv, ones], axis=1), preferred_element_type=F32)
        mb = jnp.broadcast_to(m, (2 * QB, 128))
        return (jnp.where(head0, oe[:QB, :128], oe[QB:, :128]),
                jnp.where(head0, oe[:QB, 128:], oe[QB:, 128:]),
                jnp.where(head0, mb[:QB], mb[QB:]))

    def strided_pattern(pat, d, src, acc_ref, den_ref, max_ref):
        nqb = n_units // d

        def body(u, carry):
            acc, den, mx = unit(u, pat, d, src.at[0], src.at[1], src.at[2])
            start = u // nqb + (u & (nqb - 1)) * (d * QB)
            acc_ref[pl.ds(start, QB, stride=d), :] = acc
            den_ref[pl.ds(start, QB, stride=d), :] = den
            max_ref[pl.ds(start, QB, stride=d), :] = mx
            return carry
        lax.fori_loop(0, n_units, body, 0, unroll=n_units)

    strided_pattern(2, 16, p16_ref, a16_ref, l16_ref, m16_ref)
    strided_pattern(1, 4, p4_ref, a4_ref, l4_ref, m4_ref)

    def dense(u, carry):
        a1, l1, m1 = unit(u, 0, 1, q_ref.at[0, 0], k_ref.at[0, 0], v_ref.at[0, 0])
        rows = pl.ds(pl.multiple_of(u * QB, QB), QB)
        m4, m16 = m4_ref[rows, :], m16_ref[rows, :]
        mx = jnp.maximum(jnp.maximum(m1, m4), m16)
        e1, e4, e16 = jnp.exp2(m1 - mx), jnp.exp2(m4 - mx), jnp.exp2(m16 - mx)
        num = a1 * e1 + a4_ref[rows, :] * e4 + a16_ref[rows, :] * e16
        den = l1 * e1 + l4_ref[rows, :] * e4 + l16_ref[rows, :] * e16
        oa_ref[0, 0, rows, :] = (num / den).astype(BF16)
        return carry
    lax.fori_loop(0, n_units, dense, 0, unroll=n_units)


def _band_attn(qkv, tables):
    B, _, S, _ = qkv.shape
    blk = (1, 1, S, 128)
    return pl.pallas_call(
        functools.partial(_band_attn_kernel, S=S),
        out_shape=jax.ShapeDtypeStruct((B, NHP, S, 128), BF16),
        grid=(B, NHP),
        in_specs=[pl.BlockSpec(blk, lambda b, h: (b, h, 0, 0)),
                  pl.BlockSpec(blk, lambda b, h: (b, NHP + h, 0, 0)),
                  pl.BlockSpec(blk, lambda b, h: (b, 2 * NHP + h, 0, 0)),
                  pl.BlockSpec((3, 1, 2, 2 * QB, 2 * QB), lambda b, h: (0, h, 0, 0, 0))],
        out_specs=pl.BlockSpec(blk, lambda b, h: (b, h, 0, 0)),
        scratch_shapes=[pltpu.VMEM((S, 128), F32), pltpu.VMEM((S, 128), F32),
                        pltpu.VMEM((3, S, 128), BF16), pltpu.VMEM((3, S, 128), BF16)]
                       + [pltpu.VMEM((S, 128), F32)] * 6,
        compiler_params=_params("arbitrary", "arbitrary"),
        name="band_attn",
    )(qkv, qkv, qkv, tables)


def _layernorm_silu(c, g, b):
    mu = jnp.mean(c, axis=-1, keepdims=True)
    cc = c - mu
    var = jnp.mean(cc * cc, axis=-1, keepdims=True)
    return _silu(cc * lax.rsqrt(var + EPS) * g + b)


FIRST_TAP = HALO - (CONV_W - 1)


def _conv_stage(glu_ref, halo_ref, seq_start, full_ref, tm):
    halo = halo_ref[0].astype(F32)
    full_ref[0, 0:HALO, :] = jnp.where(seq_start, jnp.zeros_like(halo), halo)
    full_ref[0, HALO:HALO + tm, :] = glu_ref[0].astype(F32)
    n_shift = HALO + tm - 8
    for s in range(1, 8):
        full_ref[s, 0:n_shift, :] = full_ref[0, s:s + n_shift, :]


def _conv_taps(c, full_ref, cw_ref, conv_ref, rc):
    base = c * rc
    accs = [jnp.zeros((8, EB), F32) for _ in range(rc // 8)]
    for j in range(CONV_W):
        off = FIRST_TAP + j
        w8 = cw_ref[j]
        for g in range(rc // 8):
            accs[g] = accs[g] + full_ref[off % 8, pl.ds(base + (off // 8) * 8 + 8 * g, 8), :] * w8
    conv_ref[pl.ds(base, rc), :] = jnp.concatenate(accs, axis=0)


def _conv_norm_gate(c, conv_ref, zb_ref, cb_ref, lg_ref, lb_ref, o_ref, rn):
    rows = pl.ds(c * rn, rn)
    ob = _layernorm_silu(conv_ref[rows, :] + cb_ref[...], lg_ref[...], lb_ref[...])
    o_ref[rows, :] = (ob * _silu(zb_ref[0, rows, :].astype(F32))).astype(BF16)


def _gated_tail(a_g, b_g, c_g, gt_ref, x, wpa_ref, wpb_ref, wpc_ref, wout_ref, gpost_ref):
    def gate(k):
        return _sigmoid(gt_ref[:, k * D_MODEL:(k + 1) * D_MODEL].astype(F32))

    mix = gate(0) * jnp.dot(a_g, wpa_ref[...], preferred_element_type=F32)
    mix = mix + gate(1) * jnp.dot(b_g, wpb_ref[...], preferred_element_type=F32)
    mix = mix + gate(2) * jnp.dot(c_g, wpc_ref[...], preferred_element_type=F32)
    z = jnp.dot(mix.astype(BF16), wout_ref[...], preferred_element_type=F32)
    return x + _rmsnorm_f32(z, gpost_ref[...])


def _tail_prompt_kernel(oa_ref, za_ref, obg_ref, qc_ref, zc_ref, mk_ref, mv_ref, gt_ref, x_ref,
                        wpa_ref, wpb_ref, wpc_ref, wout_ref, gpost_ref, y_ref, *, groups):
    tm = x_ref.shape[1]
    heads = [slice(h * DC, (h + 1) * DC) for h in range(HC)]
    rows = [pl.ds(r * (tm // groups), tm // groups) for r in range(groups)]

    def gate(r, k):
        return _sigmoid(gt_ref[0, r, k * D_MODEL:(k + 1) * D_MODEL].astype(F32))

    pb = [jnp.dot(obg_ref[0, r, :], wpb_ref[...], preferred_element_type=F32) for r in rows]
    scores = [[lax.dot_general(qc_ref[0, r, hs], mk_ref[0, :, hs], NT_DIMS, preferred_element_type=F32)
               for hs in heads] for r in rows]
    mix = []
    for r, pb_r in zip(rows, pb):
        a_g = jnp.concatenate(
            [(oa_ref[0, c, r, :].astype(F32) * _silu(za_ref[0, r, c * 128:(c + 1) * 128].astype(F32))).astype(BF16)
             for c in range(NHP)], axis=-1)
        mix.append(gate(r, 0) * jnp.dot(a_g, wpa_ref[...], preferred_element_type=F32) + gate(r, 1) * pb_r)
    c_g = []
    for r, sc in zip(rows, scores):
        pieces = []
        for s, hs in zip(sc, heads):
            p = jnp.exp(s - jnp.max(s, axis=-1, keepdims=True))
            l = jnp.sum(p, axis=-1, keepdims=True)
            oc = jnp.dot(p.astype(BF16), mv_ref[0, :, hs], preferred_element_type=F32) / l
            pieces.append((oc * _silu(zc_ref[0, r, hs].astype(F32))).astype(BF16))
        c_g.append(jnp.concatenate(pieces, axis=-1))
    mix = [m + gate(r, 2) * jnp.dot(c, wpc_ref[...], preferred_element_type=F32)
           for r, m, c in zip(rows, mix, c_g)]
    z = [jnp.dot(m.astype(BF16), wout_ref[...], preferred_element_type=F32) for m in mix]
    for r, z_r in zip(rows, z):
        y_ref[0, r, :] = x_ref[0, r, :] + _rmsnorm_f32(z_r, gpost_ref[...])


def _tail_prompt(oa, za, obg, qc, zc, mk, mv, gt, x, wpa, wpb, wpc, wout, gpost, tm, groups):
    B, S, _ = x.shape
    row = lambda b, i: (b, i, 0)
    const = lambda b, i: (0, 0)
    return pl.pallas_call(
        functools.partial(_tail_prompt_kernel, groups=groups),
        out_shape=jax.ShapeDtypeStruct((B, S, D_MODEL), F32),
        grid=(B, S // tm),
        in_specs=[
            pl.BlockSpec((1, NHP, tm, 128), lambda b, i: (b, 0, i, 0)),
            pl.BlockSpec((1, tm, EA), row), pl.BlockSpec((1, tm, EB), row),
            pl.BlockSpec((1, tm, EC), row), pl.BlockSpec((1, tm, EC), row),
            pl.BlockSpec((1, N_MEM, EC), lambda b, i: (b, 0, 0)),
            pl.BlockSpec((1, N_MEM, EC), lambda b, i: (b, 0, 0)),
            pl.BlockSpec((1, tm, 3 * D_MODEL), row), pl.BlockSpec((1, tm, D_MODEL), row),
            pl.BlockSpec((EA, D_MODEL), const, pipeline_mode=pl.Buffered(1)),
            pl.BlockSpec((EB, D_MODEL), const, pipeline_mode=pl.Buffered(1)),
            pl.BlockSpec((EC, D_MODEL), const, pipeline_mode=pl.Buffered(1)),
            pl.BlockSpec((D_MODEL, D_MODEL), const, pipeline_mode=pl.Buffered(1)),
            pl.BlockSpec((1, D_MODEL), const)],
        out_specs=pl.BlockSpec((1, tm, D_MODEL), row),
        compiler_params=_params("arbitrary", "arbitrary"),
        name="tail_prompt",
    )(oa, za, obg, qc, zc, mk, mv, gt, x, wpa, wpb, wpc, wout, gpost)


def _tail_sample_kernel(oa_ref, za_ref, ob_ref, zb_ref, oc_ref, zc_ref, gt_ref, x_ref,
                        wpa_ref, wpb_ref, wpc_ref, wout_ref, gpost_ref, y_ref):
    a_g = (oa_ref[...] * _silu(za_ref[...].astype(F32))).astype(BF16)
    b_g = (ob_ref[...] * _silu(zb_ref[...].astype(F32))).astype(BF16)
    c_g = (oc_ref[...] * _silu(zc_ref[...].astype(F32))).astype(BF16)
    y_ref[...] = _gated_tail(a_g, b_g, c_g, gt_ref, x_ref[...],
                             wpa_ref, wpb_ref, wpc_ref, wout_ref, gpost_ref)


def _tail_sample(oa, za, ob, zb, oc, zc, gt, x, wpa, wpb, wpc, wout, gpost, tm):
    M = x.shape[0]
    row = lambda i: (i, 0)
    const = lambda i: (0, 0)
    widths = (EA, EA, EB, EB, EC, EC, 3 * D_MODEL, D_MODEL)
    return pl.pallas_call(
        _tail_sample_kernel,
        out_shape=jax.ShapeDtypeStruct((M, D_MODEL), F32),
        grid=(M // tm,),
        in_specs=[pl.BlockSpec((tm, n), row) for n in widths] + [
            pl.BlockSpec((EA, D_MODEL), const), pl.BlockSpec((EB, D_MODEL), const),
            pl.BlockSpec((EC, D_MODEL), const), pl.BlockSpec((D_MODEL, D_MODEL), const),
            pl.BlockSpec((1, D_MODEL), const)],
        out_specs=pl.BlockSpec((tm, D_MODEL), row),
        compiler_params=_params("arbitrary"),
        name="tail_sample",
    )(oa, za, ob, zb, oc, zc, gt, x, wpa, wpb, wpc, wout, gpost)


def _sample_tables(rel_bias, wb, past, T):
    def mult_of(dl, real):
        m = np.zeros(dl.shape, np.float32)
        for (w, d) in PATTERNS:
            m += (real & (dl % d == 0) & (dl <= w)).astype(np.float32)
        return m

    def tables(dl, real):
        row = np.arange(QROWS)[:, None]
        real = real & (dl >= 0) & (past + row - dl >= 0) & (wb + row - dl >= 0)
        m = mult_of(dl, real)
        dlc = np.clip(dl, 0, MAX_DIST)
        m[T:] = m[0]
        bias = jnp.where(jnp.asarray(m > 0)[None], _bias_by_bucket(rel_bias, _t5_bucket_ids(dlc)), NEG)
        return bias, jnp.asarray(m)

    i = np.arange(QROWS)[:, None]
    pos = np.arange(wb)[None, :]
    cache_b, cache_m = tables(wb + i - pos, np.ones((QROWS, wb), bool))
    j = np.arange(QROWS)[None, :]
    new_b, new_m = tables(i - j, np.broadcast_to(j < T, (QROWS, QROWS)))
    return cache_b, cache_m, new_b, new_m


def _pad_rows(x, rows):
    return jnp.concatenate([x, jnp.zeros((rows - x.shape[0], x.shape[1]), x.dtype)], axis=0)


def _sample_heads(heads, qh_ref, knh_ref, vnh_ref, kt_ref, vt_ref, cb_ref, cm_ref, nb_ref, nm_ref, oh_ref):
    scores = []
    for h in heads:
        qh = qh_ref[h].astype(BF16)
        s = jnp.dot(qh, kt_ref[0, h].astype(BF16), preferred_element_type=F32) + cb_ref[h]
        sn = lax.dot_general(qh, knh_ref[h].astype(BF16), NT_DIMS, preferred_element_type=F32) + nb_ref[h]
        scores.append((s, sn))
    probs = []
    for s, sn in scores:
        m = jnp.maximum(jnp.max(s, axis=-1, keepdims=True), jnp.max(sn, axis=-1, keepdims=True))
        p = jnp.exp(s - m) * cm_ref[...]
        pn = jnp.exp(sn - m) * nm_ref[...]
        l = jnp.sum(p, axis=-1, keepdims=True) + jnp.sum(pn, axis=-1, keepdims=True)
        probs.append((p.astype(BF16), pn.astype(BF16), l))
    for h, (p, pn, l) in zip(heads, probs):
        o = lax.dot_general(p, vt_ref[0, h].astype(BF16), NT_DIMS, preferred_element_type=F32)
        o = o + jnp.dot(pn, vnh_ref[h].astype(BF16), preferred_element_type=F32)
        oh_ref[h] = o / l


def _sample_cross_attn(qc_ref, mk_ref, mv_ref, xm_ref, oc_ref, T):
    qc8 = _pad_rows(qc_ref[0], QROWS)
    qc_heads = jnp.concatenate([qc8[:, h * DC:(h + 1) * DC] for h in range(HC)], axis=0).astype(BF16)
    sc = lax.dot_general(qc_heads, mk_ref[0].astype(BF16), NT_DIMS, preferred_element_type=F32) + xm_ref[...]
    pc = jnp.exp(sc - jnp.max(sc, axis=-1, keepdims=True))
    lc = jnp.sum(pc, axis=-1, keepdims=True)
    oc = jnp.dot(pc.astype(BF16), mv_ref[0].astype(BF16), preferred_element_type=F32) / lc
    oc_ref[0] = jnp.concatenate([oc[h * QROWS:h * QROWS + T] for h in range(HC)], axis=-1)


def _conv_attn_tail_kernel(glu_ref, halo_ref, zb_ref, cw_ref, cb_ref, lg_ref, lb_ref,
                           q_ref, kt_ref, vt_ref, kn_ref, vn_ref, tb_ref, tm_ref, nb_ref, nm_ref,
                           qc_ref, mk_ref, mv_ref, xm_ref,
                           oa_ref, za_ref, pqc_ref, pzc_ref, pmk_ref, pmv_ref, gt_ref, x_ref,
                           wpa_ref, wpb_ref, wpc_ref, wout_ref, gpost_ref,
                           y_ref, oas_ref, ocs_ref,
                           full_ref, conv_ref, obg_ref, qh_ref, knh_ref, vnh_ref, oh_ref,
                           *, tm, rc, rn, T, tiles_per_seq):
    g = pl.program_id(0)
    heads = [slice(h * DC, (h + 1) * DC) for h in range(HC)]

    def gate(k):
        return _sigmoid(gt_ref[0, :, k * D_MODEL:(k + 1) * D_MODEL].astype(F32))

    _conv_stage(glu_ref, halo_ref, g % tiles_per_seq == 0, full_ref, tm)
    q8, kn8, vn8 = _pad_rows(q_ref[0], QROWS), _pad_rows(kn_ref[0], QROWS), _pad_rows(vn_ref[0], QROWS)
    for h in range(HA):
        hs = slice(h * DA, (h + 1) * DA)
        qh_ref[h], knh_ref[h], vnh_ref[h] = q8[:, hs], kn8[:, hs], vn8[:, hs]
    scores = [lax.dot_general(pqc_ref[0, :, hs], pmk_ref[0, :, hs], NT_DIMS, preferred_element_type=F32)
              for hs in heads]
    a_g = jnp.concatenate(
        [(oa_ref[0, c].astype(F32) * _silu(za_ref[0, :, c * 128:(c + 1) * 128].astype(F32))).astype(BF16)
         for c in range(NHP)], axis=-1)
    pa = jnp.dot(a_g, wpa_ref[...], preferred_element_type=F32)
    _sample_cross_attn(qc_ref, mk_ref, mv_ref, xm_ref, ocs_ref, T)
    for c in range(tm // rc):
        _conv_taps(c, full_ref, cw_ref, conv_ref, rc)
    _sample_heads(list(range(HA)), qh_ref, knh_ref, vnh_ref, kt_ref, vt_ref,
                  tb_ref, tm_ref, nb_ref, nm_ref, oh_ref)
    pieces = []
    for s, hs in zip(scores, heads):
        p = jnp.exp(s - jnp.max(s, axis=-1, keepdims=True))
        l = jnp.sum(p, axis=-1, keepdims=True)
        oc = jnp.dot(p.astype(BF16), pmv_ref[0, :, hs], preferred_element_type=F32) / l
        pieces.append((oc * _silu(pzc_ref[0, :, hs].astype(F32))).astype(BF16))
    pc = jnp.dot(jnp.concatenate(pieces, axis=-1), wpc_ref[...], preferred_element_type=F32)
    for c in range(tm // rn):
        _conv_norm_gate(c, conv_ref, zb_ref, cb_ref, lg_ref, lb_ref, obg_ref, rn)
    pb = jnp.dot(obg_ref[...], wpb_ref[...], preferred_element_type=F32)
    mix = gate(0) * pa + gate(1) * pb + gate(2) * pc
    z = jnp.dot(mix.astype(BF16), wout_ref[...], preferred_element_type=F32)
    y_ref[0] = x_ref[0] + _rmsnorm_f32(z, gpost_ref[...])
    oas_ref[0] = jnp.concatenate([oh_ref[h][:T] for h in range(HA)], axis=-1)


def _prompt_tail_sample_attn(glu, zb, cw, cb, lg, lb, q, k_new, v_new, cache_kt, cache_vt, tables,
                             qc, mem_k, mem_v, oa, za, pqc, pzc, pmk, pmv, gt, x,
                             wpa, wpb, wpc, wout, gpost, tm, rc, rn):
    B, S, _ = glu.shape
    Bd, T, _ = q.shape
    wb = cache_kt.shape[-1]
    tiles_per_seq = S // tm
    assert B * tiles_per_seq == Bd
    cache_b, cache_m, new_b, new_m = tables
    own_head = jnp.asarray(np.where(
        np.arange(HC * QROWS)[:, None] // QROWS == np.arange(N_MEM * HC)[None, :] % HC, 0.0, NEG
    ).astype(np.float32))
    tile = lambda g: (g // tiles_per_seq, g % tiles_per_seq, 0)
    halo = lambda g: (g // tiles_per_seq, jnp.maximum((g % tiles_per_seq) * (tm // HALO) - 1, 0), 0)
    per_b3 = lambda g: (g, 0, 0)
    per_b4 = lambda g: (g, 0, 0, 0)
    c2 = lambda g: (0, 0)
    c3 = lambda g: (0, 0, 0)
    head_scratch = pltpu.VMEM((HA, QROWS, DA), F32)
    pair_tile = lambda g: (g // tiles_per_seq, 0, g % tiles_per_seq, 0)
    per_seq = lambda g: (g // tiles_per_seq, 0, 0)
    once = pl.Buffered(1)
    return pl.pallas_call(
        functools.partial(_conv_attn_tail_kernel, tm=tm, rc=rc, rn=rn, T=T, tiles_per_seq=tiles_per_seq),
        out_shape=(jax.ShapeDtypeStruct((B, S, D_MODEL), F32),
                   jax.ShapeDtypeStruct((Bd, T, EA), F32), jax.ShapeDtypeStruct((Bd, T, EC), F32)),
        grid=(Bd,),
        in_specs=[pl.BlockSpec((1, tm, EB), tile), pl.BlockSpec((1, HALO, EB), halo),
                  pl.BlockSpec((1, tm, EB), tile),
                  pl.BlockSpec((CONV_W, 8, EB), c3),
                  pl.BlockSpec((1, EB), c2), pl.BlockSpec((1, EB), c2), pl.BlockSpec((1, EB), c2),
                  pl.BlockSpec((1, T, EA), per_b3),
                  pl.BlockSpec((1, HA, DA, wb), per_b4), pl.BlockSpec((1, HA, DA, wb), per_b4),
                  pl.BlockSpec((1, T, EA), per_b3), pl.BlockSpec((1, T, EA), per_b3),
                  pl.BlockSpec((HA, QROWS, wb), c3), pl.BlockSpec((QROWS, wb), c2),
                  pl.BlockSpec((HA, QROWS, QROWS), c3), pl.BlockSpec((QROWS, QROWS), c2),
                  pl.BlockSpec((1, T, EC), per_b3),
                  pl.BlockSpec((1, N_MEM * HC, DC), per_b3), pl.BlockSpec((1, N_MEM * HC, DC), per_b3),
                  pl.BlockSpec((HC * QROWS, N_MEM * HC), c2),
                  pl.BlockSpec((1, NHP, tm, 128), pair_tile), pl.BlockSpec((1, tm, EA), tile),
                  pl.BlockSpec((1, tm, EC), tile), pl.BlockSpec((1, tm, EC), tile),
                  pl.BlockSpec((1, N_MEM, EC), per_seq), pl.BlockSpec((1, N_MEM, EC), per_seq),
                  pl.BlockSpec((1, tm, 3 * D_MODEL), tile), pl.BlockSpec((1, tm, D_MODEL), tile),
                  pl.BlockSpec((EA, D_MODEL), c2, pipeline_mode=once),
                  pl.BlockSpec((EB, D_MODEL), c2, pipeline_mode=once),
                  pl.BlockSpec((EC, D_MODEL), c2, pipeline_mode=once),
                  pl.BlockSpec((D_MODEL, D_MODEL), c2, pipeline_mode=once),
                  pl.BlockSpec((1, D_MODEL), c2)],
        out_specs=(pl.BlockSpec((1, tm, D_MODEL), tile),
                   pl.BlockSpec((1, T, EA), per_b3), pl.BlockSpec((1, T, EC), per_b3)),
        scratch_shapes=[pltpu.VMEM((8, HALO + tm, EB), F32), pltpu.VMEM((tm, EB), F32),
                        pltpu.VMEM((tm, EB), BF16),
                        head_scratch, head_scratch, head_scratch, head_scratch],
        compiler_params=_params("arbitrary"),
        name="prompt_tail_sample_attn",
    )(glu, glu, zb, cw, cb, lg, lb, q, cache_kt, cache_vt, k_new, v_new,
      cache_b, cache_m, new_b, new_m, qc, mem_k, mem_v, own_head,
      oa, za, pqc, pzc, pmk, pmv, gt, x, wpa, wpb, wpc, wout, gpost)


def _conv_sample_kernel(st_ref, glu_ref, cw_ref, cb_ref, lg_ref, lb_ref, o_ref, ns_ref, *, T, rc):
    n_hist = CONV_W - 1
    Bd = st_ref.shape[1]

    def shift(t, carry):
        ns_ref[t] = st_ref[t + T]
        return carry
    lax.fori_loop(0, n_hist - T, shift, 0)
    for i in range(T):
        ns_ref[n_hist - T + i] = glu_ref[i]
    for i in range(T):
        def chunk(c, carry, i=i):
            rows = pl.ds(pl.multiple_of(c * rc, rc), rc)
            acc = jnp.zeros((rc, EB), F32)
            for j in range(CONV_W):
                src = st_ref[i + j, rows, :] if i + j < n_hist else glu_ref[i + j - n_hist, rows, :]
                acc = acc + src * cw_ref[j:j + 1, :]
            o_ref[i, rows, :] = _layernorm_silu(acc + cb_ref[...], lg_ref[...], lb_ref[...])
            return carry
        lax.fori_loop(0, Bd // rc, chunk, 0)


def _conv_sample(state_t, glu_t, cw, cb, lg, lb, rc):
    T, Bd, _ = glu_t.shape
    return pl.pallas_call(
        functools.partial(_conv_sample_kernel, T=T, rc=rc),
        out_shape=(jax.ShapeDtypeStruct((T, Bd, EB), F32), jax.ShapeDtypeStruct(state_t.shape, F32)),
        compiler_params=pltpu.CompilerParams(vmem_limit_bytes=V7X_VMEM_LIMIT),
        name="conv_sample",
    )(state_t, glu_t, cw, cb, lg, lb)


def kernel(x_prompt, x_sample, mem_prompt, cache_k_win, cache_v_win, state_conv, cache_k_mem, cache_v_mem,
           rel_bias, g_pre, w_in, g_mem, w_mem_kv, conv_w, conv_b, ln_g, ln_b, w_proj_a, w_proj_b,
           w_proj_c, w_out, g_post):
    depth = g_pre.shape[0]
    assert depth == 1, "single-layer step"
    B, S, _ = x_prompt.shape
    Bd, T, _ = x_sample.shape
    wb = cache_k_win.shape[2]
    past = wb
    assert wb == MAX_DIST and S % (16 * QB) == 0 and T <= QROWS

    l = 0
    w_in_b = w_in[l].astype(BF16)
    wpa, wpb, wpc = w_proj_a[l].astype(BF16), w_proj_b[l].astype(BF16), w_proj_c[l].astype(BF16)
    wout = w_out[l].astype(BF16)
    gpre, gpost = g_pre[l][None], g_post[l][None]
    cb, lg, lb = conv_b[l][None], ln_g[l][None], ln_b[l][None]
    cw = conv_w[l]

    (qkv, k_keep, v_keep, za, glu, zb, qc, zc, gt, glu_tail,
     q_s, k_s, v_s, za_s, glu_s, zb_s, qc_s, zc_s, gt_s) = _inproj(
        x_prompt, x_sample.reshape(Bd * T, D_MODEL), gpre, w_in_b, tm=256)
    mk_f, mv_f, mk_b, mv_b = _memkv(mem_prompt.reshape(B * N_MEM, D_MODEL), g_mem[l][None],
                                    w_mem_kv[l].astype(BF16), tm=512)
    oa = _band_attn(qkv, _band_tables(rel_bias))

    cache_kt = jnp.transpose(cache_k_win[l], (0, 2, 3, 1))
    cache_vt = jnp.transpose(cache_v_win[l], (0, 2, 3, 1))
    cw_tiles = jnp.broadcast_to(cw[:, None, :], (CONV_W, 8, EB))
    y_p, oa_s, oc_s = _prompt_tail_sample_attn(
        glu, zb, cw_tiles, cb, lg, lb,
        q_s.reshape(Bd, T, EA), k_s.reshape(Bd, T, EA), v_s.reshape(Bd, T, EA),
        cache_kt, cache_vt, _sample_tables(rel_bias, wb, past, T),
        qc_s.reshape(Bd, T, EC), cache_k_mem[l].reshape(Bd, N_MEM * HC, DC),
        cache_v_mem[l].reshape(Bd, N_MEM * HC, DC),
        oa, za, qc, zc, mk_b.reshape(B, N_MEM, EC), mv_b.reshape(B, N_MEM, EC), gt, x_prompt,
        wpa, wpb, wpc, wout, gpost, tm=(B * S) // Bd, rc=32, rn=64)
    state_t = jnp.transpose(state_conv[l], (1, 0, 2))
    glu_t = jnp.transpose(glu_s.reshape(Bd, T, EB), (1, 0, 2))
    ob_t, new_state_t = _conv_sample(state_t, glu_t, cw, cb, lg, lb, rc=32)
    ob_s = jnp.transpose(ob_t, (1, 0, 2)).reshape(Bd * T, EB)
    y_s = _tail_sample(oa_s.reshape(Bd * T, EA), za_s, ob_s, zb_s,
                       oc_s.reshape(Bd * T, EC), zc_s, gt_s, x_sample.reshape(Bd * T, D_MODEL),
                       wpa, wpb, wpc, wout, gpost, tm=Bd * T)

    conv_state_s = jnp.transpose(new_state_t, (1, 0, 2))
    return (y_p, y_s.reshape(Bd, T, D_MODEL),
            k_keep.reshape(1, B, wb, HA, DA), v_keep.reshape(1, B, wb, HA, DA),
            glu_tail[:, HALO - (CONV_W - 1):][None],
            mk_f.reshape(1, B, N_MEM, HC, DC), mv_f.reshape(1, B, N_MEM, HC, DC),
            k_s.reshape(1, Bd, T, HA, DA), v_s.reshape(1, Bd, T, HA, DA),
            conv_state_s[None])
```

```python
import functools

import jax
import jax.numpy as jnp
import numpy as np
from jax import lax
from jax.experimental import pallas as pl
from jax.experimental.pallas import tpu as pltpu

F32 = jnp.float32
BF16 = jnp.bfloat16

D_MODEL = 1024
HA, DA = 12, 64
EA = HA * DA
NHP = HA // 2
PATTERNS = ((128, 1), (512, 4), (2048, 16))
NW = 128
QB = 128
EB = 768
CONV_W = 31
HC, DC = 4, 128
EC = HC * DC
N_MEM = 256
N_BUCKETS = 32
MAX_DIST = 2048
EPS = 1e-6
NEG = -1e30
LOG2E = 1.4426950408889634

C_Q, C_K, C_V, C_ZA = 0, EA, 2 * EA, 3 * EA
C_U = 4 * EA
C_G = C_U + EB
C_ZB = C_U + 2 * EB
C_QC = C_ZB + EB
C_ZC = C_QC + EC
C_GT = C_ZC + EC
IN_COLS = C_GT + 3 * D_MODEL

V7X_VMEM_LIMIT = 56 * 1024 * 1024
HALO = 32
QROWS = 8
NT_DIMS = (((1,), (1,)), ((), ()))


def _t5_bucket_ids(n):
    exact = N_BUCKETS // 2
    nf = np.maximum(n, 1).astype(np.float32)
    scale = np.float32(N_BUCKETS - exact) / np.log(np.float32(MAX_DIST) / np.float32(exact))
    large = exact + (np.log(nf / np.float32(exact)) * scale).astype(np.int32)
    large = np.minimum(large, N_BUCKETS - 1)
    return np.where(n < exact, n, large).astype(np.int32)


def _bias_by_bucket(rel_bias, ids):
    onehot = (ids[None] == np.arange(N_BUCKETS).reshape((-1,) + (1,) * ids.ndim)).astype(np.float32)
    return jnp.tensordot(rel_bias.astype(F32).T, jnp.asarray(onehot), axes=1,
                         precision=lax.Precision.HIGHEST)


def _sigmoid(x):
    return 1.0 / (1.0 + jnp.exp(-x))


def _silu(x):
    return x * _sigmoid(x)


def _rmsnorm_f32(x, g):
    return x * lax.rsqrt(jnp.mean(x * x, axis=-1, keepdims=True) + EPS) * g


def _params(*sem):
    return pltpu.CompilerParams(dimension_semantics=sem, vmem_limit_bytes=V7X_VMEM_LIMIT)


def _inproj_columns(xn, xn_ref, w_ref):
    xn_ref[...] = xn

    def mm(lo, n):
        return jnp.dot(xn_ref[...], w_ref[:, lo:lo + n], preferred_element_type=F32)
    return mm


def _inproj_kernel(xp_ref, xs_ref, g_ref, w_ref, cw_ref, cb_ref, lg_ref, lb_ref,
                   qkv_ref, kf_ref, vf_ref, za_ref, obg_ref, qc_ref, zc_ref, gt_ref, tail_ref,
                   sq_ref, sk_ref, sv_ref, sza_ref, sglu_ref, szb_ref, sqc_ref, szc_ref, sgt_ref,
                   full_ref, conv_ref, halo_ref, xn_ref,
                   *, prompt_steps, tiles_per_seq, rc, rn):
    g = pl.program_id(0)
    tm = xp_ref.shape[1]
    sample_refs = (sq_ref, sk_ref, sv_ref, sza_ref, sglu_ref, szb_ref, sqc_ref, szc_ref, sgt_ref)

    @pl.when(g == 0)
    def _():
        for r in sample_refs + (halo_ref,):
            r[...] = jnp.zeros(r.shape, r.dtype)

    @pl.when(g < prompt_steps)
    def _():
        mm = _inproj_columns(_rmsnorm_f32(xp_ref[0], g_ref[...]).astype(BF16), xn_ref, w_ref)

        def put_pairs(res, base):
            for c in range(NHP):
                qkv_ref[0, base + c] = res[:, c * 128:(c + 1) * 128].astype(BF16)

        glu = mm(C_U, EB) * _sigmoid(mm(C_G, EB))
        zb = mm(C_ZB, EB)
        tail_ref[0] = glu[tm - HALO:, :]
        _conv_stage(glu, halo_ref[...], g % tiles_per_seq == 0, full_ref, tm)
        halo_ref[...] = glu[tm - HALO:, :]
        for c in range(tm // rc):
            _conv_taps(c, full_ref, cw_ref, conv_ref, rc)
        for c in range(tm // rn):
            rows = slice(c * rn, (c + 1) * rn)
            obg_ref[0, rows, :] = _conv_norm_gate(conv_ref[rows, :], zb[rows], cb_ref, lg_ref, lb_ref)
        put_pairs(mm(C_Q, EA) * (DA ** -0.5 * LOG2E), 0)
        k = mm(C_K, EA)
        put_pairs(k, NHP)
        kf_ref[0] = k
        v = mm(C_V, EA)
        put_pairs(v, 2 * NHP)
        vf_ref[0] = v
        za_ref[0] = mm(C_ZA, EA).astype(BF16)
        qc_ref[0] = (mm(C_QC, EC) * (DC ** -0.5)).astype(BF16)
        zc_ref[0] = mm(C_ZC, EC).astype(BF16)
        for c in range(3):
            gt_ref[0, :, c * D_MODEL:(c + 1) * D_MODEL] = mm(C_GT + c * D_MODEL, D_MODEL).astype(BF16)

    @pl.when(g >= prompt_steps)
    def _():
        mm = _inproj_columns(_rmsnorm_f32(xs_ref[...], g_ref[...]).astype(BF16), xn_ref, w_ref)
        sq_ref[...] = mm(C_Q, EA) * (DA ** -0.5)
        sk_ref[...] = mm(C_K, EA)
        sv_ref[...] = mm(C_V, EA)
        sza_ref[...] = mm(C_ZA, EA).astype(BF16)
        sglu_ref[...] = mm(C_U, EB) * _sigmoid(mm(C_G, EB))
        szb_ref[...] = mm(C_ZB, EB).astype(BF16)
        sqc_ref[...] = mm(C_QC, EC) * (DC ** -0.5)
        szc_ref[...] = mm(C_ZC, EC).astype(BF16)
        for c in range(3):
            sgt_ref[:, c * D_MODEL:(c + 1) * D_MODEL] = mm(C_GT + c * D_MODEL, D_MODEL).astype(BF16)


def _inproj(x, xs, g, w, cw, cb, lg, lb, tm, rc, rn):
    B, S, _ = x.shape
    Ms = xs.shape[0]
    nt = S // tm
    P = B * nt
    wb = min(MAX_DIST, S)
    first_kept = (S - wb) // tm
    pb = lambda s: jnp.minimum(s, P - 1) // nt
    pi = lambda s: jnp.minimum(s, P - 1) % nt
    row = lambda s: (pb(s), pi(s), 0)
    kept = lambda s: (pb(s), jnp.maximum(pi(s) - first_kept, 0), 0)
    srow = lambda s: (jnp.maximum(s - P, 0), 0)
    const = lambda s: (0, 0)
    s_widths = (EA, EA, EA, EA, EB, EB, EC, EC, 3 * D_MODEL)
    s_dtypes = (F32, F32, F32, BF16, F32, BF16, F32, BF16, BF16)
    out_shape = (
        jax.ShapeDtypeStruct((B, 3 * NHP, S, 128), BF16),
        jax.ShapeDtypeStruct((B, wb, EA), F32),
        jax.ShapeDtypeStruct((B, wb, EA), F32),
        jax.ShapeDtypeStruct((B, S, EA), BF16),
        jax.ShapeDtypeStruct((B, S, EB), BF16),
        jax.ShapeDtypeStruct((B, S, EC), BF16),
        jax.ShapeDtypeStruct((B, S, EC), BF16),
        jax.ShapeDtypeStruct((B, S, 3 * D_MODEL), BF16),
        jax.ShapeDtypeStruct((B, HALO, EB), F32),
    ) + tuple(jax.ShapeDtypeStruct((Ms, n), dt) for n, dt in zip(s_widths, s_dtypes))
    out_specs = (
        pl.BlockSpec((1, 3 * NHP, tm, 128), lambda s: (pb(s), 0, pi(s), 0)),
        pl.BlockSpec((1, tm, EA), kept),
        pl.BlockSpec((1, tm, EA), kept),
        pl.BlockSpec((1, tm, EA), row),
        pl.BlockSpec((1, tm, EB), row),
        pl.BlockSpec((1, tm, EC), row),
        pl.BlockSpec((1, tm, EC), row),
        pl.BlockSpec((1, tm, 3 * D_MODEL), row),
        pl.BlockSpec((1, HALO, EB), lambda s: (pb(s), 0, 0)),
    ) + tuple(pl.BlockSpec((tm, n), srow) for n in s_widths)
    return pl.pallas_call(
        functools.partial(_inproj_kernel, prompt_steps=P, tiles_per_seq=nt, rc=rc, rn=rn),
        out_shape=out_shape,
        grid=(P + Ms // tm,),
        in_specs=[
            pl.BlockSpec((1, tm, D_MODEL), row),
            pl.BlockSpec((tm, D_MODEL), srow),
            pl.BlockSpec((1, D_MODEL), const),
            pl.BlockSpec((D_MODEL, IN_COLS), const, pipeline_mode=pl.Buffered(1)),
            pl.BlockSpec((CONV_W, 8, EB), lambda s: (0, 0, 0)),
            pl.BlockSpec((1, EB), const), pl.BlockSpec((1, EB), const), pl.BlockSpec((1, EB), const),
        ],
        out_specs=out_specs,
        scratch_shapes=[pltpu.VMEM((8, HALO + tm, EB), F32), pltpu.VMEM((tm, EB), F32),
                        pltpu.VMEM((HALO, EB), F32), pltpu.VMEM((tm, D_MODEL), BF16)],
        compiler_params=_params("arbitrary"),
        name="inproj",
    )(x, xs, g, w, cw, cb, lg, lb)


def _memkv_kernel(m_ref, g_ref, w_ref, kf_ref, vf_ref, kb_ref, vb_ref):
    xn = _rmsnorm_f32(m_ref[...], g_ref[...]).astype(BF16)
    k = jnp.dot(xn, w_ref[:, :EC], preferred_element_type=F32)
    v = jnp.dot(xn, w_ref[:, EC:], preferred_element_type=F32)
    kf_ref[...] = k
    vf_ref[...] = v
    kb_ref[...] = k.astype(BF16)
    vb_ref[...] = v.astype(BF16)


def _memkv(mem, g, w, tm):
    M = mem.shape[0]
    row = lambda i: (i, 0)
    return pl.pallas_call(
        _memkv_kernel,
        out_shape=(jax.ShapeDtypeStruct((M, EC), F32), jax.ShapeDtypeStruct((M, EC), F32),
                   jax.ShapeDtypeStruct((M, EC), BF16), jax.ShapeDtypeStruct((M, EC), BF16)),
        grid=(M // tm,),
        in_specs=[pl.BlockSpec((tm, D_MODEL), row),
                  pl.BlockSpec((1, D_MODEL), lambda i: (0, 0)),
                  pl.BlockSpec((D_MODEL, 2 * EC), lambda i: (0, 0))],
        out_specs=tuple(pl.BlockSpec((tm, EC), row) for _ in range(4)),
        compiler_params=_params("arbitrary"),
        name="memkv",
    )(mem, g, w)


def _band_tables(rel_bias):
    iq = np.arange(QB)[:, None]
    ik = np.arange(2 * QB)[None, :]
    dist = iq - ik + QB
    band = (dist >= 0) & (dist <= NW)
    out = []
    for (_, d) in PATTERNS:
        bias = LOG2E * _bias_by_bucket(rel_bias, _t5_bucket_ids(np.clip(dist, 0, NW) * d))
        later = jnp.where(jnp.asarray(band)[None], bias, NEG)
        first = jnp.where(jnp.asarray(band & (ik >= QB))[None], bias, NEG)
        t = jnp.stack([first, later], axis=1)
        out.append(t.reshape(NHP, 2, 2, QB, 2 * QB).transpose(0, 2, 1, 3, 4)
                   .reshape(NHP, 2, 2 * QB, 2 * QB))
    return jnp.stack(out)


def _band_attn_kernel(q_ref, k_ref, v_ref, t_ref, oa_ref,
                      nat32, p4_32, p4_ref, p16_ref, a4_ref, l4_ref, m4_ref, a16_ref, l16_ref, m16_ref,
                      *, S):
    n_units = S // QB
    lane = lax.broadcasted_iota(jnp.int32, (QB, 128), 1)
    head0 = lane < DA
    keep0 = head0.astype(F32).astype(BF16)
    keep1 = (1.0 - head0.astype(F32)).astype(BF16)
    ones = jnp.ones((2 * QB, 128), BF16)
    n4 = S // 4

    for t, src in enumerate((q_ref, k_ref, v_ref)):
        def widen(c, carry, src=src):
            r0 = pl.multiple_of(c * 256, 256)
            nat32[pl.ds(r0, 256), :] = src[0, 0, pl.ds(r0, 256), :].astype(F32)
            return carry
        lax.fori_loop(0, S // 256, widen, 0)
        for r in range(4):
            def by4(c, carry, r=r, t=t):
                u0 = pl.multiple_of(c * 256, 256)
                x = nat32[pl.ds(4 * u0 + r, 256, stride=4), :]
                p4_32[pl.ds(r * n4 + u0, 256), :] = x
                p4_ref[t, pl.ds(r * n4 + u0, 256), :] = x.astype(BF16)
                return carry
            lax.fori_loop(0, n4 // 256, by4, 0)
        for r4 in range(4):
            for s in range(4):
                x = p4_32[pl.ds(r4 * n4 + s, n4 // 4, stride=4), :]
                p16_ref[t, pl.ds((4 * s + r4) * (n4 // 4), n4 // 4), :] = x.astype(BF16)

    def unit(u, pat, d, qsrc, ksrc, vsrc):
        nqb = n_units // d
        row = pl.multiple_of(u * QB, QB)
        prow = pl.multiple_of(jnp.maximum(u - 1, 0) * QB, QB)
        q2 = qsrc[pl.ds(row, QB), :]
        kk = jnp.concatenate([ksrc[pl.ds(prow, QB), :], ksrc[pl.ds(row, QB), :]], axis=0)
        vv = jnp.concatenate([vsrc[pl.ds(prow, QB), :], vsrc[pl.ds(row, QB), :]], axis=0)
        qs = jnp.concatenate([q2 * keep0, q2 * keep1], axis=0)
        s = lax.dot_general(qs, kk, NT_DIMS, preferred_element_type=F32)
        s = s + t_ref[pat, 0, jnp.minimum(u & (nqb - 1), 1)]
        m = jnp.max(s, axis=-1, keepdims=True)
        p = jnp.exp2(s - m).astype(BF16)
        oe = jnp.dot(p, jnp.concatenate([vv, ones], axis=1), preferred_element_type=F32)
        mb = jnp.broadcast_to(m, (2 * QB, 128))
        return (jnp.where(head0, oe[:QB, :128], oe[QB:, :128]),
                jnp.where(head0, oe[:QB, 128:], oe[QB:, 128:]),
                jnp.where(head0, mb[:QB], mb[QB:]))

    def strided_pattern(pat, d, src, acc_ref, den_ref, max_ref):
        nqb = n_units // d

        def body(u, carry):
            acc, den, mx = unit(u, pat, d, src.at[0], src.at[1], src.at[2])
            start = u // nqb + (u & (nqb - 1)) * (d * QB)
            acc_ref[pl.ds(start, QB, stride=d), :] = acc
            den_ref[pl.ds(start, QB, stride=d), :] = den
            max_ref[pl.ds(start, QB, stride=d), :] = mx
            return carry
        lax.fori_loop(0, n_units, body, 0, unroll=n_units)

    strided_pattern(2, 16, p16_ref, a16_ref, l16_ref, m16_ref)
    strided_pattern(1, 4, p4_ref, a4_ref, l4_ref, m4_ref)

    def dense(u, carry):
        a1, l1, m1 = unit(u, 0, 1, q_ref.at[0, 0], k_ref.at[0, 0], v_ref.at[0, 0])
        rows = pl.ds(pl.multiple_of(u * QB, QB), QB)
        m4, m16 = m4_ref[rows, :], m16_ref[rows, :]
        mx = jnp.maximum(jnp.maximum(m1, m4), m16)
        e1, e4, e16 = jnp.exp2(m1 - mx), jnp.exp2(m4 - mx), jnp.exp2(m16 - mx)
        num = a1 * e1 + a4_ref[rows, :] * e4 + a16_ref[rows, :] * e16
        den = l1 * e1 + l4_ref[rows, :] * e4 + l16_ref[rows, :] * e16
        oa_ref[0, 0, rows, :] = (num / den).astype(BF16)
        return carry
    lax.fori_loop(0, n_units, dense, 0, unroll=n_units)


def _band_attn(qkv, tables):
    B, _, S, _ = qkv.shape
    blk = (1, 1, S, 128)
    return pl.pallas_call(
        functools.partial(_band_attn_kernel, S=S),
        out_shape=jax.ShapeDtypeStruct((B, NHP, S, 128), BF16),
        grid=(B, NHP),
        in_specs=[pl.BlockSpec(blk, lambda b, h: (b, h, 0, 0)),
                  pl.BlockSpec(blk, lambda b, h: (b, NHP + h, 0, 0)),
                  pl.BlockSpec(blk, lambda b, h: (b, 2 * NHP + h, 0, 0)),
                  pl.BlockSpec((3, 1, 2, 2 * QB, 2 * QB), lambda b, h: (0, h, 0, 0, 0))],
        out_specs=pl.BlockSpec(blk, lambda b, h: (b, h, 0, 0)),
        scratch_shapes=[pltpu.VMEM((S, 128), F32), pltpu.VMEM((S, 128), F32),
                        pltpu.VMEM((3, S, 128), BF16), pltpu.VMEM((3, S, 128), BF16)]
                       + [pltpu.VMEM((S, 128), F32)] * 6,
        compiler_params=_params("arbitrary", "arbitrary"),
        name="band_attn",
    )(qkv, qkv, qkv, tables)


def _layernorm_silu(c, g, b):
    mu = jnp.mean(c, axis=-1, keepdims=True)
    cc = c - mu
    var = jnp.mean(cc * cc, axis=-1, keepdims=True)
    return _silu(cc * lax.rsqrt(var + EPS) * g + b)


FIRST_TAP = HALO - (CONV_W - 1)


def _conv_stage(glu, halo, seq_start, full_ref, tm):
    full_ref[0, 0:HALO, :] = jnp.where(seq_start, jnp.zeros_like(halo), halo)
    full_ref[0, HALO:HALO + tm, :] = glu
    n_shift = HALO + tm - 8
    for s in range(1, 8):
        full_ref[s, 0:n_shift, :] = full_ref[0, s:s + n_shift, :]


def _conv_taps(c, full_ref, cw_ref, conv_ref, rc):
    base = c * rc
    accs = [jnp.zeros((8, EB), F32) for _ in range(rc // 8)]
    for j in range(CONV_W):
        off = FIRST_TAP + j
        w8 = cw_ref[j]
        for g in range(rc // 8):
            accs[g] = accs[g] + full_ref[off % 8, pl.ds(base + (off // 8) * 8 + 8 * g, 8), :] * w8
    conv_ref[pl.ds(base, rc), :] = jnp.concatenate(accs, axis=0)


def _conv_norm_gate(conv, zb, cb_ref, lg_ref, lb_ref):
    return (_layernorm_silu(conv + cb_ref[...], lg_ref[...], lb_ref[...]) * _silu(zb)).astype(BF16)


def _gated_tail(a_g, b_g, c_g, gt_ref, x, wpa_ref, wpb_ref, wpc_ref, wout_ref, gpost_ref):
    def gate(k):
        return _sigmoid(gt_ref[:, k * D_MODEL:(k + 1) * D_MODEL].astype(F32))

    mix = gate(0) * jnp.dot(a_g, wpa_ref[...], preferred_element_type=F32)
    mix = mix + gate(1) * jnp.dot(b_g, wpb_ref[...], preferred_element_type=F32)
    mix = mix + gate(2) * jnp.dot(c_g, wpc_ref[...], preferred_element_type=F32)
    z = jnp.dot(mix.astype(BF16), wout_ref[...], preferred_element_type=F32)
    return x + _rmsnorm_f32(z, gpost_ref[...])


def _tail_prompt_kernel(oa_ref, za_ref, obg_ref, qc_ref, zc_ref, mk_ref, mv_ref, gt_ref, x_ref,
                        wpa_ref, wpb_ref, wpc_ref, wout_ref, gpost_ref, y_ref, *, groups):
    tm = x_ref.shape[1]
    heads = [slice(h * DC, (h + 1) * DC) for h in range(HC)]
    rows = [pl.ds(r * (tm // groups), tm // groups) for r in range(groups)]

    def gate(r, k):
        return _sigmoid(gt_ref[0, r, k * D_MODEL:(k + 1) * D_MODEL].astype(F32))

    pb = [jnp.dot(obg_ref[0, r, :], wpb_ref[...], preferred_element_type=F32) for r in rows]
    scores = [[lax.dot_general(qc_ref[0, r, hs], mk_ref[0, :, hs], NT_DIMS, preferred_element_type=F32)
               for hs in heads] for r in rows]
    mix = []
    for r, pb_r in zip(rows, pb):
        a_g = jnp.concatenate(
            [(oa_ref[0, c, r, :].astype(F32) * _silu(za_ref[0, r, c * 128:(c + 1) * 128].astype(F32))).astype(BF16)
             for c in range(NHP)], axis=-1)
        mix.append(gate(r, 0) * jnp.dot(a_g, wpa_ref[...], preferred_element_type=F32) + gate(r, 1) * pb_r)
    c_g = []
    for r, sc in zip(rows, scores):
        pieces = []
        for s, hs in zip(sc, heads):
            p = jnp.exp(s - jnp.max(s, axis=-1, keepdims=True))
            l = jnp.sum(p, axis=-1, keepdims=True)
            oc = jnp.dot(p.astype(BF16), mv_ref[0, :, hs], preferred_element_type=F32) / l
            pieces.append((oc * _silu(zc_ref[0, r, hs].astype(F32))).astype(BF16))
        c_g.append(jnp.concatenate(pieces, axis=-1))
    mix = [m + gate(r, 2) * jnp.dot(c, wpc_ref[...], preferred_element_type=F32)
           for r, m, c in zip(rows, mix, c_g)]
    z = [jnp.dot(m.astype(BF16), wout_ref[...], preferred_element_type=F32) for m in mix]
    for r, z_r in zip(rows, z):
        y_ref[0, r, :] = x_ref[0, r, :] + _rmsnorm_f32(z_r, gpost_ref[...])


def _tail_prompt(oa, za, obg, qc, zc, mk, mv, gt, x, wpa, wpb, wpc, wout, gpost, tm, groups):
    B, S, _ = x.shape
    row = lambda b, i: (b, i, 0)
    const = lambda b, i: (0, 0)
    return pl.pallas_call(
        functools.partial(_tail_prompt_kernel, groups=groups),
        out_shape=jax.ShapeDtypeStruct((B, S, D_MODEL), F32),
        grid=(B, S // tm),
        in_specs=[
            pl.BlockSpec((1, NHP, tm, 128), lambda b, i: (b, 0, i, 0)),
            pl.BlockSpec((1, tm, EA), row), pl.BlockSpec((1, tm, EB), row),
            pl.BlockSpec((1, tm, EC), row), pl.BlockSpec((1, tm, EC), row),
            pl.BlockSpec((1, N_MEM, EC), lambda b, i: (b, 0, 0)),
            pl.BlockSpec((1, N_MEM, EC), lambda b, i: (b, 0, 0)),
            pl.BlockSpec((1, tm, 3 * D_MODEL), row), pl.BlockSpec((1, tm, D_MODEL), row),
            pl.BlockSpec((EA, D_MODEL), const, pipeline_mode=pl.Buffered(1)),
            pl.BlockSpec((EB, D_MODEL), const, pipeline_mode=pl.Buffered(1)),
            pl.BlockSpec((EC, D_MODEL), const, pipeline_mode=pl.Buffered(1)),
            pl.BlockSpec((D_MODEL, D_MODEL), const, pipeline_mode=pl.Buffered(1)),
            pl.BlockSpec((1, D_MODEL), const)],
        out_specs=pl.BlockSpec((1, tm, D_MODEL), row),
        compiler_params=_params("arbitrary", "arbitrary"),
        name="tail_prompt",
    )(oa, za, obg, qc, zc, mk, mv, gt, x, wpa, wpb, wpc, wout, gpost)


def _tail_sample_kernel(oa_ref, za_ref, ob_ref, zb_ref, oc_ref, zc_ref, gt_ref, x_ref,
                        wpa_ref, wpb_ref, wpc_ref, wout_ref, gpost_ref, y_ref):
    a_g = (oa_ref[...] * _silu(za_ref[...].astype(F32))).astype(BF16)
    b_g = (ob_ref[...] * _silu(zb_ref[...].astype(F32))).astype(BF16)
    c_g = (oc_ref[...] * _silu(zc_ref[...].astype(F32))).astype(BF16)
    y_ref[...] = _gated_tail(a_g, b_g, c_g, gt_ref, x_ref[...],
                             wpa_ref, wpb_ref, wpc_ref, wout_ref, gpost_ref)


def _tail_sample(oa, za, ob, zb, oc, zc, gt, x, wpa, wpb, wpc, wout, gpost, tm):
    M = x.shape[0]
    row = lambda i: (i, 0)
    const = lambda i: (0, 0)
    widths = (EA, EA, EB, EB, EC, EC, 3 * D_MODEL, D_MODEL)
    return pl.pallas_call(
        _tail_sample_kernel,
        out_shape=jax.ShapeDtypeStruct((M, D_MODEL), F32),
        grid=(M // tm,),
        in_specs=[pl.BlockSpec((tm, n), row) for n in widths] + [
            pl.BlockSpec((EA, D_MODEL), const), pl.BlockSpec((EB, D_MODEL), const),
            pl.BlockSpec((EC, D_MODEL), const), pl.BlockSpec((D_MODEL, D_MODEL), const),
            pl.BlockSpec((1, D_MODEL), const)],
        out_specs=pl.BlockSpec((tm, D_MODEL), row),
        compiler_params=_params("arbitrary"),
        name="tail_sample",
    )(oa, za, ob, zb, oc, zc, gt, x, wpa, wpb, wpc, wout, gpost)


def _sample_tables(rel_bias, wb, past, T):
    def mult_of(dl, real):
        m = np.zeros(dl.shape, np.float32)
        for (w, d) in PATTERNS:
            m += (real & (dl % d == 0) & (dl <= w)).astype(np.float32)
        return m

    def tables(dl, real):
        row = np.arange(QROWS)[:, None]
        real = real & (dl >= 0) & (past + row - dl >= 0) & (wb + row - dl >= 0)
        m = mult_of(dl, real)
        dlc = np.clip(dl, 0, MAX_DIST)
        m[T:] = m[0]
        bias = jnp.where(jnp.asarray(m > 0)[None], _bias_by_bucket(rel_bias, _t5_bucket_ids(dlc)), NEG)
        return bias, jnp.asarray(m)

    i = np.arange(QROWS)[:, None]
    pos = np.arange(wb)[None, :]
    cache_b, cache_m = tables(wb + i - pos, np.ones((QROWS, wb), bool))
    j = np.arange(QROWS)[None, :]
    new_b, new_m = tables(i - j, np.broadcast_to(j < T, (QROWS, QROWS)))
    return cache_b, cache_m, new_b, new_m


def _pad_rows(x, rows):
    return jnp.concatenate([x, jnp.zeros((rows - x.shape[0], x.shape[1]), x.dtype)], axis=0)


def _sample_heads(heads, qh_ref, knh_ref, vnh_ref, kt_ref, vt_ref, cb_ref, cm_ref, nb_ref, nm_ref, oh_ref):
    scores = []
    for h in heads:
        qh = qh_ref[h].astype(BF16)
        s = jnp.dot(qh, kt_ref[0, h].astype(BF16), preferred_element_type=F32) + cb_ref[h]
        sn = lax.dot_general(qh, knh_ref[h].astype(BF16), NT_DIMS, preferred_element_type=F32) + nb_ref[h]
        scores.append((s, sn))
    probs = []
    for s, sn in scores:
        m = jnp.maximum(jnp.max(s, axis=-1, keepdims=True), jnp.max(sn, axis=-1, keepdims=True))
        p = jnp.exp(s - m) * cm_ref[...]
        pn = jnp.exp(sn - m) * nm_ref[...]
        l = jnp.sum(p, axis=-1, keepdims=True) + jnp.sum(pn, axis=-1, keepdims=True)
        probs.append((p.astype(BF16), pn.astype(BF16), l))
    for h, (p, pn, l) in zip(heads, probs):
        o = lax.dot_general(p, vt_ref[0, h].astype(BF16), NT_DIMS, preferred_element_type=F32)
        o = o + jnp.dot(pn, vnh_ref[h].astype(BF16), preferred_element_type=F32)
        oh_ref[h] = o / l


def _sample_cross_attn(qc_ref, mk_ref, mv_ref, xm_ref, oc_ref, T):
    qc8 = _pad_rows(qc_ref[0], QROWS)
    qc_heads = jnp.concatenate([qc8[:, h * DC:(h + 1) * DC] for h in range(HC)], axis=0).astype(BF16)
    sc = lax.dot_general(qc_heads, mk_ref[0].astype(BF16), NT_DIMS, preferred_element_type=F32) + xm_ref[...]
    pc = jnp.exp(sc - jnp.max(sc, axis=-1, keepdims=True))
    lc = jnp.sum(pc, axis=-1, keepdims=True)
    oc = jnp.dot(pc.astype(BF16), mv_ref[0].astype(BF16), preferred_element_type=F32) / lc
    oc_ref[0] = jnp.concatenate([oc[h * QROWS:h * QROWS + T] for h in range(HC)], axis=-1)


def _attn_tail_kernel(q_ref, kt_ref, vt_ref, kn_ref, vn_ref, tb_ref, tm_ref, nb_ref, nm_ref,
                      qc_ref, mk_ref, mv_ref, xm_ref,
                      oa_ref, za_ref, obg_ref, pqc_ref, pzc_ref, pmk_ref, pmv_ref, gt_ref, x_ref,
                      wpa_ref, wpb_ref, wpc_ref, wout_ref, gpost_ref,
                      y_ref, oas_ref, ocs_ref,
                      qh_ref, knh_ref, vnh_ref, oh_ref, *, T):
    heads = [slice(h * DC, (h + 1) * DC) for h in range(HC)]

    def gate(k):
        return _sigmoid(gt_ref[0, :, k * D_MODEL:(k + 1) * D_MODEL].astype(F32))

    q8, kn8, vn8 = _pad_rows(q_ref[0], QROWS), _pad_rows(kn_ref[0], QROWS), _pad_rows(vn_ref[0], QROWS)
    for h in range(HA):
        hs = slice(h * DA, (h + 1) * DA)
        qh_ref[h], knh_ref[h], vnh_ref[h] = q8[:, hs], kn8[:, hs], vn8[:, hs]
    scores = [lax.dot_general(pqc_ref[0, :, hs], pmk_ref[0, :, hs], NT_DIMS, preferred_element_type=F32)
              for hs in heads]
    a_g = jnp.concatenate(
        [(oa_ref[0, c].astype(F32) * _silu(za_ref[0, :, c * 128:(c + 1) * 128].astype(F32))).astype(BF16)
         for c in range(NHP)], axis=-1)
    pa = jnp.dot(a_g, wpa_ref[...], preferred_element_type=F32)
    pb = jnp.dot(obg_ref[0], wpb_ref[...], preferred_element_type=F32)
    _sample_cross_attn(qc_ref, mk_ref, mv_ref, xm_ref, ocs_ref, T)
    _sample_heads(list(range(HA)), qh_ref, knh_ref, vnh_ref, kt_ref, vt_ref,
                  tb_ref, tm_ref, nb_ref, nm_ref, oh_ref)
    pieces = []
    for s, hs in zip(scores, heads):
        p = jnp.exp(s - jnp.max(s, axis=-1, keepdims=True))
        l = jnp.sum(p, axis=-1, keepdims=True)
        oc = jnp.dot(p.astype(BF16), pmv_ref[0, :, hs], preferred_element_type=F32) / l
        pieces.append((oc * _silu(pzc_ref[0, :, hs].astype(F32))).astype(BF16))
    pc = jnp.dot(jnp.concatenate(pieces, axis=-1), wpc_ref[...], preferred_element_type=F32)
    mix = gate(0) * pa + gate(1) * pb + gate(2) * pc
    z = jnp.dot(mix.astype(BF16), wout_ref[...], preferred_element_type=F32)
    y_ref[0] = x_ref[0] + _rmsnorm_f32(z, gpost_ref[...])
    oas_ref[0] = jnp.concatenate([oh_ref[h][:T] for h in range(HA)], axis=-1)


def _prompt_tail_sample_attn(q, k_new, v_new, cache_kt, cache_vt, tables, qc, mem_k, mem_v,
                             oa, za, obg, pqc, pzc, pmk, pmv, gt, x, wpa, wpb, wpc, wout, gpost):
    B, S, _ = x.shape
    Bd, T, _ = q.shape
    wb = cache_kt.shape[-1]
    tm = (B * S) // Bd
    tiles_per_seq = S // tm
    assert B * tiles_per_seq == Bd
    cache_b, cache_m, new_b, new_m = tables
    own_head = jnp.asarray(np.where(
        np.arange(HC * QROWS)[:, None] // QROWS == np.arange(N_MEM * HC)[None, :] % HC, 0.0, NEG
    ).astype(np.float32))
    tile = lambda g: (g // tiles_per_seq, g % tiles_per_seq, 0)
    per_b3 = lambda g: (g, 0, 0)
    per_b4 = lambda g: (g, 0, 0, 0)
    c2 = lambda g: (0, 0)
    c3 = lambda g: (0, 0, 0)
    head_scratch = pltpu.VMEM((HA, QROWS, DA), F32)
    pair_tile = lambda g: (g // tiles_per_seq, 0, g % tiles_per_seq, 0)
    per_seq = lambda g: (g // tiles_per_seq, 0, 0)
    once = pl.Buffered(1)
    return pl.pallas_call(
        functools.partial(_attn_tail_kernel, T=T),
        out_shape=(jax.ShapeDtypeStruct((B, S, D_MODEL), F32),
                   jax.ShapeDtypeStruct((Bd, T, EA), F32), jax.ShapeDtypeStruct((Bd, T, EC), F32)),
        grid=(Bd,),
        in_specs=[pl.BlockSpec((1, T, EA), per_b3),
                  pl.BlockSpec((1, HA, DA, wb), per_b4), pl.BlockSpec((1, HA, DA, wb), per_b4),
                  pl.BlockSpec((1, T, EA), per_b3), pl.BlockSpec((1, T, EA), per_b3),
                  pl.BlockSpec((HA, QROWS, wb), c3), pl.BlockSpec((QROWS, wb), c2),
                  pl.BlockSpec((HA, QROWS, QROWS), c3), pl.BlockSpec((QROWS, QROWS), c2),
                  pl.BlockSpec((1, T, EC), per_b3),
                  pl.BlockSpec((1, N_MEM * HC, DC), per_b3), pl.BlockSpec((1, N_MEM * HC, DC), per_b3),
                  pl.BlockSpec((HC * QROWS, N_MEM * HC), c2),
                  pl.BlockSpec((1, NHP, tm, 128), pair_tile), pl.BlockSpec((1, tm, EA), tile),
                  pl.BlockSpec((1, tm, EB), tile),
                  pl.BlockSpec((1, tm, EC), tile), pl.BlockSpec((1, tm, EC), tile),
                  pl.BlockSpec((1, N_MEM, EC), per_seq), pl.BlockSpec((1, N_MEM, EC), per_seq),
                  pl.BlockSpec((1, tm, 3 * D_MODEL), tile), pl.BlockSpec((1, tm, D_MODEL), tile),
                  pl.BlockSpec((EA, D_MODEL), c2, pipeline_mode=once),
                  pl.BlockSpec((EB, D_MODEL), c2, pipeline_mode=once),
                  pl.BlockSpec((EC, D_MODEL), c2, pipeline_mode=once),
                  pl.BlockSpec((D_MODEL, D_MODEL), c2, pipeline_mode=once),
                  pl.BlockSpec((1, D_MODEL), c2)],
        out_specs=(pl.BlockSpec((1, tm, D_MODEL), tile),
                   pl.BlockSpec((1, T, EA), per_b3), pl.BlockSpec((1, T, EC), per_b3)),
        scratch_shapes=[head_scratch, head_scratch, head_scratch, head_scratch],
        compiler_params=_params("arbitrary"),
        name="prompt_tail_sample_attn",
    )(q, cache_kt, cache_vt, k_new, v_new, cache_b, cache_m, new_b, new_m, qc, mem_k, mem_v, own_head,
      oa, za, obg, pqc, pzc, pmk, pmv, gt, x, wpa, wpb, wpc, wout, gpost)


def _conv_sample_kernel(st_ref, glu_ref, cw_ref, cb_ref, lg_ref, lb_ref, o_ref, ns_ref, *, T, rc):
    n_hist = CONV_W - 1
    Bd = st_ref.shape[1]

    def shift(t, carry):
        ns_ref[t] = st_ref[t + T]
        return carry
    lax.fori_loop(0, n_hist - T, shift, 0)
    for i in range(T):
        ns_ref[n_hist - T + i] = glu_ref[i]
    for i in range(T):
        def chunk(c, carry, i=i):
            rows = pl.ds(pl.multiple_of(c * rc, rc), rc)
            acc = jnp.zeros((rc, EB), F32)
            for j in range(CONV_W):
                src = st_ref[i + j, rows, :] if i + j < n_hist else glu_ref[i + j - n_hist, rows, :]
                acc = acc + src * cw_ref[j:j + 1, :]
            o_ref[i, rows, :] = _layernorm_silu(acc + cb_ref[...], lg_ref[...], lb_ref[...])
            return carry
        lax.fori_loop(0, Bd // rc, chunk, 0)


def _conv_sample(state_t, glu_t, cw, cb, lg, lb, rc):
    T, Bd, _ = glu_t.shape
    return pl.pallas_call(
        functools.partial(_conv_sample_kernel, T=T, rc=rc),
        out_shape=(jax.ShapeDtypeStruct((T, Bd, EB), F32), jax.ShapeDtypeStruct(state_t.shape, F32)),
        compiler_params=pltpu.CompilerParams(vmem_limit_bytes=V7X_VMEM_LIMIT),
        name="conv_sample",
    )(state_t, glu_t, cw, cb, lg, lb)


def kernel(x_prompt, x_sample, mem_prompt, cache_k_win, cache_v_win, state_conv, cache_k_mem, cache_v_mem,
           rel_bias, g_pre, w_in, g_mem, w_mem_kv, conv_w, conv_b, ln_g, ln_b, w_proj_a, w_proj_b,
           w_proj_c, w_out, g_post):
    depth = g_pre.shape[0]
    assert depth == 1, "single-layer step"
    B, S, _ = x_prompt.shape
    Bd, T, _ = x_sample.shape
    wb = cache_k_win.shape[2]
    past = wb
    assert wb == MAX_DIST and S % (16 * QB) == 0 and T <= QROWS

    l = 0
    w_in_b = w_in[l].astype(BF16)
    wpa, wpb, wpc = w_proj_a[l].astype(BF16), w_proj_b[l].astype(BF16), w_proj_c[l].astype(BF16)
    wout = w_out[l].astype(BF16)
    gpre, gpost = g_pre[l][None], g_post[l][None]
    cb, lg, lb = conv_b[l][None], ln_g[l][None], ln_b[l][None]
    cw = conv_w[l]

    cw_tiles = jnp.broadcast_to(cw[:, None, :], (CONV_W, 8, EB))
    (qkv, k_keep, v_keep, za, obg, qc, zc, gt, glu_tail,
     q_s, k_s, v_s, za_s, glu_s, zb_s, qc_s, zc_s, gt_s) = _inproj(
        x_prompt, x_sample.reshape(Bd * T, D_MODEL), gpre, w_in_b, cw_tiles, cb, lg, lb,
        tm=256, rc=32, rn=64)
    mk_f, mv_f, mk_b, mv_b = _memkv(mem_prompt.reshape(B * N_MEM, D_MODEL), g_mem[l][None],
                                    w_mem_kv[l].astype(BF16), tm=512)
    oa = _band_attn(qkv, _band_tables(rel_bias))

    cache_kt = jnp.transpose(cache_k_win[l], (0, 2, 3, 1))
    cache_vt = jnp.transpose(cache_v_win[l], (0, 2, 3, 1))
    y_p, oa_s, oc_s = _prompt_tail_sample_attn(
        q_s.reshape(Bd, T, EA), k_s.reshape(Bd, T, EA), v_s.reshape(Bd, T, EA),
        cache_kt, cache_vt, _sample_tables(rel_bias, wb, past, T),
        qc_s.reshape(Bd, T, EC), cache_k_mem[l].reshape(Bd, N_MEM * HC, DC),
        cache_v_mem[l].reshape(Bd, N_MEM * HC, DC),
        oa, za, obg, qc, zc, mk_b.reshape(B, N_MEM, EC), mv_b.reshape(B, N_MEM, EC), gt, x_prompt,
        wpa, wpb, wpc, wout, gpost)
    state_t = jnp.transpose(state_conv[l], (1, 0, 2))
    glu_t = jnp.transpose(glu_s.reshape(Bd, T, EB), (1, 0, 2))
    ob_t, new_state_t = _conv_sample(state_t, glu_t, cw, cb, lg, lb, rc=32)
    ob_s = jnp.transpose(ob_t, (1, 0, 2)).reshape(Bd * T, EB)
    y_s = _tail_sample(oa_s.reshape(Bd * T, EA), za_s, ob_s, zb_s,
                       oc_s.reshape(Bd * T, EC), zc_s, gt_s, x_sample.reshape(Bd * T, D_MODEL),
                       wpa, wpb, wpc, wout, gpost, tm=Bd * T)

    conv_state_s = jnp.transpose(new_state_t, (1, 0, 2))
    return (y_p, y_s.reshape(Bd, T, D_MODEL),
            k_keep.reshape(1, B, wb, HA, DA), v_keep.reshape(1, B, wb, HA, DA),
            glu_tail[:, HALO - (CONV_W - 1):][None],
            mk_f.reshape(1, B, N_MEM, HC, DC), mv_f.reshape(1, B, N_MEM, HC, DC),
            k_s.reshape(1, Bd, T, HA, DA), v_s.reshape(1, Bd, T, HA, DA),
            conv_state_s[None])
```

```python
import functools

import jax
import jax.numpy as jnp
import numpy as np
from jax import lax
from jax.experimental import pallas as pl
from jax.experimental.pallas import tpu as pltpu

F32 = jnp.float32
BF16 = jnp.bfloat16

D_MODEL = 1024
HA, DA = 12, 64
EA = HA * DA
NHP = HA // 2
PATTERNS = ((128, 1), (512, 4), (2048, 16))
NW = 128
QB = 128
EB = 768
CONV_W = 31
HC, DC = 4, 128
EC = HC * DC
N_MEM = 256
N_BUCKETS = 32
MAX_DIST = 2048
EPS = 1e-6
NEG = -1e30
LOG2E = 1.4426950408889634

C_Q, C_K, C_V, C_ZA = 0, EA, 2 * EA, 3 * EA
C_U = 4 * EA
C_G = C_U + EB
C_ZB = C_U + 2 * EB
C_QC = C_ZB + EB
C_ZC = C_QC + EC
C_GT = C_ZC + EC
IN_COLS = C_GT + 3 * D_MODEL

V7X_VMEM_LIMIT = 56 * 1024 * 1024
HALO = 32
QROWS = 8
NT_DIMS = (((1,), (1,)), ((), ()))

TM_INPROJ = 256
TM_MEMKV = 512
CONV_TAP_ROWS = 32
CONV_NORM_ROWS = 64


def _t5_bucket_ids(n):
    exact = N_BUCKETS // 2
    nf = np.maximum(n, 1).astype(np.float32)
    scale = np.float32(N_BUCKETS - exact) / np.log(np.float32(MAX_DIST) / np.float32(exact))
    large = exact + (np.log(nf / np.float32(exact)) * scale).astype(np.int32)
    large = np.minimum(large, N_BUCKETS - 1)
    return np.where(n < exact, n, large).astype(np.int32)


def _bias_by_bucket(rel_bias, ids):
    onehot = (ids[None] == np.arange(N_BUCKETS).reshape((-1,) + (1,) * ids.ndim)).astype(np.float32)
    return jnp.tensordot(rel_bias.astype(F32).T, jnp.asarray(onehot), axes=1,
                         precision=lax.Precision.HIGHEST)


def _sigmoid(x):
    return 1.0 / (1.0 + jnp.exp(-x))


def _silu(x):
    return x * _sigmoid(x)


def _rmsnorm_f32(x, g):
    return x * lax.rsqrt(jnp.mean(x * x, axis=-1, keepdims=True) + EPS) * g


def _params(*sem):
    return pltpu.CompilerParams(dimension_semantics=sem, vmem_limit_bytes=V7X_VMEM_LIMIT)


def _inproj_columns(xn, w_ref):
    def mm(lo, n):
        return jnp.dot(xn, w_ref[:, lo:lo + n], preferred_element_type=F32)
    return mm


def _inproj_kernel(xp_ref, xs_ref, g_ref, w_ref,
                   qkv_ref, kf_ref, vf_ref, za_ref, glu_ref, zb_ref, qc_ref, zc_ref, gt_ref, tail_ref,
                   sq_ref, sk_ref, sv_ref, sza_ref, sglu_ref, szb_ref, sqc_ref, szc_ref, sgt_ref,
                   *, prompt_steps):
    g = pl.program_id(0)
    sample_refs = (sq_ref, sk_ref, sv_ref, sza_ref, sglu_ref, szb_ref, sqc_ref, szc_ref, sgt_ref)

    @pl.when(g == 0)
    def _():
        for r in sample_refs:
            r[...] = jnp.zeros(r.shape, r.dtype)

    @pl.when(g < prompt_steps)
    def _():
        mm = _inproj_columns(_rmsnorm_f32(xp_ref[0], g_ref[...]).astype(BF16), w_ref)

        def put_pairs(res, base):
            for c in range(NHP):
                qkv_ref[0, base + c] = res[:, c * 128:(c + 1) * 128].astype(BF16)

        put_pairs(mm(C_Q, EA) * (DA ** -0.5 * LOG2E), 0)
        k = mm(C_K, EA)
        put_pairs(k, NHP)
        kf_ref[0] = k
        v = mm(C_V, EA)
        put_pairs(v, 2 * NHP)
        vf_ref[0] = v
        za_ref[0] = mm(C_ZA, EA).astype(BF16)
        glu = mm(C_U, EB) * _sigmoid(mm(C_G, EB))
        glu_ref[0] = glu.astype(BF16)
        tail_ref[0] = glu[glu.shape[0] - HALO:, :]
        zb_ref[0] = mm(C_ZB, EB).astype(BF16)
        qc_ref[0] = (mm(C_QC, EC) * (DC ** -0.5)).astype(BF16)
        zc_ref[0] = mm(C_ZC, EC).astype(BF16)
        for c in range(3):
            gt_ref[0, :, c * D_MODEL:(c + 1) * D_MODEL] = mm(C_GT + c * D_MODEL, D_MODEL).astype(BF16)

    @pl.when(g >= prompt_steps)
    def _():
        mm = _inproj_columns(_rmsnorm_f32(xs_ref[...], g_ref[...]).astype(BF16), w_ref)
        sq_ref[...] = mm(C_Q, EA) * (DA ** -0.5)
        sk_ref[...] = mm(C_K, EA)
        sv_ref[...] = mm(C_V, EA)
        sza_ref[...] = mm(C_ZA, EA).astype(BF16)
        sglu_ref[...] = mm(C_U, EB) * _sigmoid(mm(C_G, EB))
        szb_ref[...] = mm(C_ZB, EB).astype(BF16)
        sqc_ref[...] = mm(C_QC, EC) * (DC ** -0.5)
        szc_ref[...] = mm(C_ZC, EC).astype(BF16)
        for c in range(3):
            sgt_ref[:, c * D_MODEL:(c + 1) * D_MODEL] = mm(C_GT + c * D_MODEL, D_MODEL).astype(BF16)


def _inproj(x, xs, g, w, tm):
    B, S, _ = x.shape
    Ms = xs.shape[0]
    nt = S // tm
    P = B * nt
    wb = min(MAX_DIST, S)
    first_kept = (S - wb) // tm
    pb = lambda s: jnp.minimum(s, P - 1) // nt
    pi = lambda s: jnp.minimum(s, P - 1) % nt
    row = lambda s: (pb(s), pi(s), 0)
    kept = lambda s: (pb(s), jnp.maximum(pi(s) - first_kept, 0), 0)
    srow = lambda s: (jnp.maximum(s - P, 0), 0)
    const = lambda s: (0, 0)
    s_widths = (EA, EA, EA, EA, EB, EB, EC, EC, 3 * D_MODEL)
    s_dtypes = (F32, F32, F32, BF16, F32, BF16, F32, BF16, BF16)
    out_shape = (
        jax.ShapeDtypeStruct((B, 3 * NHP, S, 128), BF16),
        jax.ShapeDtypeStruct((B, wb, EA), F32),
        jax.ShapeDtypeStruct((B, wb, EA), F32),
        jax.ShapeDtypeStruct((B, S, EA), BF16),
        jax.ShapeDtypeStruct((B, S, EB), BF16),
        jax.ShapeDtypeStruct((B, S, EB), BF16),
        jax.ShapeDtypeStruct((B, S, EC), BF16),
        jax.ShapeDtypeStruct((B, S, EC), BF16),
        jax.ShapeDtypeStruct((B, S, 3 * D_MODEL), BF16),
        jax.ShapeDtypeStruct((B, HALO, EB), F32),
    ) + tuple(jax.ShapeDtypeStruct((Ms, n), dt) for n, dt in zip(s_widths, s_dtypes))
    out_specs = (
        pl.BlockSpec((1, 3 * NHP, tm, 128), lambda s: (pb(s), 0, pi(s), 0)),
        pl.BlockSpec((1, tm, EA), kept),
        pl.BlockSpec((1, tm, EA), kept),
        pl.BlockSpec((1, tm, EA), row),
        pl.BlockSpec((1, tm, EB), row),
        pl.BlockSpec((1, tm, EB), row),
        pl.BlockSpec((1, tm, EC), row),
        pl.BlockSpec((1, tm, EC), row),
        pl.BlockSpec((1, tm, 3 * D_MODEL), row),
        pl.BlockSpec((1, HALO, EB), lambda s: (pb(s), 0, 0)),
    ) + tuple(pl.BlockSpec((tm, n), srow) for n in s_widths)
    return pl.pallas_call(
        functools.partial(_inproj_kernel, prompt_steps=P),
        out_shape=out_shape,
        grid=(P + Ms // tm,),
        in_specs=[
            pl.BlockSpec((1, tm, D_MODEL), row),
            pl.BlockSpec((tm, D_MODEL), srow),
            pl.BlockSpec((1, D_MODEL), const),
            pl.BlockSpec((D_MODEL, IN_COLS), const, pipeline_mode=pl.Buffered(1)),
        ],
        out_specs=out_specs,
        compiler_params=_params("arbitrary"),
        name="inproj",
    )(x, xs, g, w)


def _memkv_kernel(m_ref, g_ref, w_ref, kf_ref, vf_ref, kb_ref, vb_ref):
    xn = _rmsnorm_f32(m_ref[...], g_ref[...]).astype(BF16)
    k = jnp.dot(xn, w_ref[:, :EC], preferred_element_type=F32)
    v = jnp.dot(xn, w_ref[:, EC:], preferred_element_type=F32)
    kf_ref[...] = k
    vf_ref[...] = v
    kb_ref[...] = k.astype(BF16)
    vb_ref[...] = v.astype(BF16)


def _memkv(mem, g, w, tm):
    M = mem.shape[0]
    row = lambda i: (i, 0)
    return pl.pallas_call(
        _memkv_kernel,
        out_shape=(jax.ShapeDtypeStruct((M, EC), F32), jax.ShapeDtypeStruct((M, EC), F32),
                   jax.ShapeDtypeStruct((M, EC), BF16), jax.ShapeDtypeStruct((M, EC), BF16)),
        grid=(M // tm,),
        in_specs=[pl.BlockSpec((tm, D_MODEL), row),
                  pl.BlockSpec((1, D_MODEL), lambda i: (0, 0)),
                  pl.BlockSpec((D_MODEL, 2 * EC), lambda i: (0, 0))],
        out_specs=tuple(pl.BlockSpec((tm, EC), row) for _ in range(4)),
        compiler_params=_params("arbitrary"),
        name="memkv",
    )(mem, g, w)


def _band_tables(rel_bias):
    iq = np.arange(QB)[:, None]
    ik = np.arange(2 * QB)[None, :]
    dist = iq - ik + QB
    band = (dist >= 0) & (dist <= NW)
    out = []
    for (_, d) in PATTERNS:
        bias = LOG2E * _bias_by_bucket(rel_bias, _t5_bucket_ids(np.clip(dist, 0, NW) * d))
        later = jnp.where(jnp.asarray(band)[None], bias, NEG)
        first = jnp.where(jnp.asarray(band & (ik >= QB))[None], bias, NEG)
        t = jnp.stack([first, later], axis=1)
        out.append(t.reshape(NHP, 2, 2, QB, 2 * QB).transpose(0, 2, 1, 3, 4)
                   .reshape(NHP, 2, 2 * QB, 2 * QB))
    return jnp.stack(out)


def _band_attn_kernel(q_ref, k_ref, v_ref, t_ref, oa_ref,
                      nat32, p4_32, p4_ref, p16_ref, a4_ref, l4_ref, m4_ref, a16_ref, l16_ref, m16_ref,
                      *, S):
    n_units = S // QB
    lane = lax.broadcasted_iota(jnp.int32, (QB, 128), 1)
    head0 = lane < DA
    keep0 = head0.astype(F32).astype(BF16)
    keep1 = (1.0 - head0.astype(F32)).astype(BF16)
    ones = jnp.ones((2 * QB, 128), BF16)
    n4 = S // 4

    for t, src in enumerate((q_ref, k_ref, v_ref)):
        def widen(c, carry, src=src):
            r0 = pl.multiple_of(c * 256, 256)
            nat32[pl.ds(r0, 256), :] = src[0, 0, pl.ds(r0, 256), :].astype(F32)
            return carry
        lax.fori_loop(0, S // 256, widen, 0)
        for r in range(4):
            def by4(c, carry, r=r, t=t):
                u0 = pl.multiple_of(c * 256, 256)
                x = nat32[pl.ds(4 * u0 + r, 256, stride=4), :]
                p4_32[pl.ds(r * n4 + u0, 256), :] = x
                p4_ref[t, pl.ds(r * n4 + u0, 256), :] = x.astype(BF16)
                return carry
            lax.fori_loop(0, n4 // 256, by4, 0)
        for r4 in range(4):
            for s in range(4):
                x = p4_32[pl.ds(r4 * n4 + s, n4 // 4, stride=4), :]
                p16_ref[t, pl.ds((4 * s + r4) * (n4 // 4), n4 // 4), :] = x.astype(BF16)

    def unit(u, pat, d, qsrc, ksrc, vsrc):
        nqb = n_units // d
        row = pl.multiple_of(u * QB, QB)
        prow = pl.multiple_of(jnp.maximum(u - 1, 0) * QB, QB)
        q2 = qsrc[pl.ds(row, QB), :]
        kk = jnp.concatenate([ksrc[pl.ds(prow, QB), :], ksrc[pl.ds(row, QB), :]], axis=0)
        vv = jnp.concatenate([vsrc[pl.ds(prow, QB), :], vsrc[pl.ds(row, QB), :]], axis=0)
        qs = jnp.concatenate([q2 * keep0, q2 * keep1], axis=0)
        s = lax.dot_general(qs, kk, NT_DIMS, preferred_element_type=F32)
        s = s + t_ref[pat, 0, jnp.minimum(u & (nqb - 1), 1)]
        m = jnp.max(s, axis=-1, keepdims=True)
        p = jnp.exp2(s - m).astype(BF16)
        oe = jnp.dot(p, jnp.concatenate([vv, ones], axis=1), preferred_element_type=F32)
        mb = jnp.broadcast_to(m, (2 * QB, 128))
        return (jnp.where(head0, oe[:QB, :128], oe[QB:, :128]),
                jnp.where(head0, oe[:QB, 128:], oe[QB:, 128:]),
                jnp.where(head0, mb[:QB], mb[QB:]))

    def strided_pattern(pat, d, src, acc_ref, den_ref, max_ref):
        nqb = n_units // d

        def body(u, carry):
            acc, den, mx = unit(u, pat, d, src.at[0], src.at[1], src.at[2])
            start = u // nqb + (u & (nqb - 1)) * (d * QB)
            acc_ref[pl.ds(start, QB, stride=d), :] = acc
            den_ref[pl.ds(start, QB, stride=d), :] = den
            max_ref[pl.ds(start, QB, stride=d), :] = mx
            return carry
        lax.fori_loop(0, n_units, body, 0, unroll=n_units)

    strided_pattern(2, 16, p16_ref, a16_ref, l16_ref, m16_ref)
    strided_pattern(1, 4, p4_ref, a4_ref, l4_ref, m4_ref)

    def dense(u, carry):
        a1, l1, m1 = unit(u, 0, 1, q_ref.at[0, 0], k_ref.at[0, 0], v_ref.at[0, 0])
        rows = pl.ds(pl.multiple_of(u * QB, QB), QB)
        m4, m16 = m4_ref[rows, :], m16_ref[rows, :]
        mx = jnp.maximum(jnp.maximum(m1, m4), m16)
        e1, e4, e16 = jnp.exp2(m1 - mx), jnp.exp2(m4 - mx), jnp.exp2(m16 - mx)
        num = a1 * e1 + a4_ref[rows, :] * e4 + a16_ref[rows, :] * e16
        den = l1 * e1 + l4_ref[rows, :] * e4 + l16_ref[rows, :] * e16
        oa_ref[0, 0, rows, :] = (num / den).astype(BF16)
        return carry
    lax.fori_loop(0, n_units, dense, 0, unroll=n_units)


def _band_attn(qkv, tables):
    B, _, S, _ = qkv.shape
    blk = (1, 1, S, 128)
    return pl.pallas_call(
        functools.partial(_band_attn_kernel, S=S),
        out_shape=jax.ShapeDtypeStruct((B, NHP, S, 128), BF16),
        grid=(B, NHP),
        in_specs=[pl.BlockSpec(blk, lambda b, h: (b, h, 0, 0)),
                  pl.BlockSpec(blk, lambda b, h: (b, NHP + h, 0, 0)),
                  pl.BlockSpec(blk, lambda b, h: (b, 2 * NHP + h, 0, 0)),
                  pl.BlockSpec((3, 1, 2, 2 * QB, 2 * QB), lambda b, h: (0, h, 0, 0, 0))],
        out_specs=pl.BlockSpec(blk, lambda b, h: (b, h, 0, 0)),
        scratch_shapes=[pltpu.VMEM((S, 128), F32), pltpu.VMEM((S, 128), F32),
                        pltpu.VMEM((3, S, 128), BF16), pltpu.VMEM((3, S, 128), BF16)]
                       + [pltpu.VMEM((S, 128), F32)] * 6,
        compiler_params=_params("arbitrary", "arbitrary"),
        name="band_attn",
    )(qkv, qkv, qkv, tables)


def _layernorm_silu(c, g, b):
    mu = jnp.mean(c, axis=-1, keepdims=True)
    cc = c - mu
    var = jnp.mean(cc * cc, axis=-1, keepdims=True)
    return _silu(cc * lax.rsqrt(var + EPS) * g + b)


FIRST_TAP = HALO - (CONV_W - 1)


def _conv_stage(glu_ref, halo_ref, seq_start, full_ref, tm):
    halo = halo_ref[0].astype(F32)
    full_ref[0, 0:HALO, :] = jnp.where(seq_start, jnp.zeros_like(halo), halo)
    full_ref[0, HALO:HALO + tm, :] = glu_ref[0].astype(F32)
    n_shift = HALO + tm - 8
    for s in range(1, 8):
        full_ref[s, 0:n_shift, :] = full_ref[0, s:s + n_shift, :]


def _conv_taps(c, full_ref, cw_ref, conv_ref, rc):
    base = c * rc
    accs = [jnp.zeros((8, EB), F32) for _ in range(rc // 8)]
    for j in range(CONV_W):
        off = FIRST_TAP + j
        w8 = cw_ref[j]
        for g in range(rc // 8):
            accs[g] = accs[g] + full_ref[off % 8, pl.ds(base + (off // 8) * 8 + 8 * g, 8), :] * w8
    conv_ref[pl.ds(base, rc), :] = jnp.concatenate(accs, axis=0)


def _conv_norm_gate(c, conv_ref, zb_ref, cb_ref, lg_ref, lb_ref, o_ref, rn):
    rows = pl.ds(c * rn, rn)
    ob = _layernorm_silu(conv_ref[rows, :] + cb_ref[...], lg_ref[...], lb_ref[...])
    o_ref[rows, :] = (ob * _silu(zb_ref[0, rows, :].astype(F32))).astype(BF16)


def _gated_tail(a_g, b_g, c_g, gt_ref, x, wpa_ref, wpb_ref, wpc_ref, wout_ref, gpost_ref):
    def gate(k):
        return _sigmoid(gt_ref[:, k * D_MODEL:(k + 1) * D_MODEL].astype(F32))

    mix = gate(0) * jnp.dot(a_g, wpa_ref[...], preferred_element_type=F32)
    mix = mix + gate(1) * jnp.dot(b_g, wpb_ref[...], preferred_element_type=F32)
    mix = mix + gate(2) * jnp.dot(c_g, wpc_ref[...], preferred_element_type=F32)
    z = jnp.dot(mix.astype(BF16), wout_ref[...], preferred_element_type=F32)
    return x + _rmsnorm_f32(z, gpost_ref[...])


def _tail_sample_kernel(oa_ref, za_ref, ob_ref, zb_ref, oc_ref, zc_ref, gt_ref, x_ref,
                        wpa_ref, wpb_ref, wpc_ref, wout_ref, gpost_ref, y_ref):
    a_g = (oa_ref[...] * _silu(za_ref[...].astype(F32))).astype(BF16)
    b_g = (ob_ref[...] * _silu(zb_ref[...].astype(F32))).astype(BF16)
    c_g = (oc_ref[...] * _silu(zc_ref[...].astype(F32))).astype(BF16)
    y_ref[...] = _gated_tail(a_g, b_g, c_g, gt_ref, x_ref[...],
                             wpa_ref, wpb_ref, wpc_ref, wout_ref, gpost_ref)


def _tail_sample(oa, za, ob, zb, oc, zc, gt, x, wpa, wpb, wpc, wout, gpost, tm):
    M = x.shape[0]
    row = lambda i: (i, 0)
    const = lambda i: (0, 0)
    widths = (EA, EA, EB, EB, EC, EC, 3 * D_MODEL, D_MODEL)
    return pl.pallas_call(
        _tail_sample_kernel,
        out_shape=jax.ShapeDtypeStruct((M, D_MODEL), F32),
        grid=(M // tm,),
        in_specs=[pl.BlockSpec((tm, n), row) for n in widths] + [
            pl.BlockSpec((EA, D_MODEL), const), pl.BlockSpec((EB, D_MODEL), const),
            pl.BlockSpec((EC, D_MODEL), const), pl.BlockSpec((D_MODEL, D_MODEL), const),
            pl.BlockSpec((1, D_MODEL), const)],
        out_specs=pl.BlockSpec((tm, D_MODEL), row),
        compiler_params=_params("arbitrary"),
        name="tail_sample",
    )(oa, za, ob, zb, oc, zc, gt, x, wpa, wpb, wpc, wout, gpost)


def _sample_tables(rel_bias, wb, past, T):
    def mult_of(dl, real):
        m = np.zeros(dl.shape, np.float32)
        for (w, d) in PATTERNS:
            m += (real & (dl % d == 0) & (dl <= w)).astype(np.float32)
        return m

    def tables(dl, real):
        row = np.arange(QROWS)[:, None]
        real = real & (dl >= 0) & (past + row - dl >= 0) & (wb + row - dl >= 0)
        m = mult_of(dl, real)
        dlc = np.clip(dl, 0, MAX_DIST)
        m[T:] = m[0]
        bias = jnp.where(jnp.asarray(m > 0)[None], _bias_by_bucket(rel_bias, _t5_bucket_ids(dlc)), NEG)
        return bias, jnp.asarray(m)

    i = np.arange(QROWS)[:, None]
    pos = np.arange(wb)[None, :]
    cache_b, cache_m = tables(wb + i - pos, np.ones((QROWS, wb), bool))
    j = np.arange(QROWS)[None, :]
    new_b, new_m = tables(i - j, np.broadcast_to(j < T, (QROWS, QROWS)))
    return cache_b, cache_m, new_b, new_m


def _pad_rows(x, rows):
    return jnp.concatenate([x, jnp.zeros((rows - x.shape[0], x.shape[1]), x.dtype)], axis=0)


def _sample_heads(heads, qh_ref, knh_ref, vnh_ref, kt_ref, vt_ref, cb_ref, cm_ref, nb_ref, nm_ref, oh_ref):
    scores = []
    for h in heads:
        qh = qh_ref[h].astype(BF16)
        s = jnp.dot(qh, kt_ref[0, h].astype(BF16), preferred_element_type=F32) + cb_ref[h]
        sn = lax.dot_general(qh, knh_ref[h].astype(BF16), NT_DIMS, preferred_element_type=F32) + nb_ref[h]
        scores.append((s, sn))
    probs = []
    for s, sn in scores:
        m = jnp.maximum(jnp.max(s, axis=-1, keepdims=True), jnp.max(sn, axis=-1, keepdims=True))
        p = jnp.exp(s - m) * cm_ref[...]
        pn = jnp.exp(sn - m) * nm_ref[...]
        l = jnp.sum(p, axis=-1, keepdims=True) + jnp.sum(pn, axis=-1, keepdims=True)
        probs.append((p.astype(BF16), pn.astype(BF16), l))
    for h, (p, pn, l) in zip(heads, probs):
        o = lax.dot_general(p, vt_ref[0, h].astype(BF16), NT_DIMS, preferred_element_type=F32)
        o = o + jnp.dot(pn, vnh_ref[h].astype(BF16), preferred_element_type=F32)
        oh_ref[h] = o / l


def _sample_cross_attn(qc_ref, mk_ref, mv_ref, xm_ref, oc_ref, T):
    qc8 = _pad_rows(qc_ref[0], QROWS)
    qc_heads = jnp.concatenate([qc8[:, h * DC:(h + 1) * DC] for h in range(HC)], axis=0).astype(BF16)
    sc = lax.dot_general(qc_heads, mk_ref[0].astype(BF16), NT_DIMS, preferred_element_type=F32) + xm_ref[...]
    pc = jnp.exp(sc - jnp.max(sc, axis=-1, keepdims=True))
    lc = jnp.sum(pc, axis=-1, keepdims=True)
    oc = jnp.dot(pc.astype(BF16), mv_ref[0].astype(BF16), preferred_element_type=F32) / lc
    oc_ref[0] = jnp.concatenate([oc[h * QROWS:h * QROWS + T] for h in range(HC)], axis=-1)


def _conv_attn_tail_kernel(glu_ref, halo_ref, zb_ref, cw_ref, cb_ref, lg_ref, lb_ref,
                           q_ref, kt_ref, vt_ref, kn_ref, vn_ref, tb_ref, tm_ref, nb_ref, nm_ref,
                           qc_ref, mk_ref, mv_ref, xm_ref,
                           oa_ref, za_ref, pqc_ref, pzc_ref, pmk_ref, pmv_ref, gt_ref, x_ref,
                           wpa_ref, wpb_ref, wpc_ref, wout_ref, gpost_ref,
                           y_ref, oas_ref, ocs_ref,
                           full_ref, conv_ref, obg_ref, qh_ref, knh_ref, vnh_ref, oh_ref,
                           *, tm, rc, rn, T, tiles_per_seq):
    g = pl.program_id(0)
    heads = [slice(h * DC, (h + 1) * DC) for h in range(HC)]

    def gate(k):
        return _sigmoid(gt_ref[0, :, k * D_MODEL:(k + 1) * D_MODEL].astype(F32))

    _conv_stage(glu_ref, halo_ref, g % tiles_per_seq == 0, full_ref, tm)
    q8, kn8, vn8 = _pad_rows(q_ref[0], QROWS), _pad_rows(kn_ref[0], QROWS), _pad_rows(vn_ref[0], QROWS)
    for h in range(HA):
        hs = slice(h * DA, (h + 1) * DA)
        qh_ref[h], knh_ref[h], vnh_ref[h] = q8[:, hs], kn8[:, hs], vn8[:, hs]
    scores = [lax.dot_general(pqc_ref[0, :, hs], pmk_ref[0, :, hs], NT_DIMS, preferred_element_type=F32)
              for hs in heads]
    a_g = jnp.concatenate(
        [(oa_ref[0, c].astype(F32) * _silu(za_ref[0, :, c * 128:(c + 1) * 128].astype(F32))).astype(BF16)
         for c in range(NHP)], axis=-1)
    pa = jnp.dot(a_g, wpa_ref[...], preferred_element_type=F32)
    _sample_cross_attn(qc_ref, mk_ref, mv_ref, xm_ref, ocs_ref, T)
    for c in range(tm // rc):
        _conv_taps(c, full_ref, cw_ref, conv_ref, rc)
    _sample_heads(list(range(HA)), qh_ref, knh_ref, vnh_ref, kt_ref, vt_ref,
                  tb_ref, tm_ref, nb_ref, nm_ref, oh_ref)
    pieces = []
    for s, hs in zip(scores, heads):
        p = jnp.exp(s - jnp.max(s, axis=-1, keepdims=True))
        l = jnp.sum(p, axis=-1, keepdims=True)
        oc = jnp.dot(p.astype(BF16), pmv_ref[0, :, hs], preferred_element_type=F32) / l
        pieces.append((oc * _silu(pzc_ref[0, :, hs].astype(F32))).astype(BF16))
    pc = jnp.dot(jnp.concatenate(pieces, axis=-1), wpc_ref[...], preferred_element_type=F32)
    for c in range(tm // rn):
        _conv_norm_gate(c, conv_ref, zb_ref, cb_ref, lg_ref, lb_ref, obg_ref, rn)
    pb = jnp.dot(obg_ref[...], wpb_ref[...], preferred_element_type=F32)
    mix = gate(0) * pa + gate(1) * pb + gate(2) * pc
    z = jnp.dot(mix.astype(BF16), wout_ref[...], preferred_element_type=F32)
    y_ref[0] = x_ref[0] + _rmsnorm_f32(z, gpost_ref[...])
    oas_ref[0] = jnp.concatenate([oh_ref[h][:T] for h in range(HA)], axis=-1)


def _prompt_tail_sample_attn(glu, zb, cw, cb, lg, lb, q, k_new, v_new, cache_kt, cache_vt, tables,
                             qc, mem_k, mem_v, oa, za, pqc, pzc, pmk, pmv, gt, x,
                             wpa, wpb, wpc, wout, gpost, rc, rn):
    B, S, _ = glu.shape
    Bd, T, _ = q.shape
    wb = cache_kt.shape[-1]
    tm = (B * S) // Bd
    tiles_per_seq = S // tm
    assert B * tiles_per_seq == Bd and tm % rn == 0 and tm % HALO == 0
    cache_b, cache_m, new_b, new_m = tables
    own_head = jnp.asarray(np.where(
        np.arange(HC * QROWS)[:, None] // QROWS == np.arange(N_MEM * HC)[None, :] % HC, 0.0, NEG
    ).astype(np.float32))
    tile = lambda g: (g // tiles_per_seq, g % tiles_per_seq, 0)
    halo = lambda g: (g // tiles_per_seq, jnp.maximum((g % tiles_per_seq) * (tm // HALO) - 1, 0), 0)
    per_b3 = lambda g: (g, 0, 0)
    per_b4 = lambda g: (g, 0, 0, 0)
    c2 = lambda g: (0, 0)
    c3 = lambda g: (0, 0, 0)
    head_scratch = pltpu.VMEM((HA, QROWS, DA), F32)
    pair_tile = lambda g: (g // tiles_per_seq, 0, g % tiles_per_seq, 0)
    per_seq = lambda g: (g // tiles_per_seq, 0, 0)
    once = pl.Buffered(1)
    return pl.pallas_call(
        functools.partial(_conv_attn_tail_kernel, tm=tm, rc=rc, rn=rn, T=T, tiles_per_seq=tiles_per_seq),
        out_shape=(jax.ShapeDtypeStruct((B, S, D_MODEL), F32),
                   jax.ShapeDtypeStruct((Bd, T, EA), F32), jax.ShapeDtypeStruct((Bd, T, EC), F32)),
        grid=(Bd,),
        in_specs=[pl.BlockSpec((1, tm, EB), tile), pl.BlockSpec((1, HALO, EB), halo),
                  pl.BlockSpec((1, tm, EB), tile),
                  pl.BlockSpec((CONV_W, 8, EB), c3),
                  pl.BlockSpec((1, EB), c2), pl.BlockSpec((1, EB), c2), pl.BlockSpec((1, EB), c2),
                  pl.BlockSpec((1, T, EA), per_b3),
                  pl.BlockSpec((1, HA, DA, wb), per_b4), pl.BlockSpec((1, HA, DA, wb), per_b4),
                  pl.BlockSpec((1, T, EA), per_b3), pl.BlockSpec((1, T, EA), per_b3),
                  pl.BlockSpec((HA, QROWS, wb), c3), pl.BlockSpec((QROWS, wb), c2),
                  pl.BlockSpec((HA, QROWS, QROWS), c3), pl.BlockSpec((QROWS, QROWS), c2),
                  pl.BlockSpec((1, T, EC), per_b3),
                  pl.BlockSpec((1, N_MEM * HC, DC), per_b3), pl.BlockSpec((1, N_MEM * HC, DC), per_b3),
                  pl.BlockSpec((HC * QROWS, N_MEM * HC), c2),
                  pl.BlockSpec((1, NHP, tm, 128), pair_tile), pl.BlockSpec((1, tm, EA), tile),
                  pl.BlockSpec((1, tm, EC), tile), pl.BlockSpec((1, tm, EC), tile),
                  pl.BlockSpec((1, N_MEM, EC), per_seq), pl.BlockSpec((1, N_MEM, EC), per_seq),
                  pl.BlockSpec((1, tm, 3 * D_MODEL), tile), pl.BlockSpec((1, tm, D_MODEL), tile),
                  pl.BlockSpec((EA, D_MODEL), c2, pipeline_mode=once),
                  pl.BlockSpec((EB, D_MODEL), c2, pipeline_mode=once),
                  pl.BlockSpec((EC, D_MODEL), c2, pipeline_mode=once),
                  pl.BlockSpec((D_MODEL, D_MODEL), c2, pipeline_mode=once),
                  pl.BlockSpec((1, D_MODEL), c2)],
        out_specs=(pl.BlockSpec((1, tm, D_MODEL), tile),
                   pl.BlockSpec((1, T, EA), per_b3), pl.BlockSpec((1, T, EC), per_b3)),
        scratch_shapes=[pltpu.VMEM((8, HALO + tm, EB), F32), pltpu.VMEM((tm, EB), F32),
                        pltpu.VMEM((tm, EB), BF16),
                        head_scratch, head_scratch, head_scratch, head_scratch],
        compiler_params=_params("arbitrary"),
        name="prompt_tail_sample_attn",
    )(glu, glu, zb, cw, cb, lg, lb, q, cache_kt, cache_vt, k_new, v_new,
      cache_b, cache_m, new_b, new_m, qc, mem_k, mem_v, own_head,
      oa, za, pqc, pzc, pmk, pmv, gt, x, wpa, wpb, wpc, wout, gpost)


def _conv_sample_kernel(st_ref, glu_ref, cw_ref, cb_ref, lg_ref, lb_ref, o_ref, ns_ref, *, T, rc):
    n_hist = CONV_W - 1
    Bd = st_ref.shape[1]

    def shift(t, carry):
        ns_ref[t] = st_ref[t + T]
        return carry
    lax.fori_loop(0, n_hist - T, shift, 0)
    for i in range(T):
        ns_ref[n_hist - T + i] = glu_ref[i]
    for i in range(T):
        def chunk(c, carry, i=i):
            rows = pl.ds(pl.multiple_of(c * rc, rc), rc)
            acc = jnp.zeros((rc, EB), F32)
            for j in range(CONV_W):
                src = st_ref[i + j, rows, :] if i + j < n_hist else glu_ref[i + j - n_hist, rows, :]
                acc = acc + src * cw_ref[j:j + 1, :]
            o_ref[i, rows, :] = _layernorm_silu(acc + cb_ref[...], lg_ref[...], lb_ref[...])
            return carry
        lax.fori_loop(0, Bd // rc, chunk, 0)


def _conv_sample(state_t, glu_t, cw, cb, lg, lb, rc):
    T, Bd, _ = glu_t.shape
    return pl.pallas_call(
        functools.partial(_conv_sample_kernel, T=T, rc=rc),
        out_shape=(jax.ShapeDtypeStruct((T, Bd, EB), F32), jax.ShapeDtypeStruct(state_t.shape, F32)),
        compiler_params=pltpu.CompilerParams(vmem_limit_bytes=V7X_VMEM_LIMIT),
        name="conv_sample",
    )(state_t, glu_t, cw, cb, lg, lb)


def kernel(x_prompt, x_sample, mem_prompt, cache_k_win, cache_v_win, state_conv, cache_k_mem, cache_v_mem,
           rel_bias, g_pre, w_in, g_mem, w_mem_kv, conv_w, conv_b, ln_g, ln_b, w_proj_a, w_proj_b,
           w_proj_c, w_out, g_post):
    depth = g_pre.shape[0]
    assert depth == 1, "single-layer step"
    B, S, _ = x_prompt.shape
    Bd, T, _ = x_sample.shape
    wb = cache_k_win.shape[2]
    past = wb
    assert wb == MAX_DIST and S % (16 * QB) == 0 and T <= QROWS

    l = 0
    w_in_b = w_in[l].astype(BF16)
    wpa, wpb, wpc = w_proj_a[l].astype(BF16), w_proj_b[l].astype(BF16), w_proj_c[l].astype(BF16)
    wout = w_out[l].astype(BF16)
    gpre, gpost = g_pre[l][None], g_post[l][None]
    cb, lg, lb = conv_b[l][None], ln_g[l][None], ln_b[l][None]
    cw = conv_w[l]

    (qkv, k_keep, v_keep, za, glu, zb, qc, zc, gt, glu_tail,
     q_s, k_s, v_s, za_s, glu_s, zb_s, qc_s, zc_s, gt_s) = _inproj(
        x_prompt, x_sample.reshape(Bd * T, D_MODEL), gpre, w_in_b, tm=TM_INPROJ)
    mk_f, mv_f, mk_b, mv_b = _memkv(mem_prompt.reshape(B * N_MEM, D_MODEL), g_mem[l][None],
                                    w_mem_kv[l].astype(BF16), tm=TM_MEMKV)
    oa = _band_attn(qkv, _band_tables(rel_bias))

    cache_kt = jnp.transpose(cache_k_win[l], (0, 2, 3, 1))
    cache_vt = jnp.transpose(cache_v_win[l], (0, 2, 3, 1))
    cw_tiles = jnp.broadcast_to(cw[:, None, :], (CONV_W, 8, EB))
    y_p, oa_s, oc_s = _prompt_tail_sample_attn(
        glu, zb, cw_tiles, cb, lg, lb,
        q_s.reshape(Bd, T, EA), k_s.reshape(Bd, T, EA), v_s.reshape(Bd, T, EA),
        cache_kt, cache_vt, _sample_tables(rel_bias, wb, past, T),
        qc_s.reshape(Bd, T, EC), cache_k_mem[l].reshape(Bd, N_MEM * HC, DC),
        cache_v_mem[l].reshape(Bd, N_MEM * HC, DC),
        oa, za, qc, zc, mk_b.reshape(B, N_MEM, EC), mv_b.reshape(B, N_MEM, EC), gt, x_prompt,
        wpa, wpb, wpc, wout, gpost, rc=CONV_TAP_ROWS, rn=CONV_NORM_ROWS)

    state_t = jnp.transpose(state_conv[l], (1, 0, 2))
    glu_t = jnp.transpose(glu_s.reshape(Bd, T, EB), (1, 0, 2))
    ob_t, new_state_t = _conv_sample(state_t, glu_t, cw, cb, lg, lb, rc=CONV_TAP_ROWS)
    ob_s = jnp.transpose(ob_t, (1, 0, 2)).reshape(Bd * T, EB)
    y_s = _tail_sample(oa_s.reshape(Bd * T, EA), za_s, ob_s, zb_s,
                       oc_s.reshape(Bd * T, EC), zc_s, gt_s, x_sample.reshape(Bd * T, D_MODEL),
                       wpa, wpb, wpc, wout, gpost, tm=Bd * T)

    conv_state_s = jnp.transpose(new_state_t, (1, 0, 2))
    return (y_p, y_s.reshape(Bd, T, D_MODEL),
            k_keep.reshape(1, B, wb, HA, DA), v_keep.reshape(1, B, wb, HA, DA),
            glu_tail[:, HALO - (CONV_W - 1):][None],
            mk_f.reshape(1, B, N_MEM, HC, DC), mv_f.reshape(1, B, N_MEM, HC, DC),
            k_s.reshape(1, Bd, T, HA, DA), v_s.reshape(1, Bd, T, HA, DA),
            conv_state_s[None])
```

```python
import functools

import jax
import jax.numpy as jnp
import numpy as np
from jax import lax
from jax.experimental import pallas as pl
from jax.experimental.pallas import tpu as pltpu

F32 = jnp.float32
BF16 = jnp.bfloat16

D_MODEL = 1024
HA, DA = 12, 64
EA = HA * DA
NHP = HA // 2
PATTERNS = ((128, 1), (512, 4), (2048, 16))
NW = 128
QB = 128
EB = 768
CONV_W = 31
HC, DC = 4, 128
EC = HC * DC
N_MEM = 256
N_BUCKETS = 32
MAX_DIST = 2048
EPS = 1e-6
NEG = -1e30
LOG2E = 1.4426950408889634

C_Q, C_K, C_V, C_ZA = 0, EA, 2 * EA, 3 * EA
C_U = 4 * EA
C_G = C_U + EB
C_ZB = C_U + 2 * EB
C_QC = C_ZB + EB
C_ZC = C_QC + EC
C_GT = C_ZC + EC
IN_COLS = C_GT + 3 * D_MODEL

V7X_VMEM_LIMIT = 56 * 1024 * 1024
HALO = 32
QROWS = 8
NT_DIMS = (((1,), (1,)), ((), ()))

TM_INPROJ = 256
TM_MEMKV = 512
CONV_TAP_ROWS = 32
CONV_NORM_ROWS = 64


def _t5_bucket_ids(n):
    exact = N_BUCKETS // 2
    nf = np.maximum(n, 1).astype(np.float32)
    scale = np.float32(N_BUCKETS - exact) / np.log(np.float32(MAX_DIST) / np.float32(exact))
    large = exact + (np.log(nf / np.float32(exact)) * scale).astype(np.int32)
    large = np.minimum(large, N_BUCKETS - 1)
    return np.where(n < exact, n, large).astype(np.int32)


def _bias_by_bucket(rel_bias, ids):
    onehot = (ids[None] == np.arange(N_BUCKETS).reshape((-1,) + (1,) * ids.ndim)).astype(np.float32)
    return jnp.tensordot(rel_bias.astype(F32).T, jnp.asarray(onehot), axes=1,
                         precision=lax.Precision.HIGHEST)


def _sigmoid(x):
    return 1.0 / (1.0 + jnp.exp(-x))


def _silu(x):
    return x * _sigmoid(x)


def _rmsnorm_f32(x, g):
    return x * lax.rsqrt(jnp.mean(x * x, axis=-1, keepdims=True) + EPS) * g


def _params(*sem):
    return pltpu.CompilerParams(dimension_semantics=sem, vmem_limit_bytes=V7X_VMEM_LIMIT)


def _inproj_columns(xn, w_ref):
    def mm(lo, n):
        return jnp.dot(xn, w_ref[:, lo:lo + n], preferred_element_type=F32)
    return mm


def _inproj_kernel(xp_ref, xs_ref, g_ref, w_ref,
                   qkv_ref, kf_ref, vf_ref, za_ref, glu_ref, zb_ref, qc_ref, zc_ref, gt_ref, tail_ref,
                   sq_ref, sk_ref, sv_ref, sza_ref, sglu_ref, szb_ref, sqc_ref, szc_ref, sgt_ref,
                   *, prompt_steps):
    g = pl.program_id(0)
    sample_refs = (sq_ref, sk_ref, sv_ref, sza_ref, sglu_ref, szb_ref, sqc_ref, szc_ref, sgt_ref)

    @pl.when(g == 0)
    def _():
        for r in sample_refs:
            r[...] = jnp.zeros(r.shape, r.dtype)

    @pl.when(g < prompt_steps)
    def _():
        mm = _inproj_columns(_rmsnorm_f32(xp_ref[0], g_ref[...]).astype(BF16), w_ref)

        def put_pairs(res, base):
            for c in range(NHP):
                qkv_ref[0, base + c] = res[:, c * 128:(c + 1) * 128].astype(BF16)

        put_pairs(mm(C_Q, EA) * (DA ** -0.5 * LOG2E), 0)
        k = mm(C_K, EA)
        put_pairs(k, NHP)
        kf_ref[0] = k
        v = mm(C_V, EA)
        put_pairs(v, 2 * NHP)
        vf_ref[0] = v
        za_ref[0] = _silu(mm(C_ZA, EA)).astype(BF16)
        glu = mm(C_U, EB) * _sigmoid(mm(C_G, EB))
        glu_ref[0] = glu.astype(BF16)
        tail_ref[0] = glu[glu.shape[0] - HALO:, :]
        zb_ref[0] = _silu(mm(C_ZB, EB)).astype(BF16)
        qc_ref[0] = (mm(C_QC, EC) * (DC ** -0.5)).astype(BF16)
        zc_ref[0] = _silu(mm(C_ZC, EC)).astype(BF16)
        for c in range(3):
            gt_ref[0, :, c * D_MODEL:(c + 1) * D_MODEL] = _sigmoid(mm(C_GT + c * D_MODEL, D_MODEL)).astype(BF16)

    @pl.when(g >= prompt_steps)
    def _():
        mm = _inproj_columns(_rmsnorm_f32(xs_ref[...], g_ref[...]).astype(BF16), w_ref)
        sq_ref[...] = mm(C_Q, EA) * (DA ** -0.5)
        sk_ref[...] = mm(C_K, EA)
        sv_ref[...] = mm(C_V, EA)
        sza_ref[...] = _silu(mm(C_ZA, EA)).astype(BF16)
        sglu_ref[...] = mm(C_U, EB) * _sigmoid(mm(C_G, EB))
        szb_ref[...] = _silu(mm(C_ZB, EB)).astype(BF16)
        sqc_ref[...] = mm(C_QC, EC) * (DC ** -0.5)
        szc_ref[...] = _silu(mm(C_ZC, EC)).astype(BF16)
        for c in range(3):
            sgt_ref[:, c * D_MODEL:(c + 1) * D_MODEL] = _sigmoid(mm(C_GT + c * D_MODEL, D_MODEL)).astype(BF16)


def _inproj(x, xs, g, w, tm):
    B, S, _ = x.shape
    Ms = xs.shape[0]
    nt = S // tm
    P = B * nt
    wb = min(MAX_DIST, S)
    first_kept = (S - wb) // tm
    pb = lambda s: jnp.minimum(s, P - 1) // nt
    pi = lambda s: jnp.minimum(s, P - 1) % nt
    row = lambda s: (pb(s), pi(s), 0)
    kept = lambda s: (pb(s), jnp.maximum(pi(s) - first_kept, 0), 0)
    srow = lambda s: (jnp.maximum(s - P, 0), 0)
    const = lambda s: (0, 0)
    s_widths = (EA, EA, EA, EA, EB, EB, EC, EC, 3 * D_MODEL)
    s_dtypes = (F32, F32, F32, BF16, F32, BF16, F32, BF16, BF16)
    out_shape = (
        jax.ShapeDtypeStruct((B, 3 * NHP, S, 128), BF16),
        jax.ShapeDtypeStruct((B, wb, EA), F32),
        jax.ShapeDtypeStruct((B, wb, EA), F32),
        jax.ShapeDtypeStruct((B, S, EA), BF16),
        jax.ShapeDtypeStruct((B, S, EB), BF16),
        jax.ShapeDtypeStruct((B, S, EB), BF16),
        jax.ShapeDtypeStruct((B, S, EC), BF16),
        jax.ShapeDtypeStruct((B, S, EC), BF16),
        jax.ShapeDtypeStruct((B, S, 3 * D_MODEL), BF16),
        jax.ShapeDtypeStruct((B, HALO, EB), F32),
    ) + tuple(jax.ShapeDtypeStruct((Ms, n), dt) for n, dt in zip(s_widths, s_dtypes))
    out_specs = (
        pl.BlockSpec((1, 3 * NHP, tm, 128), lambda s: (pb(s), 0, pi(s), 0)),
        pl.BlockSpec((1, tm, EA), kept),
        pl.BlockSpec((1, tm, EA), kept),
        pl.BlockSpec((1, tm, EA), row),
        pl.BlockSpec((1, tm, EB), row),
        pl.BlockSpec((1, tm, EB), row),
        pl.BlockSpec((1, tm, EC), row),
        pl.BlockSpec((1, tm, EC), row),
        pl.BlockSpec((1, tm, 3 * D_MODEL), row),
        pl.BlockSpec((1, HALO, EB), lambda s: (pb(s), 0, 0)),
    ) + tuple(pl.BlockSpec((tm, n), srow) for n in s_widths)
    return pl.pallas_call(
        functools.partial(_inproj_kernel, prompt_steps=P),
        out_shape=out_shape,
        grid=(P + Ms // tm,),
        in_specs=[
            pl.BlockSpec((1, tm, D_MODEL), row),
            pl.BlockSpec((tm, D_MODEL), srow),
            pl.BlockSpec((1, D_MODEL), const),
            pl.BlockSpec((D_MODEL, IN_COLS), const, pipeline_mode=pl.Buffered(1)),
        ],
        out_specs=out_specs,
        compiler_params=_params("arbitrary"),
        name="inproj",
    )(x, xs, g, w)


def _memkv_kernel(m_ref, g_ref, w_ref, kf_ref, vf_ref, kb_ref, vb_ref):
    xn = _rmsnorm_f32(m_ref[...], g_ref[...]).astype(BF16)
    k = jnp.dot(xn, w_ref[:, :EC], preferred_element_type=F32)
    v = jnp.dot(xn, w_ref[:, EC:], preferred_element_type=F32)
    kf_ref[...] = k
    vf_ref[...] = v
    kb_ref[...] = k.astype(BF16)
    vb_ref[...] = v.astype(BF16)


def _memkv(mem, g, w, tm):
    M = mem.shape[0]
    row = lambda i: (i, 0)
    return pl.pallas_call(
        _memkv_kernel,
        out_shape=(jax.ShapeDtypeStruct((M, EC), F32), jax.ShapeDtypeStruct((M, EC), F32),
                   jax.ShapeDtypeStruct((M, EC), BF16), jax.ShapeDtypeStruct((M, EC), BF16)),
        grid=(M // tm,),
        in_specs=[pl.BlockSpec((tm, D_MODEL), row),
                  pl.BlockSpec((1, D_MODEL), lambda i: (0, 0)),
                  pl.BlockSpec((D_MODEL, 2 * EC), lambda i: (0, 0))],
        out_specs=tuple(pl.BlockSpec((tm, EC), row) for _ in range(4)),
        compiler_params=_params("arbitrary"),
        name="memkv",
    )(mem, g, w)


def _band_tables(rel_bias):
    iq = np.arange(QB)[:, None]
    ik = np.arange(2 * QB)[None, :]
    dist = iq - ik + QB
    band = (dist >= 0) & (dist <= NW)
    out = []
    for (_, d) in PATTERNS:
        bias = LOG2E * _bias_by_bucket(rel_bias, _t5_bucket_ids(np.clip(dist, 0, NW) * d))
        later = jnp.where(jnp.asarray(band)[None], bias, NEG)
        first = jnp.where(jnp.asarray(band & (ik >= QB))[None], bias, NEG)
        t = jnp.stack([first, later], axis=1)
        out.append(t.reshape(NHP, 2, 2, QB, 2 * QB).transpose(0, 2, 1, 3, 4)
                   .reshape(NHP, 2, 2 * QB, 2 * QB))
    return jnp.stack(out)


def _band_attn_kernel(q_ref, k_ref, v_ref, t_ref, oa_ref,
                      nat32, p4_32, p4_ref, p16_ref, a4_ref, l4_ref, m4_ref, a16_ref, l16_ref, m16_ref,
                      *, S):
    n_units = S // QB
    lane = lax.broadcasted_iota(jnp.int32, (QB, 128), 1)
    head0 = lane < DA
    keep0 = head0.astype(F32).astype(BF16)
    keep1 = (1.0 - head0.astype(F32)).astype(BF16)
    ones = jnp.ones((2 * QB, 128), BF16)
    n4 = S // 4

    for t, src in enumerate((q_ref, k_ref, v_ref)):
        def widen(c, carry, src=src):
            r0 = pl.multiple_of(c * 256, 256)
            nat32[pl.ds(r0, 256), :] = src[0, 0, pl.ds(r0, 256), :].astype(F32)
            return carry
        lax.fori_loop(0, S // 256, widen, 0)
        for r in range(4):
            def by4(c, carry, r=r, t=t):
                u0 = pl.multiple_of(c * 256, 256)
                x = nat32[pl.ds(4 * u0 + r, 256, stride=4), :]
                p4_32[pl.ds(r * n4 + u0, 256), :] = x
                p4_ref[t, pl.ds(r * n4 + u0, 256), :] = x.astype(BF16)
                return carry
            lax.fori_loop(0, n4 // 256, by4, 0)
        for r4 in range(4):
            for s in range(4):
                x = p4_32[pl.ds(r4 * n4 + s, n4 // 4, stride=4), :]
                p16_ref[t, pl.ds((4 * s + r4) * (n4 // 4), n4 // 4), :] = x.astype(BF16)

    def unit(u, pat, d, qsrc, ksrc, vsrc):
        nqb = n_units // d
        row = pl.multiple_of(u * QB, QB)
        prow = pl.multiple_of(jnp.maximum(u - 1, 0) * QB, QB)
        q2 = qsrc[pl.ds(row, QB), :]
        kk = jnp.concatenate([ksrc[pl.ds(prow, QB), :], ksrc[pl.ds(row, QB), :]], axis=0)
        vv = jnp.concatenate([vsrc[pl.ds(prow, QB), :], vsrc[pl.ds(row, QB), :]], axis=0)
        qs = jnp.concatenate([q2 * keep0, q2 * keep1], axis=0)
        s = lax.dot_general(qs, kk, NT_DIMS, preferred_element_type=F32)
        s = s + t_ref[pat, 0, jnp.minimum(u & (nqb - 1), 1)]
        m = jnp.max(s, axis=-1, keepdims=True)
        p = jnp.exp2(s - m).astype(BF16)
        oe = jnp.dot(p, jnp.concatenate([vv, ones], axis=1), preferred_element_type=F32)
        mb = jnp.broadcast_to(m, (2 * QB, 128))
        return (jnp.where(head0, oe[:QB, :128], oe[QB:, :128]),
                jnp.where(head0, oe[:QB, 128:], oe[QB:, 128:]),
                jnp.where(head0, mb[:QB], mb[QB:]))

    def strided_pattern(pat, d, src, acc_ref, den_ref, max_ref):
        nqb = n_units // d

        def body(u, carry):
            acc, den, mx = unit(u, pat, d, src.at[0], src.at[1], src.at[2])
            start = u // nqb + (u & (nqb - 1)) * (d * QB)
            acc_ref[pl.ds(start, QB, stride=d), :] = acc
            den_ref[pl.ds(start, QB, stride=d), :] = den
            max_ref[pl.ds(start, QB, stride=d), :] = mx
            return carry
        lax.fori_loop(0, n_units, body, 0, unroll=n_units)

    strided_pattern(2, 16, p16_ref, a16_ref, l16_ref, m16_ref)
    strided_pattern(1, 4, p4_ref, a4_ref, l4_ref, m4_ref)

    def dense(u, carry):
        a1, l1, m1 = unit(u, 0, 1, q_ref.at[0, 0], k_ref.at[0, 0], v_ref.at[0, 0])
        rows = pl.ds(pl.multiple_of(u * QB, QB), QB)
        m4, m16 = m4_ref[rows, :], m16_ref[rows, :]
        mx = jnp.maximum(jnp.maximum(m1, m4), m16)
        e1, e4, e16 = jnp.exp2(m1 - mx), jnp.exp2(m4 - mx), jnp.exp2(m16 - mx)
        num = a1 * e1 + a4_ref[rows, :] * e4 + a16_ref[rows, :] * e16
        den = l1 * e1 + l4_ref[rows, :] * e4 + l16_ref[rows, :] * e16
        oa_ref[0, 0, rows, :] = (num / den).astype(BF16)
        return carry
    lax.fori_loop(0, n_units, dense, 0, unroll=n_units)


def _band_attn(qkv, tables):
    B, _, S, _ = qkv.shape
    blk = (1, 1, S, 128)
    return pl.pallas_call(
        functools.partial(_band_attn_kernel, S=S),
        out_shape=jax.ShapeDtypeStruct((B, NHP, S, 128), BF16),
        grid=(B, NHP),
        in_specs=[pl.BlockSpec(blk, lambda b, h: (b, h, 0, 0)),
                  pl.BlockSpec(blk, lambda b, h: (b, NHP + h, 0, 0)),
                  pl.BlockSpec(blk, lambda b, h: (b, 2 * NHP + h, 0, 0)),
                  pl.BlockSpec((3, 1, 2, 2 * QB, 2 * QB), lambda b, h: (0, h, 0, 0, 0))],
        out_specs=pl.BlockSpec(blk, lambda b, h: (b, h, 0, 0)),
        scratch_shapes=[pltpu.VMEM((S, 128), F32), pltpu.VMEM((S, 128), F32),
                        pltpu.VMEM((3, S, 128), BF16), pltpu.VMEM((3, S, 128), BF16)]
                       + [pltpu.VMEM((S, 128), F32)] * 6,
        compiler_params=_params("arbitrary", "arbitrary"),
        name="band_attn",
    )(qkv, qkv, qkv, tables)


def _layernorm_silu(c, g, b):
    mu = jnp.mean(c, axis=-1, keepdims=True)
    cc = c - mu
    var = jnp.mean(cc * cc, axis=-1, keepdims=True)
    return _silu(cc * lax.rsqrt(var + EPS) * g + b)


FIRST_TAP = HALO - (CONV_W - 1)


def _conv_stage(glu_ref, halo_ref, seq_start, full_ref, tm):
    halo = halo_ref[0].astype(F32)
    full_ref[0, 0:HALO, :] = jnp.where(seq_start, jnp.zeros_like(halo), halo)
    full_ref[0, HALO:HALO + tm, :] = glu_ref[0].astype(F32)
    n_shift = HALO + tm - 8
    for s in range(1, 8):
        full_ref[s, 0:n_shift, :] = full_ref[0, s:s + n_shift, :]


def _conv_taps(c, full_ref, cw_ref, conv_ref, rc):
    base = c * rc
    accs = [jnp.zeros((8, EB), F32) for _ in range(rc // 8)]
    for j in range(CONV_W):
        off = FIRST_TAP + j
        w8 = cw_ref[j]
        for g in range(rc // 8):
            accs[g] = accs[g] + full_ref[off % 8, pl.ds(base + (off // 8) * 8 + 8 * g, 8), :] * w8
    conv_ref[pl.ds(base, rc), :] = jnp.concatenate(accs, axis=0)


def _conv_norm_gate(c, conv_ref, zb_ref, cb_ref, lg_ref, lb_ref, o_ref, rn):
    rows = pl.ds(c * rn, rn)
    ob = _layernorm_silu(conv_ref[rows, :] + cb_ref[...], lg_ref[...], lb_ref[...])
    o_ref[rows, :] = (ob * zb_ref[0, rows, :].astype(F32)).astype(BF16)


def _gated_tail(a_g, b_g, c_g, gt_ref, x, wpa_ref, wpb_ref, wpc_ref, wout_ref, gpost_ref):
    def gate(k):
        return gt_ref[:, k * D_MODEL:(k + 1) * D_MODEL].astype(F32)

    mix = gate(0) * jnp.dot(a_g, wpa_ref[...], preferred_element_type=F32)
    mix = mix + gate(1) * jnp.dot(b_g, wpb_ref[...], preferred_element_type=F32)
    mix = mix + gate(2) * jnp.dot(c_g, wpc_ref[...], preferred_element_type=F32)
    z = jnp.dot(mix.astype(BF16), wout_ref[...], preferred_element_type=F32)
    return x + _rmsnorm_f32(z, gpost_ref[...])


def _tail_sample_kernel(oa_ref, za_ref, ob_ref, zb_ref, oc_ref, zc_ref, gt_ref, x_ref,
                        wpa_ref, wpb_ref, wpc_ref, wout_ref, gpost_ref, y_ref):
    a_g = (oa_ref[...] * za_ref[...].astype(F32)).astype(BF16)
    b_g = (ob_ref[...] * zb_ref[...].astype(F32)).astype(BF16)
    c_g = (oc_ref[...] * zc_ref[...].astype(F32)).astype(BF16)
    y_ref[...] = _gated_tail(a_g, b_g, c_g, gt_ref, x_ref[...],
                             wpa_ref, wpb_ref, wpc_ref, wout_ref, gpost_ref)


def _tail_sample(oa, za, ob, zb, oc, zc, gt, x, wpa, wpb, wpc, wout, gpost, tm):
    M = x.shape[0]
    row = lambda i: (i, 0)
    const = lambda i: (0, 0)
    widths = (EA, EA, EB, EB, EC, EC, 3 * D_MODEL, D_MODEL)
    return pl.pallas_call(
        _tail_sample_kernel,
        out_shape=jax.ShapeDtypeStruct((M, D_MODEL), F32),
        grid=(M // tm,),
        in_specs=[pl.BlockSpec((tm, n), row) for n in widths] + [
            pl.BlockSpec((EA, D_MODEL), const), pl.BlockSpec((EB, D_MODEL), const),
            pl.BlockSpec((EC, D_MODEL), const), pl.BlockSpec((D_MODEL, D_MODEL), const),
            pl.BlockSpec((1, D_MODEL), const)],
        out_specs=pl.BlockSpec((tm, D_MODEL), row),
        compiler_params=_params("arbitrary"),
        name="tail_sample",
    )(oa, za, ob, zb, oc, zc, gt, x, wpa, wpb, wpc, wout, gpost)


def _sample_tables(rel_bias, wb, past, T):
    def mult_of(dl, real):
        m = np.zeros(dl.shape, np.float32)
        for (w, d) in PATTERNS:
            m += (real & (dl % d == 0) & (dl <= w)).astype(np.float32)
        return m

    def tables(dl, real):
        row = np.arange(QROWS)[:, None]
        real = real & (dl >= 0) & (past + row - dl >= 0) & (wb + row - dl >= 0)
        m = mult_of(dl, real)
        dlc = np.clip(dl, 0, MAX_DIST)
        m[T:] = m[0]
        bias = jnp.where(jnp.asarray(m > 0)[None], _bias_by_bucket(rel_bias, _t5_bucket_ids(dlc)), NEG)
        return bias, jnp.asarray(m)

    i = np.arange(QROWS)[:, None]
    pos = np.arange(wb)[None, :]
    cache_b, cache_m = tables(wb + i - pos, np.ones((QROWS, wb), bool))
    j = np.arange(QROWS)[None, :]
    new_b, new_m = tables(i - j, np.broadcast_to(j < T, (QROWS, QROWS)))
    return cache_b, cache_m, new_b, new_m


def _pad_rows(x, rows):
    return jnp.concatenate([x, jnp.zeros((rows - x.shape[0], x.shape[1]), x.dtype)], axis=0)


def _sample_heads(heads, qh_ref, knh_ref, vnh_ref, kt_ref, vt_ref, cb_ref, cm_ref, nb_ref, nm_ref, oh_ref):
    scores = []
    for h in heads:
        qh = qh_ref[h].astype(BF16)
        s = jnp.dot(qh, kt_ref[0, h].astype(BF16), preferred_element_type=F32) + cb_ref[h]
        sn = lax.dot_general(qh, knh_ref[h].astype(BF16), NT_DIMS, preferred_element_type=F32) + nb_ref[h]
        scores.append((s, sn))
    probs = []
    for s, sn in scores:
        m = jnp.maximum(jnp.max(s, axis=-1, keepdims=True), jnp.max(sn, axis=-1, keepdims=True))
        p = jnp.exp(s - m) * cm_ref[...]
        pn = jnp.exp(sn - m) * nm_ref[...]
        l = jnp.sum(p, axis=-1, keepdims=True) + jnp.sum(pn, axis=-1, keepdims=True)
        probs.append((p.astype(BF16), pn.astype(BF16), l))
    for h, (p, pn, l) in zip(heads, probs):
        o = lax.dot_general(p, vt_ref[0, h].astype(BF16), NT_DIMS, preferred_element_type=F32)
        o = o + jnp.dot(pn, vnh_ref[h].astype(BF16), preferred_element_type=F32)
        oh_ref[h] = o / l


def _sample_cross_attn(qc_ref, mk_ref, mv_ref, xm_ref, oc_ref, T):
    qc8 = _pad_rows(qc_ref[0], QROWS)
    qc_heads = jnp.concatenate([qc8[:, h * DC:(h + 1) * DC] for h in range(HC)], axis=0).astype(BF16)
    sc = lax.dot_general(qc_heads, mk_ref[0].astype(BF16), NT_DIMS, preferred_element_type=F32) + xm_ref[...]
    pc = jnp.exp(sc - jnp.max(sc, axis=-1, keepdims=True))
    lc = jnp.sum(pc, axis=-1, keepdims=True)
    oc = jnp.dot(pc.astype(BF16), mv_ref[0].astype(BF16), preferred_element_type=F32) / lc
    oc_ref[0] = jnp.concatenate([oc[h * QROWS:h * QROWS + T] for h in range(HC)], axis=-1)


def _conv_attn_tail_kernel(glu_ref, halo_ref, zb_ref, cw_ref, cb_ref, lg_ref, lb_ref,
                           q_ref, kt_ref, vt_ref, kn_ref, vn_ref, tb_ref, tm_ref, nb_ref, nm_ref,
                           qc_ref, mk_ref, mv_ref, xm_ref,
                           oa_ref, za_ref, pqc_ref, pzc_ref, pmk_ref, pmv_ref, gt_ref, x_ref,
                           wpa_ref, wpb_ref, wpc_ref, wout_ref, gpost_ref,
                           y_ref, oas_ref, ocs_ref,
                           full_ref, conv_ref, obg_ref, qh_ref, knh_ref, vnh_ref, oh_ref,
                           *, tm, rc, rn, T, tiles_per_seq):
    g = pl.program_id(0)
    heads = [slice(h * DC, (h + 1) * DC) for h in range(HC)]

    def gate(k):
        return gt_ref[0, :, k * D_MODEL:(k + 1) * D_MODEL].astype(F32)

    _conv_stage(glu_ref, halo_ref, g % tiles_per_seq == 0, full_ref, tm)
    q8, kn8, vn8 = _pad_rows(q_ref[0], QROWS), _pad_rows(kn_ref[0], QROWS), _pad_rows(vn_ref[0], QROWS)
    for h in range(HA):
        hs = slice(h * DA, (h + 1) * DA)
        qh_ref[h], knh_ref[h], vnh_ref[h] = q8[:, hs], kn8[:, hs], vn8[:, hs]
    scores = [lax.dot_general(pqc_ref[0, :, hs], pmk_ref[0, :, hs], NT_DIMS, preferred_element_type=F32)
              for hs in heads]
    a_g = jnp.concatenate(
        [(oa_ref[0, c].astype(F32) * za_ref[0, :, c * 128:(c + 1) * 128].astype(F32)).astype(BF16)
         for c in range(NHP)], axis=-1)
    pa = jnp.dot(a_g, wpa_ref[...], preferred_element_type=F32)
    _sample_cross_attn(qc_ref, mk_ref, mv_ref, xm_ref, ocs_ref, T)
    for c in range(tm // rc):
        _conv_taps(c, full_ref, cw_ref, conv_ref, rc)
    _sample_heads(list(range(HA)), qh_ref, knh_ref, vnh_ref, kt_ref, vt_ref,
                  tb_ref, tm_ref, nb_ref, nm_ref, oh_ref)
    pieces = []
    for s, hs in zip(scores, heads):
        p = jnp.exp(s - jnp.max(s, axis=-1, keepdims=True))
        l = jnp.sum(p, axis=-1, keepdims=True)
        oc = jnp.dot(p.astype(BF16), pmv_ref[0, :, hs], preferred_element_type=F32) / l
        pieces.append((oc * pzc_ref[0, :, hs].astype(F32)).astype(BF16))
    pc = jnp.dot(jnp.concatenate(pieces, axis=-1), wpc_ref[...], preferred_element_type=F32)
    for c in range(tm // rn):
        _conv_norm_gate(c, conv_ref, zb_ref, cb_ref, lg_ref, lb_ref, obg_ref, rn)
    pb = jnp.dot(obg_ref[...], wpb_ref[...], preferred_element_type=F32)
    mix = gate(0) * pa + gate(1) * pb + gate(2) * pc
    z = jnp.dot(mix.astype(BF16), wout_ref[...], preferred_element_type=F32)
    y_ref[0] = x_ref[0] + _rmsnorm_f32(z, gpost_ref[...])
    oas_ref[0] = jnp.concatenate([oh_ref[h][:T] for h in range(HA)], axis=-1)


def _prompt_tail_sample_attn(glu, zb, cw, cb, lg, lb, q, k_new, v_new, cache_kt, cache_vt, tables,
                             qc, mem_k, mem_v, oa, za, pqc, pzc, pmk, pmv, gt, x,
                             wpa, wpb, wpc, wout, gpost, rc, rn):
    B, S, _ = glu.shape
    Bd, T, _ = q.shape
    wb = cache_kt.shape[-1]
    tm = (B * S) // Bd
    tiles_per_seq = S // tm
    assert B * tiles_per_seq == Bd and tm % rn == 0 and tm % HALO == 0
    cache_b, cache_m, new_b, new_m = tables
    own_head = jnp.asarray(np.where(
        np.arange(HC * QROWS)[:, None] // QROWS == np.arange(N_MEM * HC)[None, :] % HC, 0.0, NEG
    ).astype(np.float32))
    tile = lambda g: (g // tiles_per_seq, g % tiles_per_seq, 0)
    halo = lambda g: (g // tiles_per_seq, jnp.maximum((g % tiles_per_seq) * (tm // HALO) - 1, 0), 0)
    per_b3 = lambda g: (g, 0, 0)
    per_b4 = lambda g: (g, 0, 0, 0)
    c2 = lambda g: (0, 0)
    c3 = lambda g: (0, 0, 0)
    head_scratch = pltpu.VMEM((HA, QROWS, DA), F32)
    pair_tile = lambda g: (g // tiles_per_seq, 0, g % tiles_per_seq, 0)
    per_seq = lambda g: (g // tiles_per_seq, 0, 0)
    once = pl.Buffered(1)
    return pl.pallas_call(
        functools.partial(_conv_attn_tail_kernel, tm=tm, rc=rc, rn=rn, T=T, tiles_per_seq=tiles_per_seq),
        out_shape=(jax.ShapeDtypeStruct((B, S, D_MODEL), F32),
                   jax.ShapeDtypeStruct((Bd, T, EA), F32), jax.ShapeDtypeStruct((Bd, T, EC), F32)),
        grid=(Bd,),
        in_specs=[pl.BlockSpec((1, tm, EB), tile), pl.BlockSpec((1, HALO, EB), halo),
                  pl.BlockSpec((1, tm, EB), tile),
                  pl.BlockSpec((CONV_W, 8, EB), c3),
                  pl.BlockSpec((1, EB), c2), pl.BlockSpec((1, EB), c2), pl.BlockSpec((1, EB), c2),
                  pl.BlockSpec((1, T, EA), per_b3),
                  pl.BlockSpec((1, HA, DA, wb), per_b4), pl.BlockSpec((1, HA, DA, wb), per_b4),
                  pl.BlockSpec((1, T, EA), per_b3), pl.BlockSpec((1, T, EA), per_b3),
                  pl.BlockSpec((HA, QROWS, wb), c3), pl.BlockSpec((QROWS, wb), c2),
                  pl.BlockSpec((HA, QROWS, QROWS), c3), pl.BlockSpec((QROWS, QROWS), c2),
                  pl.BlockSpec((1, T, EC), per_b3),
                  pl.BlockSpec((1, N_MEM * HC, DC), per_b3), pl.BlockSpec((1, N_MEM * HC, DC), per_b3),
                  pl.BlockSpec((HC * QROWS, N_MEM * HC), c2),
                  pl.BlockSpec((1, NHP, tm, 128), pair_tile), pl.BlockSpec((1, tm, EA), tile),
                  pl.BlockSpec((1, tm, EC), tile), pl.BlockSpec((1, tm, EC), tile),
                  pl.BlockSpec((1, N_MEM, EC), per_seq), pl.BlockSpec((1, N_MEM, EC), per_seq),
                  pl.BlockSpec((1, tm, 3 * D_MODEL), tile), pl.BlockSpec((1, tm, D_MODEL), tile),
                  pl.BlockSpec((EA, D_MODEL), c2, pipeline_mode=once),
                  pl.BlockSpec((EB, D_MODEL), c2, pipeline_mode=once),
                  pl.BlockSpec((EC, D_MODEL), c2, pipeline_mode=once),
                  pl.BlockSpec((D_MODEL, D_MODEL), c2, pipeline_mode=once),
                  pl.BlockSpec((1, D_MODEL), c2)],
        out_specs=(pl.BlockSpec((1, tm, D_MODEL), tile),
                   pl.BlockSpec((1, T, EA), per_b3), pl.BlockSpec((1, T, EC), per_b3)),
        scratch_shapes=[pltpu.VMEM((8, HALO + tm, EB), F32), pltpu.VMEM((tm, EB), F32),
                        pltpu.VMEM((tm, EB), BF16),
                        head_scratch, head_scratch, head_scratch, head_scratch],
        compiler_params=_params("arbitrary"),
        name="prompt_tail_sample_attn",
    )(glu, glu, zb, cw, cb, lg, lb, q, cache_kt, cache_vt, k_new, v_new,
      cache_b, cache_m, new_b, new_m, qc, mem_k, mem_v, own_head,
      oa, za, pqc, pzc, pmk, pmv, gt, x, wpa, wpb, wpc, wout, gpost)


def _conv_sample_kernel(st_ref, glu_ref, cw_ref, cb_ref, lg_ref, lb_ref, o_ref, ns_ref, *, T, rc):
    n_hist = CONV_W - 1
    Bd = st_ref.shape[1]

    def shift(t, carry):
        ns_ref[t] = st_ref[t + T]
        return carry
    lax.fori_loop(0, n_hist - T, shift, 0)
    for i in range(T):
        ns_ref[n_hist - T + i] = glu_ref[i]
    for i in range(T):
        def chunk(c, carry, i=i):
            rows = pl.ds(pl.multiple_of(c * rc, rc), rc)
            acc = jnp.zeros((rc, EB), F32)
            for j in range(CONV_W):
                src = st_ref[i + j, rows, :] if i + j < n_hist else glu_ref[i + j - n_hist, rows, :]
                acc = acc + src * cw_ref[j:j + 1, :]
            o_ref[i, rows, :] = _layernorm_silu(acc + cb_ref[...], lg_ref[...], lb_ref[...])
            return carry
        lax.fori_loop(0, Bd // rc, chunk, 0)


def _conv_sample(state_t, glu_t, cw, cb, lg, lb, rc):
    T, Bd, _ = glu_t.shape
    return pl.pallas_call(
        functools.partial(_conv_sample_kernel, T=T, rc=rc),
        out_shape=(jax.ShapeDtypeStruct((T, Bd, EB), F32), jax.ShapeDtypeStruct(state_t.shape, F32)),
        compiler_params=pltpu.CompilerParams(vmem_limit_bytes=V7X_VMEM_LIMIT),
        name="conv_sample",
    )(state_t, glu_t, cw, cb, lg, lb)


def kernel(x_prompt, x_sample, mem_prompt, cache_k_win, cache_v_win, state_conv, cache_k_mem, cache_v_mem,
           rel_bias, g_pre, w_in, g_mem, w_mem_kv, conv_w, conv_b, ln_g, ln_b, w_proj_a, w_proj_b,
           w_proj_c, w_out, g_post):
    depth = g_pre.shape[0]
    assert depth == 1, "single-layer step"
    B, S, _ = x_prompt.shape
    Bd, T, _ = x_sample.shape
    wb = cache_k_win.shape[2]
    past = wb
    assert wb == MAX_DIST and S % (16 * QB) == 0 and T <= QROWS

    l = 0
    w_in_b = w_in[l].astype(BF16)
    wpa, wpb, wpc = w_proj_a[l].astype(BF16), w_proj_b[l].astype(BF16), w_proj_c[l].astype(BF16)
    wout = w_out[l].astype(BF16)
    gpre, gpost = g_pre[l][None], g_post[l][None]
    cb, lg, lb = conv_b[l][None], ln_g[l][None], ln_b[l][None]
    cw = conv_w[l]

    (qkv, k_keep, v_keep, za, glu, zb, qc, zc, gt, glu_tail,
     q_s, k_s, v_s, za_s, glu_s, zb_s, qc_s, zc_s, gt_s) = _inproj(
        x_prompt, x_sample.reshape(Bd * T, D_MODEL), gpre, w_in_b, tm=TM_INPROJ)
    mk_f, mv_f, mk_b, mv_b = _memkv(mem_prompt.reshape(B * N_MEM, D_MODEL), g_mem[l][None],
                                    w_mem_kv[l].astype(BF16), tm=TM_MEMKV)
    oa = _band_attn(qkv, _band_tables(rel_bias))

    cache_kt = jnp.transpose(cache_k_win[l], (0, 2, 3, 1))
    cache_vt = jnp.transpose(cache_v_win[l], (0, 2, 3, 1))
    cw_tiles = jnp.broadcast_to(cw[:, None, :], (CONV_W, 8, EB))
    y_p, oa_s, oc_s = _prompt_tail_sample_attn(
        glu, zb, cw_tiles, cb, lg, lb,
        q_s.reshape(Bd, T, EA), k_s.reshape(Bd, T, EA), v_s.reshape(Bd, T, EA),
        cache_kt, cache_vt, _sample_tables(rel_bias, wb, past, T),
        qc_s.reshape(Bd, T, EC), cache_k_mem[l].reshape(Bd, N_MEM * HC, DC),
        cache_v_mem[l].reshape(Bd, N_MEM * HC, DC),
        oa, za, qc, zc, mk_b.reshape(B, N_MEM, EC), mv_b.reshape(B, N_MEM, EC), gt, x_prompt,
        wpa, wpb, wpc, wout, gpost, rc=CONV_TAP_ROWS, rn=CONV_NORM_ROWS)

    state_t = jnp.transpose(state_conv[l], (1, 0, 2))
    glu_t = jnp.transpose(glu_s.reshape(Bd, T, EB), (1, 0, 2))
    ob_t, new_state_t = _conv_sample(state_t, glu_t, cw, cb, lg, lb, rc=CONV_TAP_ROWS)
    ob_s = jnp.transpose(ob_t, (1, 0, 2)).reshape(Bd * T, EB)
    y_s = _tail_sample(oa_s.reshape(Bd * T, EA), za_s, ob_s, zb_s,
                       oc_s.reshape(Bd * T, EC), zc_s, gt_s, x_sample.reshape(Bd * T, D_MODEL),
                       wpa, wpb, wpc, wout, gpost, tm=Bd * T)

    conv_state_s = jnp.transpose(new_state_t, (1, 0, 2))
    return (y_p, y_s.reshape(Bd, T, D_MODEL),
            k_keep.reshape(1, B, wb, HA, DA), v_keep.reshape(1, B, wb, HA, DA),
            glu_tail[:, HALO - (CONV_W - 1):][None],
            mk_f.reshape(1, B, N_MEM, HC, DC), mv_f.reshape(1, B, N_MEM, HC, DC),
            k_s.reshape(1, Bd, T, HA, DA), v_s.reshape(1, Bd, T, HA, DA),
            conv_state_s[None])
```

```python
import functools

import jax
import jax.numpy as jnp
import numpy as np
from jax import lax
from jax.experimental import pallas as pl
from jax.experimental.pallas import tpu as pltpu

F32 = jnp.float32
BF16 = jnp.bfloat16

D_MODEL = 1024
HA, DA = 12, 64
EA = HA * DA
NHP = HA // 2
PATTERNS = ((128, 1), (512, 4), (2048, 16))
NW = 128
QB = 128
EB = 768
CONV_W = 31
HC, DC = 4, 128
EC = HC * DC
N_MEM = 256
N_BUCKETS = 32
MAX_DIST = 2048
EPS = 1e-6
NEG = -1e30
LOG2E = 1.4426950408889634

C_Q, C_K, C_V, C_ZA = 0, EA, 2 * EA, 3 * EA
C_U = 4 * EA
C_G = C_U + EB
C_ZB = C_U + 2 * EB
C_QC = C_ZB + EB
C_ZC = C_QC + EC
C_GT = C_ZC + EC
IN_COLS = C_GT + 3 * D_MODEL

V7X_VMEM_LIMIT = 56 * 1024 * 1024
HALO = 32
QROWS = 8
NT_DIMS = (((1,), (1,)), ((), ()))

TM_INPROJ = 256
TM_MEMKV = 512
CONV_TAP_ROWS = 32
CONV_NORM_ROWS = 64


def _t5_bucket_ids(n):
    exact = N_BUCKETS // 2
    nf = np.maximum(n, 1).astype(np.float32)
    scale = np.float32(N_BUCKETS - exact) / np.log(np.float32(MAX_DIST) / np.float32(exact))
    large = exact + (np.log(nf / np.float32(exact)) * scale).astype(np.int32)
    large = np.minimum(large, N_BUCKETS - 1)
    return np.where(n < exact, n, large).astype(np.int32)


def _bias_by_bucket(rel_bias, ids):
    onehot = (ids[None] == np.arange(N_BUCKETS).reshape((-1,) + (1,) * ids.ndim)).astype(np.float32)
    return jnp.tensordot(rel_bias.astype(F32).T, jnp.asarray(onehot), axes=1,
                         precision=lax.Precision.HIGHEST)


def _sigmoid(x):
    return 1.0 / (1.0 + jnp.exp(-x))


def _silu(x):
    return x * _sigmoid(x)


def _rmsnorm_f32(x, g):
    return x * lax.rsqrt(jnp.mean(x * x, axis=-1, keepdims=True) + EPS) * g


def _params(*sem):
    return pltpu.CompilerParams(dimension_semantics=sem, vmem_limit_bytes=V7X_VMEM_LIMIT)


def _inproj_columns(xn, w_ref):
    def mm(lo, n):
        return jnp.dot(xn, w_ref[:, lo:lo + n], preferred_element_type=F32)
    return mm


def _inproj_kernel(xp_ref, xs_ref, g_ref, w_ref,
                   qkv_ref, kf_ref, vf_ref, za_ref, glu_ref, zb_ref, qc_ref, zc_ref, gt_ref, tail_ref,
                   sq_ref, sk_ref, sv_ref, sza_ref, sglu_ref, szb_ref, sqc_ref, szc_ref, sgt_ref,
                   *, prompt_steps):
    g = pl.program_id(0)
    sample_refs = (sq_ref, sk_ref, sv_ref, sza_ref, sglu_ref, szb_ref, sqc_ref, szc_ref, sgt_ref)

    @pl.when(g == 0)
    def _():
        for r in sample_refs:
            r[...] = jnp.zeros(r.shape, r.dtype)

    @pl.when(g < prompt_steps)
    def _():
        mm = _inproj_columns(_rmsnorm_f32(xp_ref[0], g_ref[...]).astype(BF16), w_ref)

        def put_pairs(res, base):
            for c in range(NHP):
                qkv_ref[0, base + c] = res[:, c * 128:(c + 1) * 128].astype(BF16)

        put_pairs(mm(C_Q, EA) * (DA ** -0.5 * LOG2E), 0)
        k = mm(C_K, EA)
        put_pairs(k, NHP)
        kf_ref[0] = k
        v = mm(C_V, EA)
        put_pairs(v, 2 * NHP)
        vf_ref[0] = v
        za_ref[0] = _silu(mm(C_ZA, EA)).astype(BF16)
        glu = mm(C_U, EB) * _sigmoid(mm(C_G, EB))
        glu_ref[0] = glu.astype(BF16)
        tail_ref[0] = glu[glu.shape[0] - HALO:, :]
        zb_ref[0] = _silu(mm(C_ZB, EB)).astype(BF16)
        qc_ref[0] = (mm(C_QC, EC) * (DC ** -0.5 * LOG2E)).astype(BF16)
        zc_ref[0] = _silu(mm(C_ZC, EC)).astype(BF16)
        for c in range(3):
            gt_ref[0, :, c * D_MODEL:(c + 1) * D_MODEL] = _sigmoid(mm(C_GT + c * D_MODEL, D_MODEL)).astype(BF16)

    @pl.when(g >= prompt_steps)
    def _():
        mm = _inproj_columns(_rmsnorm_f32(xs_ref[...], g_ref[...]).astype(BF16), w_ref)
        sq_ref[...] = mm(C_Q, EA) * (DA ** -0.5 * LOG2E)
        sk_ref[...] = mm(C_K, EA)
        sv_ref[...] = mm(C_V, EA)
        sza_ref[...] = _silu(mm(C_ZA, EA)).astype(BF16)
        sglu_ref[...] = mm(C_U, EB) * _sigmoid(mm(C_G, EB))
        szb_ref[...] = _silu(mm(C_ZB, EB)).astype(BF16)
        sqc_ref[...] = mm(C_QC, EC) * (DC ** -0.5 * LOG2E)
        szc_ref[...] = _silu(mm(C_ZC, EC)).astype(BF16)
        for c in range(3):
            sgt_ref[:, c * D_MODEL:(c + 1) * D_MODEL] = _sigmoid(mm(C_GT + c * D_MODEL, D_MODEL)).astype(BF16)


def _inproj(x, xs, g, w, tm):
    B, S, _ = x.shape
    Ms = xs.shape[0]
    nt = S // tm
    P = B * nt
    wb = min(MAX_DIST, S)
    first_kept = (S - wb) // tm
    pb = lambda s: jnp.minimum(s, P - 1) // nt
    pi = lambda s: jnp.minimum(s, P - 1) % nt
    row = lambda s: (pb(s), pi(s), 0)
    kept = lambda s: (pb(s), jnp.maximum(pi(s) - first_kept, 0), 0)
    srow = lambda s: (jnp.maximum(s - P, 0), 0)
    const = lambda s: (0, 0)
    s_widths = (EA, EA, EA, EA, EB, EB, EC, EC, 3 * D_MODEL)
    s_dtypes = (F32, F32, F32, BF16, F32, BF16, F32, BF16, BF16)
    out_shape = (
        jax.ShapeDtypeStruct((B, 3 * NHP, S, 128), BF16),
        jax.ShapeDtypeStruct((B, wb, EA), F32),
        jax.ShapeDtypeStruct((B, wb, EA), F32),
        jax.ShapeDtypeStruct((B, S, EA), BF16),
        jax.ShapeDtypeStruct((B, S, EB), BF16),
        jax.ShapeDtypeStruct((B, S, EB), BF16),
        jax.ShapeDtypeStruct((B, S, EC), BF16),
        jax.ShapeDtypeStruct((B, S, EC), BF16),
        jax.ShapeDtypeStruct((B, S, 3 * D_MODEL), BF16),
        jax.ShapeDtypeStruct((B, HALO, EB), F32),
    ) + tuple(jax.ShapeDtypeStruct((Ms, n), dt) for n, dt in zip(s_widths, s_dtypes))
    out_specs = (
        pl.BlockSpec((1, 3 * NHP, tm, 128), lambda s: (pb(s), 0, pi(s), 0)),
        pl.BlockSpec((1, tm, EA), kept),
        pl.BlockSpec((1, tm, EA), kept),
        pl.BlockSpec((1, tm, EA), row),
        pl.BlockSpec((1, tm, EB), row),
        pl.BlockSpec((1, tm, EB), row),
        pl.BlockSpec((1, tm, EC), row),
        pl.BlockSpec((1, tm, EC), row),
        pl.BlockSpec((1, tm, 3 * D_MODEL), row),
        pl.BlockSpec((1, HALO, EB), lambda s: (pb(s), 0, 0)),
    ) + tuple(pl.BlockSpec((tm, n), srow) for n in s_widths)
    return pl.pallas_call(
        functools.partial(_inproj_kernel, prompt_steps=P),
        out_shape=out_shape,
        grid=(P + Ms // tm,),
        in_specs=[
            pl.BlockSpec((1, tm, D_MODEL), row),
            pl.BlockSpec((tm, D_MODEL), srow),
            pl.BlockSpec((1, D_MODEL), const),
            pl.BlockSpec((D_MODEL, IN_COLS), const, pipeline_mode=pl.Buffered(1)),
        ],
        out_specs=out_specs,
        compiler_params=_params("arbitrary"),
        name="inproj",
    )(x, xs, g, w)


def _memkv_kernel(m_ref, g_ref, w_ref, kf_ref, vf_ref, kb_ref, vb_ref):
    xn = _rmsnorm_f32(m_ref[...], g_ref[...]).astype(BF16)
    k = jnp.dot(xn, w_ref[:, :EC], preferred_element_type=F32)
    v = jnp.dot(xn, w_ref[:, EC:], preferred_element_type=F32)
    kf_ref[...] = k
    vf_ref[...] = v
    kb_ref[...] = k.astype(BF16)
    vb_ref[...] = v.astype(BF16)


def _memkv(mem, g, w, tm):
    M = mem.shape[0]
    row = lambda i: (i, 0)
    return pl.pallas_call(
        _memkv_kernel,
        out_shape=(jax.ShapeDtypeStruct((M, EC), F32), jax.ShapeDtypeStruct((M, EC), F32),
                   jax.ShapeDtypeStruct((M, EC), BF16), jax.ShapeDtypeStruct((M, EC), BF16)),
        grid=(M // tm,),
        in_specs=[pl.BlockSpec((tm, D_MODEL), row),
                  pl.BlockSpec((1, D_MODEL), lambda i: (0, 0)),
                  pl.BlockSpec((D_MODEL, 2 * EC), lambda i: (0, 0))],
        out_specs=tuple(pl.BlockSpec((tm, EC), row) for _ in range(4)),
        compiler_params=_params("arbitrary"),
        name="memkv",
    )(mem, g, w)


def _band_tables(rel_bias):
    iq = np.arange(QB)[:, None]
    ik = np.arange(2 * QB)[None, :]
    dist = iq - ik + QB
    band = (dist >= 0) & (dist <= NW)
    out = []
    for (_, d) in PATTERNS:
        bias = LOG2E * _bias_by_bucket(rel_bias, _t5_bucket_ids(np.clip(dist, 0, NW) * d))
        later = jnp.where(jnp.asarray(band)[None], bias, NEG)
        first = jnp.where(jnp.asarray(band & (ik >= QB))[None], bias, NEG)
        t = jnp.stack([first, later], axis=1)
        out.append(t.reshape(NHP, 2, 2, QB, 2 * QB).transpose(0, 2, 1, 3, 4)
                   .reshape(NHP, 2, 2 * QB, 2 * QB))
    return jnp.stack(out)


def _band_attn_kernel(q_ref, k_ref, v_ref, t_ref, oa_ref,
                      nat32, p4_32, p4_ref, p16_ref, a4_ref, l4_ref, m4_ref, a16_ref, l16_ref, m16_ref,
                      *, S):
    n_units = S // QB
    lane = lax.broadcasted_iota(jnp.int32, (QB, 128), 1)
    head0 = lane < DA
    keep0 = head0.astype(F32).astype(BF16)
    keep1 = (1.0 - head0.astype(F32)).astype(BF16)
    ones = jnp.ones((2 * QB, 128), BF16)
    n4 = S // 4

    for t, src in enumerate((q_ref, k_ref, v_ref)):
        def widen(c, carry, src=src):
            r0 = pl.multiple_of(c * 256, 256)
            nat32[pl.ds(r0, 256), :] = src[0, 0, pl.ds(r0, 256), :].astype(F32)
            return carry
        lax.fori_loop(0, S // 256, widen, 0)
        for r in range(4):
            def by4(c, carry, r=r, t=t):
                u0 = pl.multiple_of(c * 256, 256)
                x = nat32[pl.ds(4 * u0 + r, 256, stride=4), :]
                p4_32[pl.ds(r * n4 + u0, 256), :] = x
                p4_ref[t, pl.ds(r * n4 + u0, 256), :] = x.astype(BF16)
                return carry
            lax.fori_loop(0, n4 // 256, by4, 0)
        for r4 in range(4):
            for s in range(4):
                x = p4_32[pl.ds(r4 * n4 + s, n4 // 4, stride=4), :]
                p16_ref[t, pl.ds((4 * s + r4) * (n4 // 4), n4 // 4), :] = x.astype(BF16)

    def unit(u, pat, d, qsrc, ksrc, vsrc):
        nqb = n_units // d
        row = pl.multiple_of(u * QB, QB)
        prow = pl.multiple_of(jnp.maximum(u - 1, 0) * QB, QB)
        q2 = qsrc[pl.ds(row, QB), :]
        kk = jnp.concatenate([ksrc[pl.ds(prow, QB), :], ksrc[pl.ds(row, QB), :]], axis=0)
        vv = jnp.concatenate([vsrc[pl.ds(prow, QB), :], vsrc[pl.ds(row, QB), :]], axis=0)
        qs = jnp.concatenate([q2 * keep0, q2 * keep1], axis=0)
        s = lax.dot_general(qs, kk, NT_DIMS, preferred_element_type=F32)
        s = s + t_ref[pat, 0, jnp.minimum(u & (nqb - 1), 1)]
        m = jnp.max(s, axis=-1, keepdims=True)
        p = jnp.exp2(s - m).astype(BF16)
        oe = jnp.dot(p, jnp.concatenate([vv, ones], axis=1), preferred_element_type=F32)
        mb = jnp.broadcast_to(m, (2 * QB, 128))
        return (jnp.where(head0, oe[:QB, :128], oe[QB:, :128]),
                jnp.where(head0, oe[:QB, 128:], oe[QB:, 128:]),
                jnp.where(head0, mb[:QB], mb[QB:]))

    def strided_pattern(pat, d, src, acc_ref, den_ref, max_ref):
        nqb = n_units // d

        def body(u, carry):
            acc, den, mx = unit(u, pat, d, src.at[0], src.at[1], src.at[2])
            start = u // nqb + (u & (nqb - 1)) * (d * QB)
            acc_ref[pl.ds(start, QB, stride=d), :] = acc
            den_ref[pl.ds(start, QB, stride=d), :] = den
            max_ref[pl.ds(start, QB, stride=d), :] = mx
            return carry
        lax.fori_loop(0, n_units, body, 0, unroll=n_units)

    strided_pattern(2, 16, p16_ref, a16_ref, l16_ref, m16_ref)
    strided_pattern(1, 4, p4_ref, a4_ref, l4_ref, m4_ref)

    def dense(u, carry):
        a1, l1, m1 = unit(u, 0, 1, q_ref.at[0, 0], k_ref.at[0, 0], v_ref.at[0, 0])
        rows = pl.ds(pl.multiple_of(u * QB, QB), QB)
        m4, m16 = m4_ref[rows, :], m16_ref[rows, :]
        mx = jnp.maximum(jnp.maximum(m1, m4), m16)
        e1, e4, e16 = jnp.exp2(m1 - mx), jnp.exp2(m4 - mx), jnp.exp2(m16 - mx)
        num = a1 * e1 + a4_ref[rows, :] * e4 + a16_ref[rows, :] * e16
        den = l1 * e1 + l4_ref[rows, :] * e4 + l16_ref[rows, :] * e16
        oa_ref[0, 0, rows, :] = (num / den).astype(BF16)
        return carry
    lax.fori_loop(0, n_units, dense, 0, unroll=n_units)


def _band_attn(qkv, tables):
    B, _, S, _ = qkv.shape
    blk = (1, 1, S, 128)
    return pl.pallas_call(
        functools.partial(_band_attn_kernel, S=S),
        out_shape=jax.ShapeDtypeStruct((B, NHP, S, 128), BF16),
        grid=(B, NHP),
        in_specs=[pl.BlockSpec(blk, lambda b, h: (b, h, 0, 0)),
                  pl.BlockSpec(blk, lambda b, h: (b, NHP + h, 0, 0)),
                  pl.BlockSpec(blk, lambda b, h: (b, 2 * NHP + h, 0, 0)),
                  pl.BlockSpec((3, 1, 2, 2 * QB, 2 * QB), lambda b, h: (0, h, 0, 0, 0))],
        out_specs=pl.BlockSpec(blk, lambda b, h: (b, h, 0, 0)),
        scratch_shapes=[pltpu.VMEM((S, 128), F32), pltpu.VMEM((S, 128), F32),
                        pltpu.VMEM((3, S, 128), BF16), pltpu.VMEM((3, S, 128), BF16)]
                       + [pltpu.VMEM((S, 128), F32)] * 6,
        compiler_params=_params("arbitrary", "arbitrary"),
        name="band_attn",
    )(qkv, qkv, qkv, tables)


def _layernorm_silu(c, g, b):
    mu = jnp.mean(c, axis=-1, keepdims=True)
    cc = c - mu
    var = jnp.mean(cc * cc, axis=-1, keepdims=True)
    return _silu(cc * lax.rsqrt(var + EPS) * g + b)


FIRST_TAP = HALO - (CONV_W - 1)


def _conv_stage(glu_ref, halo_ref, seq_start, full_ref, tm):
    halo = halo_ref[0].astype(F32)
    full_ref[0, 0:HALO, :] = jnp.where(seq_start, jnp.zeros_like(halo), halo)
    full_ref[0, HALO:HALO + tm, :] = glu_ref[0].astype(F32)
    n_shift = HALO + tm - 8
    for s in range(1, 8):
        full_ref[s, 0:n_shift, :] = full_ref[0, s:s + n_shift, :]


def _conv_taps(c, full_ref, cw_ref, conv_ref, rc):
    base = c * rc
    accs = [jnp.zeros((8, EB), F32) for _ in range(rc // 8)]
    for j in range(CONV_W):
        off = FIRST_TAP + j
        w8 = cw_ref[j]
        for g in range(rc // 8):
            accs[g] = accs[g] + full_ref[off % 8, pl.ds(base + (off // 8) * 8 + 8 * g, 8), :] * w8
    conv_ref[pl.ds(base, rc), :] = jnp.concatenate(accs, axis=0)


def _conv_norm_gate(c, conv_ref, zb_ref, cb_ref, lg_ref, lb_ref, o_ref, rn):
    rows = pl.ds(c * rn, rn)
    ob = _layernorm_silu(conv_ref[rows, :] + cb_ref[...], lg_ref[...], lb_ref[...])
    o_ref[rows, :] = (ob * zb_ref[0, rows, :].astype(F32)).astype(BF16)


def _gated_tail(a_g, b_g, c_g, gt_ref, x, wpa_ref, wpb_ref, wpc_ref, wout_ref, gpost_ref):
    def gate(k):
        return gt_ref[:, k * D_MODEL:(k + 1) * D_MODEL].astype(F32)

    mix = gate(0) * jnp.dot(a_g, wpa_ref[...], preferred_element_type=F32)
    mix = mix + gate(1) * jnp.dot(b_g, wpb_ref[...], preferred_element_type=F32)
    mix = mix + gate(2) * jnp.dot(c_g, wpc_ref[...], preferred_element_type=F32)
    z = jnp.dot(mix.astype(BF16), wout_ref[...], preferred_element_type=F32)
    return x + _rmsnorm_f32(z, gpost_ref[...])


def _tail_sample_kernel(oa_ref, za_ref, ob_ref, zb_ref, oc_ref, zc_ref, gt_ref, x_ref,
                        wpa_ref, wpb_ref, wpc_ref, wout_ref, gpost_ref, y_ref):
    a_g = (oa_ref[...] * za_ref[...].astype(F32)).astype(BF16)
    b_g = (ob_ref[...] * zb_ref[...].astype(F32)).astype(BF16)
    c_g = (oc_ref[...] * zc_ref[...].astype(F32)).astype(BF16)
    y_ref[...] = _gated_tail(a_g, b_g, c_g, gt_ref, x_ref[...],
                             wpa_ref, wpb_ref, wpc_ref, wout_ref, gpost_ref)


def _tail_sample(oa, za, ob, zb, oc, zc, gt, x, wpa, wpb, wpc, wout, gpost, tm):
    M = x.shape[0]
    row = lambda i: (i, 0)
    const = lambda i: (0, 0)
    widths = (EA, EA, EB, EB, EC, EC, 3 * D_MODEL, D_MODEL)
    return pl.pallas_call(
        _tail_sample_kernel,
        out_shape=jax.ShapeDtypeStruct((M, D_MODEL), F32),
        grid=(M // tm,),
        in_specs=[pl.BlockSpec((tm, n), row) for n in widths] + [
            pl.BlockSpec((EA, D_MODEL), const), pl.BlockSpec((EB, D_MODEL), const),
            pl.BlockSpec((EC, D_MODEL), const), pl.BlockSpec((D_MODEL, D_MODEL), const),
            pl.BlockSpec((1, D_MODEL), const)],
        out_specs=pl.BlockSpec((tm, D_MODEL), row),
        compiler_params=_params("arbitrary"),
        name="tail_sample",
    )(oa, za, ob, zb, oc, zc, gt, x, wpa, wpb, wpc, wout, gpost)


def _sample_tables(rel_bias, wb, past, T):
    def mult_of(dl, real):
        m = np.zeros(dl.shape, np.float32)
        for (w, d) in PATTERNS:
            m += (real & (dl % d == 0) & (dl <= w)).astype(np.float32)
        return m

    def tables(dl, real):
        row = np.arange(QROWS)[:, None]
        real = real & (dl >= 0) & (past + row - dl >= 0) & (wb + row - dl >= 0)
        m = mult_of(dl, real)
        dlc = np.clip(dl, 0, MAX_DIST)
        m[T:] = m[0]
        bias = jnp.where(jnp.asarray(m > 0)[None],
                         LOG2E * _bias_by_bucket(rel_bias, _t5_bucket_ids(dlc)), NEG)
        return bias, jnp.asarray(m)

    i = np.arange(QROWS)[:, None]
    pos = np.arange(wb)[None, :]
    cache_b, cache_m = tables(wb + i - pos, np.ones((QROWS, wb), bool))
    j = np.arange(QROWS)[None, :]
    new_b, new_m = tables(i - j, np.broadcast_to(j < T, (QROWS, QROWS)))
    return cache_b, cache_m, new_b, new_m


def _pad_rows(x, rows):
    return jnp.concatenate([x, jnp.zeros((rows - x.shape[0], x.shape[1]), x.dtype)], axis=0)


def _sample_heads(heads, qh_ref, knh_ref, vnh_ref, kt_ref, vt_ref, cb_ref, cm_ref, nb_ref, nm_ref, oh_ref):
    scores = []
    for h in heads:
        qh = qh_ref[h].astype(BF16)
        s = jnp.dot(qh, kt_ref[0, h].astype(BF16), preferred_element_type=F32) + cb_ref[h]
        sn = lax.dot_general(qh, knh_ref[h].astype(BF16), NT_DIMS, preferred_element_type=F32) + nb_ref[h]
        scores.append((s, sn))
    probs = []
    for s, sn in scores:
        m = jnp.maximum(jnp.max(s, axis=-1, keepdims=True), jnp.max(sn, axis=-1, keepdims=True))
        p = jnp.exp2(s - m) * cm_ref[...]
        pn = jnp.exp2(sn - m) * nm_ref[...]
        l = jnp.sum(p, axis=-1, keepdims=True) + jnp.sum(pn, axis=-1, keepdims=True)
        probs.append((p.astype(BF16), pn.astype(BF16), l))
    for h, (p, pn, l) in zip(heads, probs):
        o = lax.dot_general(p, vt_ref[0, h].astype(BF16), NT_DIMS, preferred_element_type=F32)
        o = o + jnp.dot(pn, vnh_ref[h].astype(BF16), preferred_element_type=F32)
        oh_ref[h] = o / l


def _sample_cross_attn(qc_ref, mk_ref, mv_ref, xm_ref, oc_ref, T):
    qc8 = _pad_rows(qc_ref[0], QROWS)
    qc_heads = jnp.concatenate([qc8[:, h * DC:(h + 1) * DC] for h in range(HC)], axis=0).astype(BF16)
    sc = lax.dot_general(qc_heads, mk_ref[0].astype(BF16), NT_DIMS, preferred_element_type=F32) + xm_ref[...]
    pc = jnp.exp2(sc - jnp.max(sc, axis=-1, keepdims=True))
    lc = jnp.sum(pc, axis=-1, keepdims=True)
    oc = jnp.dot(pc.astype(BF16), mv_ref[0].astype(BF16), preferred_element_type=F32) / lc
    oc_ref[0] = jnp.concatenate([oc[h * QROWS:h * QROWS + T] for h in range(HC)], axis=-1)


def _conv_attn_tail_kernel(glu_ref, halo_ref, zb_ref, cw_ref, cb_ref, lg_ref, lb_ref,
                           q_ref, kt_ref, vt_ref, kn_ref, vn_ref, tb_ref, tm_ref, nb_ref, nm_ref,
                           qc_ref, mk_ref, mv_ref, xm_ref,
                           oa_ref, za_ref, pqc_ref, pzc_ref, pmk_ref, pmv_ref, gt_ref, x_ref,
                           wpa_ref, wpb_ref, wpc_ref, wout_ref, gpost_ref,
                           y_ref, oas_ref, ocs_ref,
                           full_ref, conv_ref, obg_ref, qh_ref, knh_ref, vnh_ref, oh_ref,
                           *, tm, rc, rn, T, tiles_per_seq):
    g = pl.program_id(0)
    heads = [slice(h * DC, (h + 1) * DC) for h in range(HC)]

    def gate(k):
        return gt_ref[0, :, k * D_MODEL:(k + 1) * D_MODEL].astype(F32)

    _conv_stage(glu_ref, halo_ref, g % tiles_per_seq == 0, full_ref, tm)
    q8, kn8, vn8 = _pad_rows(q_ref[0], QROWS), _pad_rows(kn_ref[0], QROWS), _pad_rows(vn_ref[0], QROWS)
    for h in range(HA):
        hs = slice(h * DA, (h + 1) * DA)
        qh_ref[h], knh_ref[h], vnh_ref[h] = q8[:, hs], kn8[:, hs], vn8[:, hs]
    scores = [lax.dot_general(pqc_ref[0, :, hs], pmk_ref[0, :, hs], NT_DIMS, preferred_element_type=F32)
              for hs in heads]
    a_g = jnp.concatenate(
        [(oa_ref[0, c].astype(F32) * za_ref[0, :, c * 128:(c + 1) * 128].astype(F32)).astype(BF16)
         for c in range(NHP)], axis=-1)
    pa = jnp.dot(a_g, wpa_ref[...], preferred_element_type=F32)
    _sample_cross_attn(qc_ref, mk_ref, mv_ref, xm_ref, ocs_ref, T)
    for c in range(tm // rc):
        _conv_taps(c, full_ref, cw_ref, conv_ref, rc)
    _sample_heads(list(range(HA)), qh_ref, knh_ref, vnh_ref, kt_ref, vt_ref,
                  tb_ref, tm_ref, nb_ref, nm_ref, oh_ref)
    pieces = []
    for s, hs in zip(scores, heads):
        p = jnp.exp2(s - jnp.max(s, axis=-1, keepdims=True))
        l = jnp.sum(p, axis=-1, keepdims=True)
        oc = jnp.dot(p.astype(BF16), pmv_ref[0, :, hs], preferred_element_type=F32) / l
        pieces.append((oc * pzc_ref[0, :, hs].astype(F32)).astype(BF16))
    pc = jnp.dot(jnp.concatenate(pieces, axis=-1), wpc_ref[...], preferred_element_type=F32)
    for c in range(tm // rn):
        _conv_norm_gate(c, conv_ref, zb_ref, cb_ref, lg_ref, lb_ref, obg_ref, rn)
    pb = jnp.dot(obg_ref[...], wpb_ref[...], preferred_element_type=F32)
    mix = gate(0) * pa + gate(1) * pb + gate(2) * pc
    z = jnp.dot(mix.astype(BF16), wout_ref[...], preferred_element_type=F32)
    y_ref[0] = x_ref[0] + _rmsnorm_f32(z, gpost_ref[...])
    oas_ref[0] = jnp.concatenate([oh_ref[h][:T] for h in range(HA)], axis=-1)


def _prompt_tail_sample_attn(glu, zb, cw, cb, lg, lb, q, k_new, v_new, cache_kt, cache_vt, tables,
                             qc, mem_k, mem_v, oa, za, pqc, pzc, pmk, pmv, gt, x,
                             wpa, wpb, wpc, wout, gpost, rc, rn):
    B, S, _ = glu.shape
    Bd, T, _ = q.shape
    wb = cache_kt.shape[-1]
    tm = (B * S) // Bd
    tiles_per_seq = S // tm
    assert B * tiles_per_seq == Bd and tm % rn == 0 and tm % HALO == 0
    cache_b, cache_m, new_b, new_m = tables
    own_head = jnp.asarray(np.where(
        np.arange(HC * QROWS)[:, None] // QROWS == np.arange(N_MEM * HC)[None, :] % HC, 0.0, NEG
    ).astype(np.float32))
    tile = lambda g: (g // tiles_per_seq, g % tiles_per_seq, 0)
    halo = lambda g: (g // tiles_per_seq, jnp.maximum((g % tiles_per_seq) * (tm // HALO) - 1, 0), 0)
    per_b3 = lambda g: (g, 0, 0)
    per_b4 = lambda g: (g, 0, 0, 0)
    c2 = lambda g: (0, 0)
    c3 = lambda g: (0, 0, 0)
    head_scratch = pltpu.VMEM((HA, QROWS, DA), F32)
    pair_tile = lambda g: (g // tiles_per_seq, 0, g % tiles_per_seq, 0)
    per_seq = lambda g: (g // tiles_per_seq, 0, 0)
    once = pl.Buffered(1)
    return pl.pallas_call(
        functools.partial(_conv_attn_tail_kernel, tm=tm, rc=rc, rn=rn, T=T, tiles_per_seq=tiles_per_seq),
        out_shape=(jax.ShapeDtypeStruct((B, S, D_MODEL), F32),
                   jax.ShapeDtypeStruct((Bd, T, EA), F32), jax.ShapeDtypeStruct((Bd, T, EC), F32)),
        grid=(Bd,),
        in_specs=[pl.BlockSpec((1, tm, EB), tile), pl.BlockSpec((1, HALO, EB), halo),
                  pl.BlockSpec((1, tm, EB), tile),
                  pl.BlockSpec((CONV_W, 8, EB), c3),
                  pl.BlockSpec((1, EB), c2), pl.BlockSpec((1, EB), c2), pl.BlockSpec((1, EB), c2),
                  pl.BlockSpec((1, T, EA), per_b3),
                  pl.BlockSpec((1, HA, DA, wb), per_b4), pl.BlockSpec((1, HA, DA, wb), per_b4),
                  pl.BlockSpec((1, T, EA), per_b3), pl.BlockSpec((1, T, EA), per_b3),
                  pl.BlockSpec((HA, QROWS, wb), c3), pl.BlockSpec((QROWS, wb), c2),
                  pl.BlockSpec((HA, QROWS, QROWS), c3), pl.BlockSpec((QROWS, QROWS), c2),
                  pl.BlockSpec((1, T, EC), per_b3),
                  pl.BlockSpec((1, N_MEM * HC, DC), per_b3), pl.BlockSpec((1, N_MEM * HC, DC), per_b3),
                  pl.BlockSpec((HC * QROWS, N_MEM * HC), c2),
                  pl.BlockSpec((1, NHP, tm, 128), pair_tile), pl.BlockSpec((1, tm, EA), tile),
                  pl.BlockSpec((1, tm, EC), tile), pl.BlockSpec((1, tm, EC), tile),
                  pl.BlockSpec((1, N_MEM, EC), per_seq), pl.BlockSpec((1, N_MEM, EC), per_seq),
                  pl.BlockSpec((1, tm, 3 * D_MODEL), tile), pl.BlockSpec((1, tm, D_MODEL), tile),
                  pl.BlockSpec((EA, D_MODEL), c2, pipeline_mode=once),
                  pl.BlockSpec((EB, D_MODEL), c2, pipeline_mode=once),
                  pl.BlockSpec((EC, D_MODEL), c2, pipeline_mode=once),
                  pl.BlockSpec((D_MODEL, D_MODEL), c2, pipeline_mode=once),
                  pl.BlockSpec((1, D_MODEL), c2)],
        out_specs=(pl.BlockSpec((1, tm, D_MODEL), tile),
                   pl.BlockSpec((1, T, EA), per_b3), pl.BlockSpec((1, T, EC), per_b3)),
        scratch_shapes=[pltpu.VMEM((8, HALO + tm, EB), F32), pltpu.VMEM((tm, EB), F32),
                        pltpu.VMEM((tm, EB), BF16),
                        head_scratch, head_scratch, head_scratch, head_scratch],
        compiler_params=_params("arbitrary"),
        name="prompt_tail_sample_attn",
    )(glu, glu, zb, cw, cb, lg, lb, q, cache_kt, cache_vt, k_new, v_new,
      cache_b, cache_m, new_b, new_m, qc, mem_k, mem_v, own_head,
      oa, za, pqc, pzc, pmk, pmv, gt, x, wpa, wpb, wpc, wout, gpost)


def _conv_sample_kernel(st_ref, glu_ref, cw_ref, cb_ref, lg_ref, lb_ref, o_ref, ns_ref, *, T, rc):
    n_hist = CONV_W - 1
    Bd = st_ref.shape[1]

    def shift(t, carry):
        ns_ref[t] = st_ref[t + T]
        return carry
    lax.fori_loop(0, n_hist - T, shift, 0)
    for i in range(T):
        ns_ref[n_hist - T + i] = glu_ref[i]
    for i in range(T):
        def chunk(c, carry, i=i):
            rows = pl.ds(pl.multiple_of(c * rc, rc), rc)
            acc = jnp.zeros((rc, EB), F32)
            for j in range(CONV_W):
                src = st_ref[i + j, rows, :] if i + j < n_hist else glu_ref[i + j - n_hist, rows, :]
                acc = acc + src * cw_ref[j:j + 1, :]
            o_ref[i, rows, :] = _layernorm_silu(acc + cb_ref[...], lg_ref[...], lb_ref[...])
            return carry
        lax.fori_loop(0, Bd // rc, chunk, 0)


def _conv_sample(state_t, glu_t, cw, cb, lg, lb, rc):
    T, Bd, _ = glu_t.shape
    return pl.pallas_call(
        functools.partial(_conv_sample_kernel, T=T, rc=rc),
        out_shape=(jax.ShapeDtypeStruct((T, Bd, EB), F32), jax.ShapeDtypeStruct(state_t.shape, F32)),
        compiler_params=pltpu.CompilerParams(vmem_limit_bytes=V7X_VMEM_LIMIT),
        name="conv_sample",
    )(state_t, glu_t, cw, cb, lg, lb)


def kernel(x_prompt, x_sample, mem_prompt, cache_k_win, cache_v_win, state_conv, cache_k_mem, cache_v_mem,
           rel_bias, g_pre, w_in, g_mem, w_mem_kv, conv_w, conv_b, ln_g, ln_b, w_proj_a, w_proj_b,
           w_proj_c, w_out, g_post):
    depth = g_pre.shape[0]
    assert depth == 1, "single-layer step"
    B, S, _ = x_prompt.shape
    Bd, T, _ = x_sample.shape
    wb = cache_k_win.shape[2]
    past = wb
    assert wb == MAX_DIST and S % (16 * QB) == 0 and T <= QROWS

    l = 0
    w_in_b = w_in[l].astype(BF16)
    wpa, wpb, wpc = w_proj_a[l].astype(BF16), w_proj_b[l].astype(BF16), w_proj_c[l].astype(BF16)
    wout = w_out[l].astype(BF16)
    gpre, gpost = g_pre[l][None], g_post[l][None]
    cb, lg, lb = conv_b[l][None], ln_g[l][None], ln_b[l][None]
    cw = conv_w[l]

    (qkv, k_keep, v_keep, za, glu, zb, qc, zc, gt, glu_tail,
     q_s, k_s, v_s, za_s, glu_s, zb_s, qc_s, zc_s, gt_s) = _inproj(
        x_prompt, x_sample.reshape(Bd * T, D_MODEL), gpre, w_in_b, tm=TM_INPROJ)
    mk_f, mv_f, mk_b, mv_b = _memkv(mem_prompt.reshape(B * N_MEM, D_MODEL), g_mem[l][None],
                                    w_mem_kv[l].astype(BF16), tm=TM_MEMKV)
    oa = _band_attn(qkv, _band_tables(rel_bias))

    cache_kt = jnp.transpose(cache_k_win[l], (0, 2, 3, 1))
    cache_vt = jnp.transpose(cache_v_win[l], (0, 2, 3, 1))
    cw_tiles = jnp.broadcast_to(cw[:, None, :], (CONV_W, 8, EB))
    y_p, oa_s, oc_s = _prompt_tail_sample_attn(
        glu, zb, cw_tiles, cb, lg, lb,
        q_s.reshape(Bd, T, EA), k_s.reshape(Bd, T, EA), v_s.reshape(Bd, T, EA),
        cache_kt, cache_vt, _sample_tables(rel_bias, wb, past, T),
        qc_s.reshape(Bd, T, EC), cache_k_mem[l].reshape(Bd, N_MEM * HC, DC),
        cache_v_mem[l].reshape(Bd, N_MEM * HC, DC),
        oa, za, qc, zc, mk_b.reshape(B, N_MEM, EC), mv_b.reshape(B, N_MEM, EC), gt, x_prompt,
        wpa, wpb, wpc, wout, gpost, rc=CONV_TAP_ROWS, rn=CONV_NORM_ROWS)

    state_t = jnp.transpose(state_conv[l], (1, 0, 2))
    glu_t = jnp.transpose(glu_s.reshape(Bd, T, EB), (1, 0, 2))
    ob_t, new_state_t = _conv_sample(state_t, glu_t, cw, cb, lg, lb, rc=CONV_TAP_ROWS)
    ob_s = jnp.transpose(ob_t, (1, 0, 2)).reshape(Bd * T, EB)
    y_s = _tail_sample(oa_s.reshape(Bd * T, EA), za_s, ob_s, zb_s,
                       oc_s.reshape(Bd * T, EC), zc_s, gt_s, x_sample.reshape(Bd * T, D_MODEL),
                       wpa, wpb, wpc, wout, gpost, tm=Bd * T)

    conv_state_s = jnp.transpose(new_state_t, (1, 0, 2))
    return (y_p, y_s.reshape(Bd, T, D_MODEL),
            k_keep.reshape(1, B, wb, HA, DA), v_keep.reshape(1, B, wb, HA, DA),
            glu_tail[:, HALO - (CONV_W - 1):][None],
            mk_f.reshape(1, B, N_MEM, HC, DC), mv_f.reshape(1, B, N_MEM, HC, DC),
            k_s.reshape(1, Bd, T, HA, DA), v_s.reshape(1, Bd, T, HA, DA),
            conv_state_s[None])
```

```python
import functools

import jax
import jax.numpy as jnp
import numpy as np
from jax import lax
from jax.experimental import pallas as pl
from jax.experimental.pallas import tpu as pltpu

F32 = jnp.float32
BF16 = jnp.bfloat16

D_MODEL = 1024
HA, DA = 12, 64
EA = HA * DA
NHP = HA // 2
PATTERNS = ((128, 1), (512, 4), (2048, 16))
NW = 128
QB = 128
EB = 768
CONV_W = 31
HC, DC = 4, 128
EC = HC * DC
N_MEM = 256
N_BUCKETS = 32
MAX_DIST = 2048
EPS = 1e-6
NEG = -1e30
LOG2E = 1.4426950408889634

C_Q, C_K, C_V, C_ZA = 0, EA, 2 * EA, 3 * EA
C_U = 4 * EA
C_G = C_U + EB
C_ZB = C_U + 2 * EB
C_QC = C_ZB + EB
C_ZC = C_QC + EC
C_GT = C_ZC + EC
IN_COLS = C_GT + 3 * D_MODEL

V7X_VMEM_LIMIT = 56 * 1024 * 1024
HALO = 32
QROWS = 8
NT_DIMS = (((1,), (1,)), ((), ()))

TM_INPROJ = 256
TM_MEMKV = 512
CONV_TAP_ROWS = 32
CONV_NORM_ROWS = 64


def _t5_bucket_ids(n):
    exact = N_BUCKETS // 2
    nf = np.maximum(n, 1).astype(np.float32)
    scale = np.float32(N_BUCKETS - exact) / np.log(np.float32(MAX_DIST) / np.float32(exact))
    large = exact + (np.log(nf / np.float32(exact)) * scale).astype(np.int32)
    large = np.minimum(large, N_BUCKETS - 1)
    return np.where(n < exact, n, large).astype(np.int32)


def _bias_by_bucket(rel_bias, ids):
    onehot = (ids[None] == np.arange(N_BUCKETS).reshape((-1,) + (1,) * ids.ndim)).astype(np.float32)
    return jnp.tensordot(rel_bias.astype(F32).T, jnp.asarray(onehot), axes=1,
                         precision=lax.Precision.HIGHEST)


def _sigmoid(x):
    return 1.0 / (1.0 + jnp.exp(-x))


def _silu(x):
    return x * _sigmoid(x)


def _rmsnorm_f32(x, g):
    return x * lax.rsqrt(jnp.mean(x * x, axis=-1, keepdims=True) + EPS) * g


def _params(*sem):
    return pltpu.CompilerParams(dimension_semantics=sem, vmem_limit_bytes=V7X_VMEM_LIMIT)


def _inproj_columns(xn, w_ref):
    def mm(lo, n):
        return jnp.dot(xn, w_ref[:, lo:lo + n], preferred_element_type=F32)
    return mm


def _inproj_kernel(xp_ref, xs_ref, g_ref, w_ref,
                   qkv_ref, kf_ref, vf_ref, za_ref, glu_ref, zb_ref, qc_ref, zc_ref, gt_ref, tail_ref,
                   sq_ref, sk_ref, sv_ref, sza_ref, sglu_ref, szb_ref, sqc_ref, szc_ref, sgt_ref,
                   *, prompt_steps):
    g = pl.program_id(0)
    sample_refs = (sq_ref, sk_ref, sv_ref, sza_ref, sglu_ref, szb_ref, sqc_ref, szc_ref, sgt_ref)

    @pl.when(g == 0)
    def _():
        for r in sample_refs:
            r[...] = jnp.zeros(r.shape, r.dtype)

    @pl.when(g < prompt_steps)
    def _():
        mm = _inproj_columns(_rmsnorm_f32(xp_ref[0], g_ref[...]).astype(BF16), w_ref)

        def put_pairs(res, base):
            for c in range(NHP):
                qkv_ref[0, base + c] = res[:, c * 128:(c + 1) * 128].astype(BF16)

        put_pairs(mm(C_Q, EA) * (DA ** -0.5 * LOG2E), 0)
        k = mm(C_K, EA)
        put_pairs(k, NHP)
        kf_ref[0] = k
        v = mm(C_V, EA)
        put_pairs(v, 2 * NHP)
        vf_ref[0] = v
        za_ref[0] = _silu(mm(C_ZA, EA)).astype(BF16)
        glu = mm(C_U, EB) * _sigmoid(mm(C_G, EB))
        glu_ref[0] = glu.astype(BF16)
        tail_ref[0] = glu[glu.shape[0] - HALO:, :]
        zb_ref[0] = _silu(mm(C_ZB, EB)).astype(BF16)
        qc_ref[0] = (mm(C_QC, EC) * (DC ** -0.5 * LOG2E)).astype(BF16)
        zc_ref[0] = _silu(mm(C_ZC, EC)).astype(BF16)
        for c in range(3):
            gt_ref[0, :, c * D_MODEL:(c + 1) * D_MODEL] = _sigmoid(mm(C_GT + c * D_MODEL, D_MODEL)).astype(BF16)

    @pl.when(g >= prompt_steps)
    def _():
        mm = _inproj_columns(_rmsnorm_f32(xs_ref[...], g_ref[...]).astype(BF16), w_ref)
        sq_ref[...] = mm(C_Q, EA) * (DA ** -0.5 * LOG2E)
        sk_ref[...] = mm(C_K, EA)
        sv_ref[...] = mm(C_V, EA)
        sza_ref[...] = _silu(mm(C_ZA, EA)).astype(BF16)
        sglu_ref[...] = mm(C_U, EB) * _sigmoid(mm(C_G, EB))
        szb_ref[...] = _silu(mm(C_ZB, EB)).astype(BF16)
        sqc_ref[...] = mm(C_QC, EC) * (DC ** -0.5 * LOG2E)
        szc_ref[...] = _silu(mm(C_ZC, EC)).astype(BF16)
        for c in range(3):
            sgt_ref[:, c * D_MODEL:(c + 1) * D_MODEL] = _sigmoid(mm(C_GT + c * D_MODEL, D_MODEL)).astype(BF16)


def _inproj(x, xs, g, w, tm):
    B, S, _ = x.shape
    Ms = xs.shape[0]
    nt = S // tm
    P = B * nt
    wb = min(MAX_DIST, S)
    first_kept = (S - wb) // tm
    pb = lambda s: jnp.minimum(s, P - 1) // nt
    pi = lambda s: jnp.minimum(s, P - 1) % nt
    row = lambda s: (pb(s), pi(s), 0)
    kept = lambda s: (pb(s), jnp.maximum(pi(s) - first_kept, 0), 0)
    srow = lambda s: (jnp.maximum(s - P, 0), 0)
    const = lambda s: (0, 0)
    s_widths = (EA, EA, EA, EA, EB, EB, EC, EC, 3 * D_MODEL)
    s_dtypes = (F32, F32, F32, BF16, F32, BF16, F32, BF16, BF16)
    out_shape = (
        jax.ShapeDtypeStruct((B, 3 * NHP, S, 128), BF16),
        jax.ShapeDtypeStruct((B, wb, EA), F32),
        jax.ShapeDtypeStruct((B, wb, EA), F32),
        jax.ShapeDtypeStruct((B, S, EA), BF16),
        jax.ShapeDtypeStruct((B, S, EB), BF16),
        jax.ShapeDtypeStruct((B, S, EB), BF16),
        jax.ShapeDtypeStruct((B, S, EC), BF16),
        jax.ShapeDtypeStruct((B, S, EC), BF16),
        jax.ShapeDtypeStruct((B, S, 3 * D_MODEL), BF16),
        jax.ShapeDtypeStruct((B, HALO, EB), F32),
    ) + tuple(jax.ShapeDtypeStruct((Ms, n), dt) for n, dt in zip(s_widths, s_dtypes))
    out_specs = (
        pl.BlockSpec((1, 3 * NHP, tm, 128), lambda s: (pb(s), 0, pi(s), 0)),
        pl.BlockSpec((1, tm, EA), kept),
        pl.BlockSpec((1, tm, EA), kept),
        pl.BlockSpec((1, tm, EA), row),
        pl.BlockSpec((1, tm, EB), row),
        pl.BlockSpec((1, tm, EB), row),
        pl.BlockSpec((1, tm, EC), row),
        pl.BlockSpec((1, tm, EC), row),
        pl.BlockSpec((1, tm, 3 * D_MODEL), row),
        pl.BlockSpec((1, HALO, EB), lambda s: (pb(s), 0, 0)),
    ) + tuple(pl.BlockSpec((tm, n), srow) for n in s_widths)
    return pl.pallas_call(
        functools.partial(_inproj_kernel, prompt_steps=P),
        out_shape=out_shape,
        grid=(P + Ms // tm,),
        in_specs=[
            pl.BlockSpec((1, tm, D_MODEL), row),
            pl.BlockSpec((tm, D_MODEL), srow),
            pl.BlockSpec((1, D_MODEL), const),
            pl.BlockSpec((D_MODEL, IN_COLS), const, pipeline_mode=pl.Buffered(1)),
        ],
        out_specs=out_specs,
        compiler_params=_params("arbitrary"),
        name="inproj",
    )(x, xs, g, w)


def _memkv_kernel(m_ref, g_ref, w_ref, kf_ref, vf_ref, kb_ref, vb_ref):
    xn = _rmsnorm_f32(m_ref[...], g_ref[...]).astype(BF16)
    k = jnp.dot(xn, w_ref[:, :EC], preferred_element_type=F32)
    v = jnp.dot(xn, w_ref[:, EC:], preferred_element_type=F32)
    kf_ref[...] = k
    vf_ref[...] = v
    kb_ref[...] = k.astype(BF16)
    vb_ref[...] = v.astype(BF16)


def _memkv(mem, g, w, tm):
    M = mem.shape[0]
    row = lambda i: (i, 0)
    return pl.pallas_call(
        _memkv_kernel,
        out_shape=(jax.ShapeDtypeStruct((M, EC), F32), jax.ShapeDtypeStruct((M, EC), F32),
                   jax.ShapeDtypeStruct((M, EC), BF16), jax.ShapeDtypeStruct((M, EC), BF16)),
        grid=(M // tm,),
        in_specs=[pl.BlockSpec((tm, D_MODEL), row),
                  pl.BlockSpec((1, D_MODEL), lambda i: (0, 0)),
                  pl.BlockSpec((D_MODEL, 2 * EC), lambda i: (0, 0))],
        out_specs=tuple(pl.BlockSpec((tm, EC), row) for _ in range(4)),
        compiler_params=_params("arbitrary"),
        name="memkv",
    )(mem, g, w)


def _band_tables(rel_bias):
    iq = np.arange(QB)[:, None]
    ik = np.arange(2 * QB)[None, :]
    dist = iq - ik + QB
    band = (dist >= 0) & (dist <= NW)
    out = []
    for (_, d) in PATTERNS:
        bias = LOG2E * _bias_by_bucket(rel_bias, _t5_bucket_ids(np.clip(dist, 0, NW) * d))
        later = jnp.where(jnp.asarray(band)[None], bias, NEG)
        first = jnp.where(jnp.asarray(band & (ik >= QB))[None], bias, NEG)
        t = jnp.stack([first, later], axis=1)
        out.append(t.reshape(NHP, 2, 2, QB, 2 * QB).transpose(0, 2, 1, 3, 4)
                   .reshape(NHP, 2, 2 * QB, 2 * QB))
    return jnp.stack(out)


def _band_attn_kernel(q_ref, k_ref, v_ref, t_ref, oa_ref,
                      nat32, p4_32, p4_ref, p16_ref, a4_ref, l4_ref, m4_ref, a16_ref, l16_ref, m16_ref,
                      *, S):
    n_units = S // QB
    lane = lax.broadcasted_iota(jnp.int32, (QB, 128), 1)
    head0 = lane < DA
    keep0 = head0.astype(F32).astype(BF16)
    keep1 = (1.0 - head0.astype(F32)).astype(BF16)
    ones = jnp.ones((2 * QB, 128), BF16)
    n4 = S // 4

    for t, src in enumerate((q_ref, k_ref, v_ref)):
        def widen(c, carry, src=src):
            r0 = pl.multiple_of(c * 256, 256)
            nat32[pl.ds(r0, 256), :] = src[0, 0, pl.ds(r0, 256), :].astype(F32)
            return carry
        lax.fori_loop(0, S // 256, widen, 0)
        for r in range(4):
            def by4(c, carry, r=r, t=t):
                u0 = pl.multiple_of(c * 256, 256)
                x = nat32[pl.ds(4 * u0 + r, 256, stride=4), :]
                p4_32[pl.ds(r * n4 + u0, 256), :] = x
                p4_ref[t, pl.ds(r * n4 + u0, 256), :] = x.astype(BF16)
                return carry
            lax.fori_loop(0, n4 // 256, by4, 0)
        for r4 in range(4):
            for s in range(4):
                x = p4_32[pl.ds(r4 * n4 + s, n4 // 4, stride=4), :]
                p16_ref[t, pl.ds((4 * s + r4) * (n4 // 4), n4 // 4), :] = x.astype(BF16)

    def unit(u, pat, d, qsrc, ksrc, vsrc):
        nqb = n_units // d
        row = pl.multiple_of(u * QB, QB)
        prow = pl.multiple_of(jnp.maximum(u - 1, 0) * QB, QB)
        q2 = qsrc[pl.ds(row, QB), :]
        kk = jnp.concatenate([ksrc[pl.ds(prow, QB), :], ksrc[pl.ds(row, QB), :]], axis=0)
        vv = jnp.concatenate([vsrc[pl.ds(prow, QB), :], vsrc[pl.ds(row, QB), :]], axis=0)
        qs = jnp.concatenate([q2 * keep0, q2 * keep1], axis=0)
        s = lax.dot_general(qs, kk, NT_DIMS, preferred_element_type=F32)
        s = s + t_ref[pat, 0, jnp.minimum(u & (nqb - 1), 1)]
        m = jnp.max(s, axis=-1, keepdims=True)
        p = jnp.exp2(s - m).astype(BF16)
        oe = jnp.dot(p, jnp.concatenate([vv, ones], axis=1), preferred_element_type=F32)
        mb = jnp.broadcast_to(m, (2 * QB, 128))
        return (jnp.where(head0, oe[:QB, :128], oe[QB:, :128]),
                jnp.where(head0, oe[:QB, 128:], oe[QB:, 128:]),
                jnp.where(head0, mb[:QB], mb[QB:]))

    def strided_pattern(pat, d, src, acc_ref, den_ref, max_ref):
        nqb = n_units // d

        def body(u, carry):
            acc, den, mx = unit(u, pat, d, src.at[0], src.at[1], src.at[2])
            start = u // nqb + (u & (nqb - 1)) * (d * QB)
            acc_ref[pl.ds(start, QB, stride=d), :] = acc
            den_ref[pl.ds(start, QB, stride=d), :] = den
            max_ref[pl.ds(start, QB, stride=d), :] = mx
            return carry
        lax.fori_loop(0, n_units, body, 0, unroll=n_units)

    strided_pattern(2, 16, p16_ref, a16_ref, l16_ref, m16_ref)
    strided_pattern(1, 4, p4_ref, a4_ref, l4_ref, m4_ref)

    def dense(u, carry):
        a1, l1, m1 = unit(u, 0, 1, q_ref.at[0, 0], k_ref.at[0, 0], v_ref.at[0, 0])
        rows = pl.ds(pl.multiple_of(u * QB, QB), QB)
        m4, m16 = m4_ref[rows, :], m16_ref[rows, :]
        mx = jnp.maximum(jnp.maximum(m1, m4), m16)
        e1, e4, e16 = jnp.exp2(m1 - mx), jnp.exp2(m4 - mx), jnp.exp2(m16 - mx)
        num = a1 * e1 + a4_ref[rows, :] * e4 + a16_ref[rows, :] * e16
        den = l1 * e1 + l4_ref[rows, :] * e4 + l16_ref[rows, :] * e16
        oa_ref[0, 0, rows, :] = (num / den).astype(BF16)
        return carry
    lax.fori_loop(0, n_units, dense, 0, unroll=n_units)


def _band_attn(qkv, tables):
    B, _, S, _ = qkv.shape
    blk = (1, 1, S, 128)
    return pl.pallas_call(
        functools.partial(_band_attn_kernel, S=S),
        out_shape=jax.ShapeDtypeStruct((B, NHP, S, 128), BF16),
        grid=(B, NHP),
        in_specs=[pl.BlockSpec(blk, lambda b, h: (b, h, 0, 0)),
                  pl.BlockSpec(blk, lambda b, h: (b, NHP + h, 0, 0)),
                  pl.BlockSpec(blk, lambda b, h: (b, 2 * NHP + h, 0, 0)),
                  pl.BlockSpec((3, 1, 2, 2 * QB, 2 * QB), lambda b, h: (0, h, 0, 0, 0))],
        out_specs=pl.BlockSpec(blk, lambda b, h: (b, h, 0, 0)),
        scratch_shapes=[pltpu.VMEM((S, 128), F32), pltpu.VMEM((S, 128), F32),
                        pltpu.VMEM((3, S, 128), BF16), pltpu.VMEM((3, S, 128), BF16)]
                       + [pltpu.VMEM((S, 128), F32)] * 6,
        compiler_params=_params("arbitrary", "arbitrary"),
        name="band_attn",
    )(qkv, qkv, qkv, tables)


def _layernorm_silu(c, g, b):
    mu = jnp.mean(c, axis=-1, keepdims=True)
    cc = c - mu
    var = jnp.mean(cc * cc, axis=-1, keepdims=True)
    return _silu(cc * lax.rsqrt(var + EPS) * g + b)


FIRST_TAP = HALO - (CONV_W - 1)


SHIFT_K = 256


def _shift_matrix(tm):
    n_shift = HALO + tm - 8
    rows = -(-7 * n_shift // 16) * 16
    p = np.zeros((rows, SHIFT_K), np.float32)
    for s in range(1, 8):
        i = np.arange(n_shift)
        p[(s - 1) * n_shift + i, i + s] = 1.0
    return jnp.asarray(p, BF16)


def _conv_stage(glu_ref, halo_ref, shift_ref, seq_start, full_ref, tm):
    halo = halo_ref[0]
    halo = jnp.where(seq_start, jnp.zeros_like(halo), halo)
    staged = jnp.concatenate(
        [halo, glu_ref[0], jnp.zeros((SHIFT_K - HALO - tm, EB), BF16)], axis=0)
    full_ref[0, :, :] = staged[:HALO + tm].astype(F32)
    n_shift = HALO + tm - 8
    shifted = jnp.dot(shift_ref[...], staged, preferred_element_type=F32)
    for s in range(1, 8):
        full_ref[s, 0:n_shift, :] = shifted[(s - 1) * n_shift:s * n_shift]


def _conv_taps(c, full_ref, cw_ref, conv_ref, rc):
    base = c * rc
    accs = [jnp.zeros((8, EB), F32) for _ in range(rc // 8)]
    for j in range(CONV_W):
        off = FIRST_TAP + j
        w8 = cw_ref[j]
        for g in range(rc // 8):
            accs[g] = accs[g] + full_ref[off % 8, pl.ds(base + (off // 8) * 8 + 8 * g, 8), :] * w8
    conv_ref[pl.ds(base, rc), :] = jnp.concatenate(accs, axis=0)


def _conv_norm_gate(c, conv_ref, zb_ref, cb_ref, lg_ref, lb_ref, o_ref, rn):
    rows = pl.ds(c * rn, rn)
    ob = _layernorm_silu(conv_ref[rows, :] + cb_ref[...], lg_ref[...], lb_ref[...])
    o_ref[rows, :] = (ob * zb_ref[0, rows, :].astype(F32)).astype(BF16)


def _gated_tail(a_g, b_g, c_g, gt_ref, x, wpa_ref, wpb_ref, wpc_ref, wout_ref, gpost_ref):
    def gate(k):
        return gt_ref[:, k * D_MODEL:(k + 1) * D_MODEL].astype(F32)

    mix = gate(0) * jnp.dot(a_g, wpa_ref[...], preferred_element_type=F32)
    mix = mix + gate(1) * jnp.dot(b_g, wpb_ref[...], preferred_element_type=F32)
    mix = mix + gate(2) * jnp.dot(c_g, wpc_ref[...], preferred_element_type=F32)
    z = jnp.dot(mix.astype(BF16), wout_ref[...], preferred_element_type=F32)
    return x + _rmsnorm_f32(z, gpost_ref[...])


def _tail_sample_kernel(oa_ref, za_ref, ob_ref, zb_ref, oc_ref, zc_ref, gt_ref, x_ref,
                        wpa_ref, wpb_ref, wpc_ref, wout_ref, gpost_ref, y_ref):
    a_g = (oa_ref[...] * za_ref[...].astype(F32)).astype(BF16)
    b_g = (ob_ref[...] * zb_ref[...].astype(F32)).astype(BF16)
    c_g = (oc_ref[...] * zc_ref[...].astype(F32)).astype(BF16)
    y_ref[...] = _gated_tail(a_g, b_g, c_g, gt_ref, x_ref[...],
                             wpa_ref, wpb_ref, wpc_ref, wout_ref, gpost_ref)


def _tail_sample(oa, za, ob, zb, oc, zc, gt, x, wpa, wpb, wpc, wout, gpost, tm):
    M = x.shape[0]
    row = lambda i: (i, 0)
    const = lambda i: (0, 0)
    widths = (EA, EA, EB, EB, EC, EC, 3 * D_MODEL, D_MODEL)
    return pl.pallas_call(
        _tail_sample_kernel,
        out_shape=jax.ShapeDtypeStruct((M, D_MODEL), F32),
        grid=(M // tm,),
        in_specs=[pl.BlockSpec((tm, n), row) for n in widths] + [
            pl.BlockSpec((EA, D_MODEL), const), pl.BlockSpec((EB, D_MODEL), const),
            pl.BlockSpec((EC, D_MODEL), const), pl.BlockSpec((D_MODEL, D_MODEL), const),
            pl.BlockSpec((1, D_MODEL), const)],
        out_specs=pl.BlockSpec((tm, D_MODEL), row),
        compiler_params=_params("arbitrary"),
        name="tail_sample",
    )(oa, za, ob, zb, oc, zc, gt, x, wpa, wpb, wpc, wout, gpost)


def _sample_tables(rel_bias, wb, past, T):
    def mult_of(dl, real):
        m = np.zeros(dl.shape, np.float32)
        for (w, d) in PATTERNS:
            m += (real & (dl % d == 0) & (dl <= w)).astype(np.float32)
        return m

    def tables(dl, real):
        row = np.arange(QROWS)[:, None]
        real = real & (dl >= 0) & (past + row - dl >= 0) & (wb + row - dl >= 0)
        m = mult_of(dl, real)
        dlc = np.clip(dl, 0, MAX_DIST)
        m[T:] = m[0]
        bias = jnp.where(jnp.asarray(m > 0)[None],
                         LOG2E * _bias_by_bucket(rel_bias, _t5_bucket_ids(dlc)), NEG)
        return bias, jnp.asarray(m)

    i = np.arange(QROWS)[:, None]
    pos = np.arange(wb)[None, :]
    cache_b, cache_m = tables(wb + i - pos, np.ones((QROWS, wb), bool))
    j = np.arange(QROWS)[None, :]
    new_b, new_m = tables(i - j, np.broadcast_to(j < T, (QROWS, QROWS)))
    return cache_b, cache_m, new_b, new_m


def _pad_rows(x, rows):
    return jnp.concatenate([x, jnp.zeros((rows - x.shape[0], x.shape[1]), x.dtype)], axis=0)


def _sample_heads(heads, qh_ref, knh_ref, vnh_ref, kt_ref, vt_ref, cb_ref, cm_ref, nb_ref, nm_ref, oh_ref):
    scores = []
    for h in heads:
        qh = qh_ref[h].astype(BF16)
        s = jnp.dot(qh, kt_ref[0, h].astype(BF16), preferred_element_type=F32) + cb_ref[h]
        sn = lax.dot_general(qh, knh_ref[h].astype(BF16), NT_DIMS, preferred_element_type=F32) + nb_ref[h]
        scores.append((s, sn))
    probs = []
    for s, sn in scores:
        m = jnp.maximum(jnp.max(s, axis=-1, keepdims=True), jnp.max(sn, axis=-1, keepdims=True))
        p = jnp.exp2(s - m) * cm_ref[...]
        pn = jnp.exp2(sn - m) * nm_ref[...]
        l = jnp.sum(p, axis=-1, keepdims=True) + jnp.sum(pn, axis=-1, keepdims=True)
        probs.append((p.astype(BF16), pn.astype(BF16), l))
    for h, (p, pn, l) in zip(heads, probs):
        o = lax.dot_general(p, vt_ref[0, h].astype(BF16), NT_DIMS, preferred_element_type=F32)
        o = o + jnp.dot(pn, vnh_ref[h].astype(BF16), preferred_element_type=F32)
        oh_ref[h] = o / l


def _sample_cross_attn(qc_ref, mk_ref, mv_ref, xm_ref, oc_ref, T):
    qc8 = _pad_rows(qc_ref[0], QROWS)
    qc_heads = jnp.concatenate([qc8[:, h * DC:(h + 1) * DC] for h in range(HC)], axis=0).astype(BF16)
    sc = lax.dot_general(qc_heads, mk_ref[0].astype(BF16), NT_DIMS, preferred_element_type=F32) + xm_ref[...]
    pc = jnp.exp2(sc - jnp.max(sc, axis=-1, keepdims=True))
    lc = jnp.sum(pc, axis=-1, keepdims=True)
    oc = jnp.dot(pc.astype(BF16), mv_ref[0].astype(BF16), preferred_element_type=F32) / lc
    oc_ref[0] = jnp.concatenate([oc[h * QROWS:h * QROWS + T] for h in range(HC)], axis=-1)


def _conv_attn_tail_kernel(glu_ref, halo_ref, zb_ref, cw_ref, cb_ref, lg_ref, lb_ref, shift_ref,
                           q_ref, kt_ref, vt_ref, kn_ref, vn_ref, tb_ref, tm_ref, nb_ref, nm_ref,
                           qc_ref, mk_ref, mv_ref, xm_ref,
                           oa_ref, za_ref, pqc_ref, pzc_ref, pmk_ref, pmv_ref, gt_ref, x_ref,
                           wpa_ref, wpb_ref, wpc_ref, wout_ref, gpost_ref,
                           y_ref, oas_ref, ocs_ref,
                           full_ref, conv_ref, obg_ref, qh_ref, knh_ref, vnh_ref, oh_ref,
                           *, tm, rc, rn, T, tiles_per_seq):
    g = pl.program_id(0)
    heads = [slice(h * DC, (h + 1) * DC) for h in range(HC)]

    def gate(k):
        return gt_ref[0, :, k * D_MODEL:(k + 1) * D_MODEL].astype(F32)

    _conv_stage(glu_ref, halo_ref, shift_ref, g % tiles_per_seq == 0, full_ref, tm)
    q8, kn8, vn8 = _pad_rows(q_ref[0], QROWS), _pad_rows(kn_ref[0], QROWS), _pad_rows(vn_ref[0], QROWS)
    for h in range(HA):
        hs = slice(h * DA, (h + 1) * DA)
        qh_ref[h], knh_ref[h], vnh_ref[h] = q8[:, hs], kn8[:, hs], vn8[:, hs]
    scores = [lax.dot_general(pqc_ref[0, :, hs], pmk_ref[0, :, hs], NT_DIMS, preferred_element_type=F32)
              for hs in heads]
    a_g = jnp.concatenate(
        [(oa_ref[0, c].astype(F32) * za_ref[0, :, c * 128:(c + 1) * 128].astype(F32)).astype(BF16)
         for c in range(NHP)], axis=-1)
    pa = jnp.dot(a_g, wpa_ref[...], preferred_element_type=F32)
    _sample_cross_attn(qc_ref, mk_ref, mv_ref, xm_ref, ocs_ref, T)
    for c in range(tm // rc):
        _conv_taps(c, full_ref, cw_ref, conv_ref, rc)
    _sample_heads(list(range(HA)), qh_ref, knh_ref, vnh_ref, kt_ref, vt_ref,
                  tb_ref, tm_ref, nb_ref, nm_ref, oh_ref)
    pieces = []
    for s, hs in zip(scores, heads):
        p = jnp.exp2(s - jnp.max(s, axis=-1, keepdims=True))
        l = jnp.sum(p, axis=-1, keepdims=True)
        oc = jnp.dot(p.astype(BF16), pmv_ref[0, :, hs], preferred_element_type=F32) / l
        pieces.append((oc * pzc_ref[0, :, hs].astype(F32)).astype(BF16))
    pc = jnp.dot(jnp.concatenate(pieces, axis=-1), wpc_ref[...], preferred_element_type=F32)
    for c in range(tm // rn):
        _conv_norm_gate(c, conv_ref, zb_ref, cb_ref, lg_ref, lb_ref, obg_ref, rn)
    pb = jnp.dot(obg_ref[...], wpb_ref[...], preferred_element_type=F32)
    mix = gate(0) * pa + gate(1) * pb + gate(2) * pc
    z = jnp.dot(mix.astype(BF16), wout_ref[...], preferred_element_type=F32)
    y_ref[0] = x_ref[0] + _rmsnorm_f32(z, gpost_ref[...])
    oas_ref[0] = jnp.concatenate([oh_ref[h][:T] for h in range(HA)], axis=-1)


def _prompt_tail_sample_attn(glu, zb, cw, cb, lg, lb, q, k_new, v_new, cache_kt, cache_vt, tables,
                             qc, mem_k, mem_v, oa, za, pqc, pzc, pmk, pmv, gt, x,
                             wpa, wpb, wpc, wout, gpost, rc, rn):
    B, S, _ = glu.shape
    Bd, T, _ = q.shape
    wb = cache_kt.shape[-1]
    tm = (B * S) // Bd
    tiles_per_seq = S // tm
    assert B * tiles_per_seq == Bd and tm % rn == 0 and tm % HALO == 0
    cache_b, cache_m, new_b, new_m = tables
    own_head = jnp.asarray(np.where(
        np.arange(HC * QROWS)[:, None] // QROWS == np.arange(N_MEM * HC)[None, :] % HC, 0.0, NEG
    ).astype(np.float32))
    tile = lambda g: (g // tiles_per_seq, g % tiles_per_seq, 0)
    halo = lambda g: (g // tiles_per_seq, jnp.maximum((g % tiles_per_seq) * (tm // HALO) - 1, 0), 0)
    per_b3 = lambda g: (g, 0, 0)
    per_b4 = lambda g: (g, 0, 0, 0)
    c2 = lambda g: (0, 0)
    c3 = lambda g: (0, 0, 0)
    head_scratch = pltpu.VMEM((HA, QROWS, DA), F32)
    pair_tile = lambda g: (g // tiles_per_seq, 0, g % tiles_per_seq, 0)
    per_seq = lambda g: (g // tiles_per_seq, 0, 0)
    once = pl.Buffered(1)
    assert HALO + tm <= SHIFT_K
    shift = _shift_matrix(tm)
    return pl.pallas_call(
        functools.partial(_conv_attn_tail_kernel, tm=tm, rc=rc, rn=rn, T=T, tiles_per_seq=tiles_per_seq),
        out_shape=(jax.ShapeDtypeStruct((B, S, D_MODEL), F32),
                   jax.ShapeDtypeStruct((Bd, T, EA), F32), jax.ShapeDtypeStruct((Bd, T, EC), F32)),
        grid=(Bd,),
        in_specs=[pl.BlockSpec((1, tm, EB), tile), pl.BlockSpec((1, HALO, EB), halo),
                  pl.BlockSpec((1, tm, EB), tile),
                  pl.BlockSpec((CONV_W, 8, EB), c3),
                  pl.BlockSpec((1, EB), c2), pl.BlockSpec((1, EB), c2), pl.BlockSpec((1, EB), c2),
                  pl.BlockSpec(shift.shape, c2, pipeline_mode=once),
                  pl.BlockSpec((1, T, EA), per_b3),
                  pl.BlockSpec((1, HA, DA, wb), per_b4), pl.BlockSpec((1, HA, DA, wb), per_b4),
                  pl.BlockSpec((1, T, EA), per_b3), pl.BlockSpec((1, T, EA), per_b3),
                  pl.BlockSpec((HA, QROWS, wb), c3), pl.BlockSpec((QROWS, wb), c2),
                  pl.BlockSpec((HA, QROWS, QROWS), c3), pl.BlockSpec((QROWS, QROWS), c2),
                  pl.BlockSpec((1, T, EC), per_b3),
                  pl.BlockSpec((1, N_MEM * HC, DC), per_b3), pl.BlockSpec((1, N_MEM * HC, DC), per_b3),
                  pl.BlockSpec((HC * QROWS, N_MEM * HC), c2),
                  pl.BlockSpec((1, NHP, tm, 128), pair_tile), pl.BlockSpec((1, tm, EA), tile),
                  pl.BlockSpec((1, tm, EC), tile), pl.BlockSpec((1, tm, EC), tile),
                  pl.BlockSpec((1, N_MEM, EC), per_seq), pl.BlockSpec((1, N_MEM, EC), per_seq),
                  pl.BlockSpec((1, tm, 3 * D_MODEL), tile), pl.BlockSpec((1, tm, D_MODEL), tile),
                  pl.BlockSpec((EA, D_MODEL), c2, pipeline_mode=once),
                  pl.BlockSpec((EB, D_MODEL), c2, pipeline_mode=once),
                  pl.BlockSpec((EC, D_MODEL), c2, pipeline_mode=once),
                  pl.BlockSpec((D_MODEL, D_MODEL), c2, pipeline_mode=once),
                  pl.BlockSpec((1, D_MODEL), c2)],
        out_specs=(pl.BlockSpec((1, tm, D_MODEL), tile),
                   pl.BlockSpec((1, T, EA), per_b3), pl.BlockSpec((1, T, EC), per_b3)),
        scratch_shapes=[pltpu.VMEM((8, HALO + tm, EB), F32), pltpu.VMEM((tm, EB), F32),
                        pltpu.VMEM((tm, EB), BF16),
                        head_scratch, head_scratch, head_scratch, head_scratch],
        compiler_params=_params("arbitrary"),
        name="prompt_tail_sample_attn",
    )(glu, glu, zb, cw, cb, lg, lb, shift, q, cache_kt, cache_vt, k_new, v_new,
      cache_b, cache_m, new_b, new_m, qc, mem_k, mem_v, own_head,
      oa, za, pqc, pzc, pmk, pmv, gt, x, wpa, wpb, wpc, wout, gpost)


def _conv_sample_kernel(st_ref, glu_ref, cw_ref, cb_ref, lg_ref, lb_ref, o_ref, ns_ref, *, T, rc):
    n_hist = CONV_W - 1
    Bd = st_ref.shape[1]

    def shift(t, carry):
        ns_ref[t] = st_ref[t + T]
        return carry
    lax.fori_loop(0, n_hist - T, shift, 0)
    for i in range(T):
        ns_ref[n_hist - T + i] = glu_ref[i]
    for i in range(T):
        def chunk(c, carry, i=i):
            rows = pl.ds(pl.multiple_of(c * rc, rc), rc)
            acc = jnp.zeros((rc, EB), F32)
            for j in range(CONV_W):
                src = st_ref[i + j, rows, :] if i + j < n_hist else glu_ref[i + j - n_hist, rows, :]
                acc = acc + src * cw_ref[j:j + 1, :]
            o_ref[i, rows, :] = _layernorm_silu(acc + cb_ref[...], lg_ref[...], lb_ref[...])
            return carry
        lax.fori_loop(0, Bd // rc, chunk, 0)


def _conv_sample(state_t, glu_t, cw, cb, lg, lb, rc):
    T, Bd, _ = glu_t.shape
    return pl.pallas_call(
        functools.partial(_conv_sample_kernel, T=T, rc=rc),
        out_shape=(jax.ShapeDtypeStruct((T, Bd, EB), F32), jax.ShapeDtypeStruct(state_t.shape, F32)),
        compiler_params=pltpu.CompilerParams(vmem_limit_bytes=V7X_VMEM_LIMIT),
        name="conv_sample",
    )(state_t, glu_t, cw, cb, lg, lb)


def kernel(x_prompt, x_sample, mem_prompt, cache_k_win, cache_v_win, state_conv, cache_k_mem, cache_v_mem,
           rel_bias, g_pre, w_in, g_mem, w_mem_kv, conv_w, conv_b, ln_g, ln_b, w_proj_a, w_proj_b,
           w_proj_c, w_out, g_post):
    depth = g_pre.shape[0]
    assert depth == 1, "single-layer step"
    B, S, _ = x_prompt.shape
    Bd, T, _ = x_sample.shape
    wb = cache_k_win.shape[2]
    past = wb
    assert wb == MAX_DIST and S % (16 * QB) == 0 and T <= QROWS

    l = 0
    w_in_b = w_in[l].astype(BF16)
    wpa, wpb, wpc = w_proj_a[l].astype(BF16), w_proj_b[l].astype(BF16), w_proj_c[l].astype(BF16)
    wout = w_out[l].astype(BF16)
    gpre, gpost = g_pre[l][None], g_post[l][None]
    cb, lg, lb = conv_b[l][None], ln_g[l][None], ln_b[l][None]
    cw = conv_w[l]

    (qkv, k_keep, v_keep, za, glu, zb, qc, zc, gt, glu_tail,
     q_s, k_s, v_s, za_s, glu_s, zb_s, qc_s, zc_s, gt_s) = _inproj(
        x_prompt, x_sample.reshape(Bd * T, D_MODEL), gpre, w_in_b, tm=TM_INPROJ)
    mk_f, mv_f, mk_b, mv_b = _memkv(mem_prompt.reshape(B * N_MEM, D_MODEL), g_mem[l][None],
                                    w_mem_kv[l].astype(BF16), tm=TM_MEMKV)
    oa = _band_attn(qkv, _band_tables(rel_bias))

    cache_kt = jnp.transpose(cache_k_win[l], (0, 2, 3, 1))
    cache_vt = jnp.transpose(cache_v_win[l], (0, 2, 3, 1))
    cw_tiles = jnp.broadcast_to(cw[:, None, :], (CONV_W, 8, EB))
    y_p, oa_s, oc_s = _prompt_tail_sample_attn(
        glu, zb, cw_tiles, cb, lg, lb,
        q_s.reshape(Bd, T, EA), k_s.reshape(Bd, T, EA), v_s.reshape(Bd, T, EA),
        cache_kt, cache_vt, _sample_tables(rel_bias, wb, past, T),
        qc_s.reshape(Bd, T, EC), cache_k_mem[l].reshape(Bd, N_MEM * HC, DC),
        cache_v_mem[l].reshape(Bd, N_MEM * HC, DC),
        oa, za, qc, zc, mk_b.reshape(B, N_MEM, EC), mv_b.reshape(B, N_MEM, EC), gt, x_prompt,
        wpa, wpb, wpc, wout, gpost, rc=CONV_TAP_ROWS, rn=CONV_NORM_ROWS)

    state_t = jnp.transpose(state_conv[l], (1, 0, 2))
    glu_t = jnp.transpose(glu_s.reshape(Bd, T, EB), (1, 0, 2))
    ob_t, new_state_t = _conv_sample(state_t, glu_t, cw, cb, lg, lb, rc=CONV_TAP_ROWS)
    ob_s = jnp.transpose(ob_t, (1, 0, 2)).reshape(Bd * T, EB)
    y_s = _tail_sample(oa_s.reshape(Bd * T, EA), za_s, ob_s, zb_s,
                       oc_s.reshape(Bd * T, EC), zc_s, gt_s, x_sample.reshape(Bd * T, D_MODEL),
                       wpa, wpb, wpc, wout, gpost, tm=Bd * T)

    conv_state_s = jnp.transpose(new_state_t, (1, 0, 2))
    return (y_p, y_s.reshape(Bd, T, D_MODEL),
            k_keep.reshape(1, B, wb, HA, DA), v_keep.reshape(1, B, wb, HA, DA),
            glu_tail[:, HALO - (CONV_W - 1):][None],
            mk_f.reshape(1, B, N_MEM, HC, DC), mv_f.reshape(1, B, N_MEM, HC, DC),
            k_s.reshape(1, Bd, T, HA, DA), v_s.reshape(1, Bd, T, HA, DA),
            conv_state_s[None])
```
